```python
import jax, jax.numpy as jnp
from jax import lax
import numpy as np

D_MODEL = 4096
BATCH = 1
SEQ = 16384
DEPTH = 2

N_META = 16
MIX_W = D_MODEL
M_HEADS = 4
M_W = MIX_W // 2
M_V = M_W // M_HEADS
M_QK = M_V // 2
M_CHUNK = 64
CONV_W = 4
A_HEADS = 16
A_W = MIX_W - M_W
A_V = A_W // A_HEADS
NOPE = 128
ROPE = 64
Q_LORA = 1536
KV_LORA = 512
Q_BLOCK = 128
ROPE_THETA = 10000.0
NORM_EPS = 1e-6
D_FF = -(-8 * D_MODEL // (3 * 256)) * 256
IN_SIZES = (M_HEADS * M_QK, M_HEADS * M_QK, M_W, M_W, M_HEADS, M_HEADS, Q_LORA, KV_LORA, ROPE)
N_IN = sum(IN_SIZES)
NEG_SCORE = -1e30

kernel_name = 'hybrid_mlstm_mla_meta_block'


def rmsnorm(x, g):
    xf = x.astype(jnp.float32)
    y = xf * lax.rsqrt(jnp.mean(xf * xf, axis=-1, keepdims=True) + NORM_EPS)
    return (y * g.astype(jnp.float32)).astype(x.dtype)


def split_cols(z):
    idx, acc = [], 0
    for s in IN_SIZES[:-1]:
        acc += s
        idx.append(acc)
    return jnp.split(z, idx, axis=-1)


def apply_rope(x, cos, sin):
    half = x.shape[-1] // 2
    x1 = x[..., :half].astype(jnp.float32)
    x2 = x[..., half:].astype(jnp.float32)
    return jnp.concatenate([x1 * cos - x2 * sin, x1 * sin + x2 * cos], axis=-1).astype(x.dtype)


def causal_dwconv(x, w):
    k_w, c = w.shape
    return lax.conv_general_dilated(x, w[:, None, :].astype(x.dtype), window_strides=(1,),
                                    padding=((k_w - 1, 0),), dimension_numbers=('NWC', 'WIO', 'NWC'),
                                    feature_group_count=c)


def mlstm_chunkwise(q, k, v, log_i, log_f):
    B, H, Lp, dk = q.shape
    dv = v.shape[-1]
    nc = Lp // M_CHUNK

    def chunks(t):
        return jnp.moveaxis(t.reshape(B, H, nc, M_CHUNK, *t.shape[3:]), 2, 0)

    tril = jnp.tril(jnp.ones((M_CHUNK, M_CHUNK), dtype=bool))

    def step(carry, xs):
        C, n, m = carry
        qc, kc, vc, li, lf = xs
        qf = qc.astype(jnp.float32)
        kf = kc.astype(jnp.float32)
        vf = vc.astype(jnp.float32)
        b = jnp.cumsum(lf, axis=-1)
        g = b[..., -1]
        d = jnp.where(tril, b[..., :, None] - b[..., None, :] + li[..., None, :], -jnp.inf)
        inter = b + m[..., None]
        m_t = jnp.maximum(inter, jnp.max(d, axis=-1))
        w_inter = jnp.exp(inter - m_t)
        s = jnp.einsum('bhtd,bhsd->bhts', qf, kf) * jnp.exp(d - m_t[..., None])
        num = w_inter[..., None] * jnp.einsum('bhtd,bhde->bhte', qf, C) + jnp.einsum('bhts,bhse->bhte', s, vf)
        den = w_inter * jnp.einsum('bhtd,bhd->bht', qf, n) + jnp.sum(s, axis=-1)
        h = num / jnp.maximum(jnp.abs(den), jnp.exp(-m_t))[..., None]
        a = g[..., None] - b + li
        m_new = jnp.maximum(g + m, jnp.max(a, axis=-1))
        decay = jnp.exp(g + m - m_new)
        wk = kf * jnp.exp(a - m_new[..., None])[..., None]
        C_new = decay[..., None, None] * C + jnp.einsum('bhsd,bhse->bhde', wk, vf)
        n_new = decay[..., None] * n + jnp.sum(wk, axis=2)
        return (C_new, n_new, m_new), h

    init = (jnp.zeros((B, H, dk, dv), jnp.float32), jnp.zeros((B, H, dk), jnp.float32),
            jnp.zeros((B, H), jnp.float32))
    xs = (chunks(q), chunks(k), chunks(v), chunks(log_i), chunks(log_f))
    _, h = lax.scan(step, init, xs)
    return jnp.moveaxis(h, 0, 2).reshape(B, H, Lp, dv)


def mlstm_group(q_in, k_in, v_in, o_pre, i_pre, f_pre, conv_w, b_gates, g_mnorm):
    B, L, _ = v_in.shape
    qk = jax.nn.silu(causal_dwconv(jnp.concatenate([q_in, k_in], axis=-1), conv_w))
    q, k = jnp.split(qk, 2, axis=-1)
    pad = (-L) % M_CHUNK

    def heads(t, dim):
        t = t.reshape(B, L, M_HEADS, dim).transpose(0, 2, 1, 3)
        return jnp.pad(t, ((0, 0), (0, 0), (pad, 0), (0, 0)))

    def gate_pad(t, fill):
        return jnp.pad(jnp.transpose(t, (0, 2, 1)), ((0, 0), (0, 0), (pad, 0)), constant_values=fill)

    log_i = gate_pad(i_pre.astype(jnp.float32) + b_gates[:M_HEADS].astype(jnp.float32), -jnp.inf)
    log_f = gate_pad(jax.nn.log_sigmoid(f_pre.astype(jnp.float32) + b_gates[M_HEADS:].astype(jnp.float32)), 0.0)
    h = mlstm_chunkwise(heads(q, M_QK) * (M_QK ** -0.5), heads(k, M_QK), heads(v_in, M_V), log_i, log_f)
    h = h[:, :, pad:].transpose(0, 2, 1, 3)
    h = rmsnorm(h, g_mnorm.reshape(M_HEADS, M_V))
    return (jax.nn.sigmoid(o_pre.astype(jnp.float32)) * h.reshape(B, L, M_W)).astype(v_in.dtype)


def mla_group(c_q, c_kv, k_rope, g_cq, w_uq, g_ckv, w_ukv, cos, sin):
    B, L, _ = c_q.shape
    q = (rmsnorm(c_q, g_cq) @ w_uq).reshape(B, L, A_HEADS, NOPE + ROPE)
    q = jnp.concatenate([q[..., :NOPE], apply_rope(q[..., NOPE:], cos[:, None, :], sin[:, None, :])], axis=-1)
    kv = (rmsnorm(c_kv, g_ckv) @ w_ukv).reshape(B, L, A_HEADS, NOPE + A_V)
    kr = apply_rope(k_rope, cos, sin)
    k = jnp.concatenate([kv[..., :NOPE], jnp.broadcast_to(kr[:, :, None, :], (B, L, A_HEADS, ROPE))], axis=-1)
    v = kv[..., NOPE:]
    pad = (-L) % Q_BLOCK
    Lp = L + pad
    nb = Lp // Q_BLOCK

    def prep(t):
        return jnp.pad(t, ((0, 0), (pad, 0), (0, 0), (0, 0))).transpose(0, 2, 1, 3)

    q, k, v = prep(q), prep(k), prep(v)
    q_blocks = jnp.moveaxis(q.reshape(B, A_HEADS, nb, Q_BLOCK, NOPE + ROPE), 2, 0)
    key_pos = jnp.arange(Lp)
    key_ok = key_pos >= pad
    scale = (NOPE + ROPE) ** -0.5

    def attend(args):
        qb, start = args
        s = jnp.einsum('bhqd,bhkd->bhqk', qb, k, preferred_element_type=jnp.float32) * scale
        q_pos = start + jnp.arange(Q_BLOCK)
        mask = (key_pos[None, :] <= q_pos[:, None]) & key_ok[None, :]
        p = jax.nn.softmax(jnp.where(mask, s, NEG_SCORE), axis=-1)
        return jnp.einsum('bhqk,bhkd->bhqd', p.astype(v.dtype), v)

    o = lax.map(attend, (q_blocks, jnp.arange(nb) * Q_BLOCK))
    o = jnp.moveaxis(o, 0, 2).reshape(B, A_HEADS, Lp, A_V)[:, :, pad:]
    return o.transpose(0, 2, 1, 3).reshape(B, L, A_W)


def hybrid_layer(x, cos, sin, g_mix_pre, w_in, conv_w, b_gates, g_mnorm, g_cq, w_uq, g_ckv, w_ukv,
                 w_out, g_mix_post, g_ffn_pre, w_gu, w_down, g_ffn_post):
    u = rmsnorm(x, g_mix_pre)
    mq, mk, mv, mo, mi, mf, cq, ckv, kr = split_cols(u @ w_in)
    h_m = mlstm_group(mq, mk, mv, mo, mi, mf, conv_w, b_gates, g_mnorm)
    h_a = mla_group(cq, ckv, kr, g_cq, w_uq, g_ckv, w_ukv, cos, sin)
    mix = jnp.concatenate([h_m, h_a.astype(h_m.dtype)], axis=-1) @ w_out
    x = x + rmsnorm(mix, g_mix_post)
    gate, up = jnp.split(rmsnorm(x, g_ffn_pre) @ w_gu, 2, axis=-1)
    y = (jax.nn.silu(gate) * up) @ w_down
    return x + rmsnorm(y, g_ffn_post)


def setup_inputs(seed: int = 0) -> dict:
    key = jax.random.key(seed)
    ks = jax.random.split(key, 20)
    f32 = jnp.float32

    def nrm(k, shape, scale):
        return jax.random.normal(k, shape, f32) * scale

    def gain(k, shape):
        return 1.0 + 0.05 * jax.random.normal(k, shape, f32)

    f_bias = jnp.linspace(3.0, 6.0, M_HEADS, dtype=f32)[None, :] + 0.1 * jax.random.normal(ks[8], (DEPTH, M_HEADS), f32)
    i_bias = 0.1 * jax.random.normal(ks[9], (DEPTH, M_HEADS), f32)
    return {
        'x': nrm(ks[0], (BATCH, SEQ, D_MODEL), 1.0),
        'meta': nrm(ks[1], (N_META, D_MODEL), 1.0),
        'g_mix_pre': gain(ks[2], (DEPTH, D_MODEL)),
        'w_in': nrm(ks[3], (DEPTH, D_MODEL, N_IN), D_MODEL ** -0.5),
        'conv_w': nrm(ks[4], (DEPTH, CONV_W, 2 * M_HEADS * M_QK), CONV_W ** -0.5),
        'b_gates': jnp.concatenate([i_bias, f_bias], axis=-1),
        'g_mnorm': gain(ks[5], (DEPTH, M_W)),
        'g_cq': gain(ks[6], (DEPTH, Q_LORA)),
        'w_uq': nrm(ks[7], (DEPTH, Q_LORA, A_HEADS * (NOPE + ROPE)), Q_LORA ** -0.5),
        'g_ckv': gain(ks[10], (DEPTH, KV_LORA)),
        'w_ukv': nrm(ks[11], (DEPTH, KV_LORA, A_HEADS * (NOPE + A_V)), KV_LORA ** -0.5),
        'w_out': nrm(ks[12], (DEPTH, MIX_W, D_MODEL), MIX_W ** -0.5),
        'g_mix_post': gain(ks[13], (DEPTH, D_MODEL)),
        'g_ffn_pre': gain(ks[14], (DEPTH, D_MODEL)),
        'w_gu': nrm(ks[15], (DEPTH, D_MODEL, 2 * D_FF), D_MODEL ** -0.5),
        'w_down': nrm(ks[16], (DEPTH, D_FF, D_MODEL), D_FF ** -0.5),
        'g_ffn_post': gain(ks[17], (DEPTH, D_MODEL)),
    }


def reference(x, meta, g_mix_pre, w_in, conv_w, b_gates, g_mnorm, g_cq, w_uq, g_ckv, w_ukv,
              w_out, g_mix_post, g_ffn_pre, w_gu, w_down, g_ffn_post):
    B = x.shape[0]
    h = jnp.concatenate([jnp.broadcast_to(meta[None].astype(x.dtype), (B, N_META, D_MODEL)), x], axis=1)
    L = h.shape[1]
    pos = jnp.arange(L, dtype=jnp.float32)
    inv_freq = ROPE_THETA ** (-jnp.arange(ROPE // 2, dtype=jnp.float32) / (ROPE // 2))
    ang = pos[:, None] * inv_freq[None, :]
    cos, sin = jnp.cos(ang), jnp.sin(ang)
    for l in range(DEPTH):
        h = hybrid_layer(h, cos, sin, g_mix_pre[l], w_in[l], conv_w[l], b_gates[l], g_mnorm[l], g_cq[l],
                         w_uq[l], g_ckv[l], w_ukv[l], w_out[l], g_mix_post[l], g_ffn_pre[l], w_gu[l],
                         w_down[l], g_ffn_post[l])
    return h[:, N_META:]
```

```python
import functools

import jax
import jax.numpy as jnp
from jax import lax
from jax.experimental import pallas as pl
from jax.experimental.pallas import tpu as pltpu

N_META = 16
M_HEADS = 4
CONV_W = 4
A_HEADS = 16
NOPE = 128
ROPE = 64
Q_LORA = 1536
KV_LORA = 512
ROPE_THETA = 10000.0
NORM_EPS = 1e-6
NEG_SCORE = -1e30

T0 = 256
HALO = 8
QK_W = NOPE + 2 * ROPE
GATE_W = 128
ROPE_W = 2 * ROPE

VMEM_LIMIT_BYTES = 56 * 1024 * 1024
MM_SUB_ROWS = 512

F32 = jnp.float32
BF16 = jnp.bfloat16


def _div_tile(n, target, mult):
    best = None
    t = mult
    while t <= min(n, target):
        if n % t == 0:
            best = t
        t += mult
    if best is None:
        raise ValueError(f"no tile for {n} (target {target}, multiple of {mult})")
    return best


def _params(sem):
    return pltpu.CompilerParams(dimension_semantics=sem, vmem_limit_bytes=VMEM_LIMIT_BYTES)


def _rms(t, g):
    return t * lax.rsqrt(jnp.mean(t * t, axis=-1, keepdims=True) + NORM_EPS) * g


def _prep_body(x_ref, meta_ref, g_ref, h_ref, u_ref):
    i = pl.program_id(0)

    @pl.when(i == 0)
    def _():
        h_ref[...] = jnp.zeros_like(h_ref)
        h_ref[T0 - N_META:T0, :] = meta_ref[...]

    @pl.when(i > 0)
    def _():
        h_ref[...] = x_ref[...]

    u_ref[...] = _rms(h_ref[...], g_ref[...]).astype(BF16)


def _prep(x2d, meta, g):
    seq, d = x2d.shape
    lp = T0 + seq
    return pl.pallas_call(
        _prep_body,
        grid=(lp // T0,),
        in_specs=[
            pl.BlockSpec((T0, d), lambda i: (jnp.maximum(i - 1, 0), 0)),
            pl.BlockSpec((N_META, d), lambda i: (0, 0)),
            pl.BlockSpec((1, d), lambda i: (0, 0)),
        ],
        out_specs=[
            pl.BlockSpec((T0, d), lambda i: (i, 0)),
            pl.BlockSpec((T0, d), lambda i: (i, 0)),
        ],
        out_shape=[jax.ShapeDtypeStruct((lp, d), F32), jax.ShapeDtypeStruct((lp, d), BF16)],
        compiler_params=_params(("parallel",)),
        name="prep_norm",
    )(x2d, meta, g)


def _row_loop(nrows, sub, fn):
    def step(r, carry):
        fn(pl.ds(pl.multiple_of(r * sub, sub), sub))
        return carry
    lax.fori_loop(0, nrows // sub, step, 0)


def _mm_body(*refs, n_pairs, nk, swiglu, bn, bm, sub):
    lhs = refs[0:2 * n_pairs:2]
    rhs = refs[1:2 * n_pairs:2]
    o_ref = refs[2 * n_pairs]

    def partial_sum(rows):
        acc = None
        for a, b in zip(lhs, rhs):
            d = jnp.dot(a[rows, :], b[...], preferred_element_type=F32)
            acc = d if acc is None else acc + d
        return acc

    def finish(rows):
        acc = partial_sum(rows)
        if swiglu:
            gate = acc[:, :bn]
            acc = gate * jax.nn.sigmoid(gate) * acc[:, bn:]
        o_ref[rows, :] = acc.astype(o_ref.dtype)

    if nk == 1:
        _row_loop(bm, sub, finish)
        return

    k = pl.program_id(2)

    def first(rows):
        o_ref[rows, :] = partial_sum(rows)

    def accumulate(rows):
        o_ref[rows, :] += partial_sum(rows)

    @pl.when(k == 0)
    def _():
        _row_loop(bm, sub, first)

    @pl.when(k > 0)
    def _():
        _row_loop(bm, sub, accumulate)


def _matmul(pairs, out_dtype, *, bm, bn, bk=None, swiglu=False, name):
    m = pairs[0][0].shape[0]
    n_rhs = pairs[0][1].shape[1]
    n = n_rhs // 2 if swiglu else n_rhs
    bnw = 2 * bn if swiglu else bn
    if bk is None:
        nk = 1
    else:
        assert len(pairs) == 1 and not swiglu and out_dtype == F32
        nk = pairs[0][0].shape[1] // bk
    in_specs = []
    args = []
    for a, b in pairs:
        kk = a.shape[1] if bk is None else bk
        in_specs.append(pl.BlockSpec((bm, kk), lambda i, j, k: (i, k)))
        in_specs.append(pl.BlockSpec((kk, bnw), lambda i, j, k: (k, j)))
        args += [a, b]
    return pl.pallas_call(
        functools.partial(_mm_body, n_pairs=len(pairs), nk=nk, swiglu=swiglu, bn=bn, bm=bm,
                          sub=_div_tile(bm, MM_SUB_ROWS, 16)),
        grid=(m // bm, n // bn, nk),
        in_specs=in_specs,
        out_specs=pl.BlockSpec((bm, bn), lambda i, j, k: (i, j)),
        out_shape=jax.ShapeDtypeStruct((m, n), out_dtype),
        compiler_params=_params(("parallel", "parallel", "arbitrary")),
        name=name,
    )(*args)


def _resnorm_body(x_ref, y_ref, gp_ref, gn_ref, xo_ref, u_ref, *, rows):
    i = pl.program_id(0)
    row = i * rows + lax.broadcasted_iota(jnp.int32, (rows, 1), 0)
    xn = x_ref[...] + _rms(y_ref[...], gp_ref[...])
    xn = jnp.where(row >= T0 - N_META, xn, 0.0)
    xo_ref[...] = xn
    u_ref[...] = _rms(xn, gn_ref[...]).astype(BF16)


def _resnorm(x, y, g_post, g_next):
    lp, d = x.shape
    rows = T0
    blk = pl.BlockSpec((rows, d), lambda i: (i, 0))
    vec = pl.BlockSpec((1, d), lambda i: (0, 0))
    return pl.pallas_call(
        functools.partial(_resnorm_body, rows=rows),
        grid=(lp // rows,),
        in_specs=[blk, blk, vec, vec],
        out_specs=[blk, blk],
        out_shape=[jax.ShapeDtypeStruct((lp, d), F32), jax.ShapeDtypeStruct((lp, d), BF16)],
        compiler_params=_params(("parallel",)),
        name="resnorm",
    )(x, y, g_post, g_next)


def _resnorm_final_body(x_ref, y_ref, gp_ref, o_ref):
    o_ref[...] = x_ref[...] + _rms(y_ref[...], gp_ref[...])


def _resnorm_final(x, y, g_post):
    lp, d = x.shape
    rows = T0
    skip = T0 // rows
    blk_in = pl.BlockSpec((rows, d), lambda i: (i + skip, 0))
    return pl.pallas_call(
        _resnorm_final_body,
        grid=((lp - T0) // rows,),
        in_specs=[blk_in, blk_in, pl.BlockSpec((1, d), lambda i: (0, 0))],
        out_specs=pl.BlockSpec((rows, d), lambda i: (i, 0)),
        out_shape=jax.ShapeDtypeStruct((lp - T0, d), F32),
        compiler_params=_params(("parallel",)),
        name="resnorm_final",
    )(x, y, g_post)


def _log_sigmoid(x):
    return jnp.minimum(x, 0.0) - jnp.log1p(jnp.exp(-jnp.abs(x)))


def _mlstm_body(q_ref, k_ref, v_ref, o_ref, gc_ref, gr_ref, cw_ref, bc_ref, br_ref, gn_ref,
                out_ref, xext, c_sc, n_sc, m_sc, *, heads, dk, dv, lc):
    c = pl.program_id(0)
    hk = heads * dk
    meta0 = T0 - N_META

    @pl.when(c == 0)
    def _():
        xext[0:HALO, :] = jnp.zeros((HALO, 2 * hk), F32)
        c_sc[...] = jnp.zeros_like(c_sc)
        n_sc[...] = jnp.zeros_like(n_sc)
        m_sc[...] = jnp.zeros_like(m_sc)

    xext[HALO:HALO + lc, 0:hk] = q_ref[...]
    xext[HALO:HALO + lc, hk:2 * hk] = k_ref[...]
    conv = None
    for j in range(CONV_W):
        term = cw_ref[j:j + 1, :] * xext[pl.ds(HALO - (CONV_W - 1) + j, lc), :]
        conv = term if conv is None else conv + term
    xext[0:HALO, :] = xext[lc:lc + HALO, :]
    qk = conv * jax.nn.sigmoid(conv)

    row = c * lc + lax.broadcasted_iota(jnp.int32, (lc, 1), 0)
    col = c * lc + lax.broadcasted_iota(jnp.int32, (1, lc), 1)
    valid_c = row >= meta0
    valid_r = col >= meta0
    gc = gc_ref[...] + bc_ref[...]
    gr = gr_ref[...] + br_ref[...]
    tt = lax.broadcasted_iota(jnp.int32, (lc, lc), 0)
    ss = lax.broadcasted_iota(jnp.int32, (lc, lc), 1)
    tril = ss <= tt
    triu = tt <= ss
    neg_inf = -jnp.inf

    for h in range(heads):
        li_c = jnp.where(valid_c, gc[:, h:h + 1], neg_inf)
        lf_c = jnp.where(valid_c, _log_sigmoid(gc[:, heads + h:heads + h + 1]), 0.0)
        li_r = jnp.where(valid_r, gr[h:h + 1, :], neg_inf)
        lf_r = jnp.where(valid_r, _log_sigmoid(gr[heads + h:heads + h + 1, :]), 0.0)
        b_c = jnp.sum(jnp.where(tril, lf_r, 0.0), axis=1, keepdims=True)
        b_r = jnp.sum(jnp.where(triu, lf_c, 0.0), axis=0, keepdims=True)
        g = jnp.sum(lf_r, axis=1, keepdims=True)
        m = m_sc[h:h + 1, 0:1]

        d = jnp.where(tril, b_c - b_r + li_r, neg_inf)
        inter = b_c + m
        m_t = jnp.maximum(inter, jnp.max(d, axis=1, keepdims=True))
        w_inter = jnp.exp(inter - m_t)
        p = jnp.exp(d - m_t)

        qh = qk[:, h * dk:(h + 1) * dk] * (dk ** -0.5)
        kh = qk[:, hk + h * dk:hk + (h + 1) * dk]
        qb = qh.astype(BF16)
        s = lax.dot_general(qb, kh.astype(BF16), (((1,), (1,)), ((), ())),
                            preferred_element_type=F32) * p
        vh = v_ref[:, h * dv:(h + 1) * dv]
        ch = c_sc[h]
        nh = n_sc[h:h + 1, :]
        num = (w_inter * jnp.dot(qb, ch.astype(BF16), preferred_element_type=F32)
               + jnp.dot(s.astype(BF16), vh, preferred_element_type=F32))
        den = (w_inter * jnp.sum(qh * nh, axis=1, keepdims=True)
               + jnp.sum(s, axis=1, keepdims=True))
        hh = num / jnp.maximum(jnp.abs(den), jnp.exp(-m_t))

        a_c = g - b_c + li_c
        m_new = jnp.maximum(g + m, jnp.max(a_c, axis=0, keepdims=True))
        decay = jnp.exp(g + m - m_new)
        wk = kh * jnp.exp(a_c - m_new)
        c_sc[h] = decay * ch + lax.dot_general(wk.astype(BF16), vh, (((0,), (0,)), ((), ())),
                                               preferred_element_type=F32)
        n_sc[h:h + 1, :] = decay * nh + jnp.sum(wk, axis=0, keepdims=True)
        m_sc[h:h + 1, :] = jnp.broadcast_to(m_new, (1, m_sc.shape[1]))

        hn = _rms(hh, gn_ref[:, h * dv:(h + 1) * dv])
        og = jax.nn.sigmoid(o_ref[:, h * dv:(h + 1) * dv])
        out_ref[:, h * dv:(h + 1) * dv] = (og * hn).astype(BF16)


def _mlstm(a_qko, v, b_misc, gates_t, conv_w, bias_c, bias_r, g_mnorm, *, heads, dk, dv, gate_blk):
    lp = v.shape[0]
    lc = T0
    hk = heads * dk
    mw = heads * dv
    assert 2 * heads <= 8 and 2 * hk == mw
    return pl.pallas_call(
        functools.partial(_mlstm_body, heads=heads, dk=dk, dv=dv, lc=lc),
        grid=(lp // lc,),
        in_specs=[
            pl.BlockSpec((lc, hk), lambda c: (c, 0)),
            pl.BlockSpec((lc, hk), lambda c: (c, 1)),
            pl.BlockSpec((lc, mw), lambda c: (c, 0)),
            pl.BlockSpec((lc, mw), lambda c: (c, 1)),
            pl.BlockSpec((lc, GATE_W), lambda c: (c, gate_blk)),
            pl.BlockSpec((2 * heads, lc), lambda c: (0, c)),
            pl.BlockSpec((CONV_W, 2 * hk), lambda c: (0, 0)),
            pl.BlockSpec((1, GATE_W), lambda c: (0, 0)),
            pl.BlockSpec((2 * heads, 1), lambda c: (0, 0)),
            pl.BlockSpec((1, mw), lambda c: (0, 0)),
        ],
        out_specs=pl.BlockSpec((lc, mw), lambda c: (c, 0)),
        out_shape=jax.ShapeDtypeStruct((lp, mw), BF16),
        scratch_shapes=[
            pltpu.VMEM((lc + HALO, 2 * hk), F32),
            pltpu.VMEM((heads, dk, dv), F32),
            pltpu.VMEM((8, dk), F32),
            pltpu.VMEM((8, 128), F32),
        ],
        compiler_params=_params(("arbitrary",)),
        name="mlstm",
    )(a_qko, a_qko, v, a_qko, b_misc, gates_t, conv_w, bias_c, bias_r, g_mnorm)


def _rope128(t, tab):
    pr = t * tab
    rr = pr + pltpu.roll(pr, ROPE, axis=1)
    lane = lax.broadcasted_iota(jnp.int32, pr.shape, 1)
    return jnp.where(lane < ROPE, rr, 0.0)


def _qproj_body(c_ref, g_ref, w_ref, tab_ref, o_ref, cn, *, hpb, scale, bm, sub):
    @pl.when(pl.program_id(1) == 0)
    def _():
        def norm(rows):
            cn[rows, :] = _rms(c_ref[rows, :], g_ref[...]).astype(BF16)
        _row_loop(bm, sub, norm)

    def project(rows):
        acc = jnp.dot(cn[rows, :], w_ref[...], preferred_element_type=F32)
        tab = tab_ref[rows, :]
        for hh in range(hpb):
            base = hh * QK_W
            o_ref[rows, base:base + NOPE] = (acc[:, base:base + NOPE] * scale).astype(BF16)
            rr = _rope128(acc[:, base + NOPE:base + QK_W], tab)
            o_ref[rows, base + NOPE:base + QK_W] = (rr * scale).astype(BF16)
    _row_loop(bm, sub, project)


def _kvproj_body(c_ref, g_ref, w_ref, kr_ref, tab_ref, k_ref, v_ref, cn, *, hpb, av, bm, sub):
    @pl.when(pl.program_id(1) == 0)
    def _():
        def norm(rows):
            cn[rows, :] = _rms(c_ref[rows, :], g_ref[...]).astype(BF16)
        _row_loop(bm, sub, norm)

    def project(rows):
        acc = jnp.dot(cn[rows, :], w_ref[...], preferred_element_type=F32)
        kr = _rope128(kr_ref[rows, :], tab_ref[rows, :]).astype(BF16)
        for hh in range(hpb):
            src = hh * (NOPE + av)
            k_ref[rows, hh * QK_W:hh * QK_W + NOPE] = acc[:, src:src + NOPE].astype(BF16)
            k_ref[rows, hh * QK_W + NOPE:(hh + 1) * QK_W] = kr
            v_ref[rows, hh * av:(hh + 1) * av] = acc[:, src + NOPE:src + NOPE + av].astype(BF16)
    _row_loop(bm, sub, project)


def _qproj(b_misc, g_cq, w_uq_r, tab, *, bm, hpb, scale):
    lp = b_misc.shape[0]
    cw = w_uq_r.shape[0]
    n = w_uq_r.shape[1]
    bn = hpb * QK_W
    return pl.pallas_call(
        functools.partial(_qproj_body, hpb=hpb, scale=scale, bm=bm, sub=_div_tile(bm, MM_SUB_ROWS, 16)),
        grid=(lp // bm, n // bn),
        in_specs=[
            pl.BlockSpec((bm, cw), lambda i, j: (i, 0)),
            pl.BlockSpec((1, cw), lambda i, j: (0, 0)),
            pl.BlockSpec((cw, bn), lambda i, j: (0, j)),
            pl.BlockSpec((bm, ROPE_W), lambda i, j: (i, 0)),
        ],
        out_specs=pl.BlockSpec((bm, bn), lambda i, j: (i, j)),
        out_shape=jax.ShapeDtypeStruct((lp, n), BF16),
        scratch_shapes=[pltpu.VMEM((bm, cw), BF16)],
        compiler_params=_params(("parallel", "arbitrary")),
        name="mla_qproj",
    )(b_misc, g_cq, w_uq_r, tab)


def _kvproj(b_misc, g_ckv, w_ukv, tab, *, bm, hpb, av, ckv_blk, kr_blk):
    lp = b_misc.shape[0]
    cw = w_ukv.shape[0]
    n = w_ukv.shape[1]
    heads = n // (NOPE + av)
    bn = hpb * (NOPE + av)
    return pl.pallas_call(
        functools.partial(_kvproj_body, hpb=hpb, av=av, bm=bm, sub=_div_tile(bm, MM_SUB_ROWS, 16)),
        grid=(lp // bm, n // bn),
        in_specs=[
            pl.BlockSpec((bm, cw), lambda i, j: (i, ckv_blk)),
            pl.BlockSpec((1, cw), lambda i, j: (0, 0)),
            pl.BlockSpec((cw, bn), lambda i, j: (0, j)),
            pl.BlockSpec((bm, ROPE_W), lambda i, j: (i, kr_blk)),
            pl.BlockSpec((bm, ROPE_W), lambda i, j: (i, 0)),
        ],
        out_specs=[
            pl.BlockSpec((bm, hpb * QK_W), lambda i, j: (i, j)),
            pl.BlockSpec((bm, hpb * av), lambda i, j: (i, j)),
        ],
        out_shape=[jax.ShapeDtypeStruct((lp, heads * QK_W), BF16),
                   jax.ShapeDtypeStruct((lp, heads * av), BF16)],
        scratch_shapes=[pltpu.VMEM((bm, cw), BF16)],
        compiler_params=_params(("parallel", "arbitrary")),
        name="mla_kvproj",
    )(b_misc, g_ckv, w_ukv, b_misc, tab)


def _flash_body(qi_ref, kj_ref, q_ref, k_ref, v_ref, o_ref, m_sc, l_sc, acc_sc, *, tile, sub):
    p = pl.program_id(1)
    qi = qi_ref[p]
    kj = kj_ref[p]
    meta0 = T0 - N_META

    @pl.when(kj == 0)
    def _():
        m_sc[...] = jnp.full_like(m_sc, NEG_SCORE)
        l_sc[...] = jnp.zeros_like(l_sc)
        acc_sc[...] = jnp.zeros_like(acc_sc)

    def update(rows, masked):
        sc = lax.dot_general(q_ref[rows, :], k_ref[...], (((1,), (1,)), ((), ())),
                             preferred_element_type=F32)
        if masked:
            qpos = qi * tile + rows.start + lax.broadcasted_iota(jnp.int32, (sub, tile), 0)
            kpos = kj * tile + lax.broadcasted_iota(jnp.int32, (sub, tile), 1)
            keep = jnp.logical_and(kpos <= qpos, kpos >= meta0)
            sc = jnp.where(keep, sc, NEG_SCORE)
        m_prev = m_sc[rows, :]
        m_new = jnp.maximum(m_prev, jnp.max(sc, axis=1, keepdims=True))
        alpha = jnp.exp(m_prev - m_new)
        pm = jnp.exp(sc - m_new)
        l_sc[rows, :] = alpha * l_sc[rows, :] + jnp.sum(pm, axis=1, keepdims=True)
        acc_sc[rows, :] = alpha * acc_sc[rows, :] + jnp.dot(pm.astype(BF16), v_ref[...],
                                                            preferred_element_type=F32)
        m_sc[rows, :] = m_new

    needs_mask = jnp.logical_or(kj == qi, kj == 0)

    @pl.when(needs_mask)
    def _():
        _row_loop(tile, sub, functools.partial(update, masked=True))

    @pl.when(jnp.logical_not(needs_mask))
    def _():
        _row_loop(tile, sub, functools.partial(update, masked=False))

    @pl.when(kj == qi)
    def _():
        o_ref[...] = (acc_sc[...] / l_sc[...]).astype(o_ref.dtype)


def _flash(q, k, v, *, heads, av, tile):
    lp = q.shape[0]
    nt = lp // tile
    pairs = [(i, j) for i in range(nt) for j in range(i + 1)]
    qi = jnp.asarray([p[0] for p in pairs], jnp.int32)
    kj = jnp.asarray([p[1] for p in pairs], jnp.int32)
    grid_spec = pltpu.PrefetchScalarGridSpec(
        num_scalar_prefetch=2,
        grid=(heads, len(pairs)),
        in_specs=[
            pl.BlockSpec((tile, QK_W), lambda h, p, qi, kj: (qi[p], h)),
            pl.BlockSpec((tile, QK_W), lambda h, p, qi, kj: (kj[p], h)),
            pl.BlockSpec((tile, av), lambda h, p, qi, kj: (kj[p], h)),
        ],
        out_specs=pl.BlockSpec((tile, av), lambda h, p, qi, kj: (qi[p], h)),
        scratch_shapes=[
            pltpu.VMEM((tile, 1), F32),
            pltpu.VMEM((tile, 1), F32),
            pltpu.VMEM((tile, av), F32),
        ],
    )
    return pl.pallas_call(
        functools.partial(_flash_body, tile=tile, sub=_div_tile(tile, 256, 16)),
        grid_spec=grid_spec,
        out_shape=jax.ShapeDtypeStruct((lp, heads * av), BF16),
        compiler_params=_params(("parallel", "arbitrary")),
        name="mla_attention",
    )(qi, kj, q, k, v)


def _swap_half(w):
    half = w.shape[-1] // 2
    return jnp.concatenate([-w[..., half:], w[..., :half]], axis=-1)


def _prep_layer_weights(w_in, w_uq, w_ukv, w_out, w_gu, w_down, *, mw, hk, ffp, bf):
    d = w_in.shape[0]
    o = 0
    sizes = (hk, hk, mw, mw, M_HEADS, M_HEADS, Q_LORA, KV_LORA, ROPE)
    parts = []
    for s in sizes:
        parts.append(w_in[:, o:o + s])
        o += s
    wq, wk, wv, wo, wi, wf, wcq, wckv, wkr = parts
    gate_pad = jnp.zeros((d, GATE_W - 2 * M_HEADS), w_in.dtype)
    w_a = jnp.concatenate([wq, wk, wo], axis=1).astype(BF16)
    w_b = jnp.concatenate([wcq, wckv, wi, wf, gate_pad, wkr, _swap_half(wkr)], axis=1).astype(BF16)
    w_v = wv.astype(BF16)

    uq = w_uq.reshape(Q_LORA, A_HEADS, NOPE + ROPE)
    uq_r = jnp.concatenate([uq[..., :NOPE], uq[..., NOPE:], _swap_half(uq[..., NOPE:])], axis=-1)
    uq_r = uq_r.reshape(Q_LORA, A_HEADS * QK_W).astype(BF16)

    ff = w_down.shape[0]
    wg = jnp.pad(w_gu[:, :ff], ((0, 0), (0, ffp - ff))).reshape(d, ffp // bf, 1, bf)
    wu = jnp.pad(w_gu[:, ff:], ((0, 0), (0, ffp - ff))).reshape(d, ffp // bf, 1, bf)
    w_gu_r = jnp.concatenate([wg, wu], axis=2).reshape(d, 2 * ffp).astype(BF16)
    w_down_r = jnp.pad(w_down, ((0, ffp - ff), (0, 0))).astype(BF16)
    return dict(w_a=w_a, w_b=w_b, w_v=w_v, uq=uq_r, ukv=w_ukv.astype(BF16),
                w_out_m=w_out[:mw].astype(BF16), w_out_a=w_out[mw:].astype(BF16),
                w_gu=w_gu_r, w_down=w_down_r)


def _rope_table(lp):
    meta0 = T0 - N_META
    pos = jnp.maximum(jnp.arange(lp, dtype=jnp.int32) - meta0, 0).astype(F32)
    inv_freq = ROPE_THETA ** (-jnp.arange(ROPE // 2, dtype=F32) / (ROPE // 2))
    ang = pos[:, None] * inv_freq[None, :]
    cos, sin = jnp.cos(ang), jnp.sin(ang)
    return jnp.concatenate([cos, cos, sin, sin], axis=-1)


def kernel(x, meta, g_mix_pre, w_in, conv_w, b_gates, g_mnorm, g_cq, w_uq, g_ckv, w_ukv, w_out,
           g_mix_post, g_ffn_pre, w_gu, w_down, g_ffn_post):
    batch, seq, d = x.shape
    assert batch == 1 and seq % T0 == 0
    depth = w_in.shape[0]
    lp = T0 + seq
    mw = d // 2
    dv = mw // M_HEADS
    dk = dv // 2
    hk = M_HEADS * dk
    aw = d - mw
    av = aw // A_HEADS
    ff = w_down.shape[1]
    assert Q_LORA % KV_LORA == 0 and KV_LORA % ROPE_W == 0 and ROPE_W == GATE_W

    bf = 512
    ffp = -(-ff // 1024) * 1024
    bm = _div_tile(lp, 1664, 128)
    tile = _div_tile(lp, 1280, 256)
    scale = (NOPE + ROPE) ** -0.5

    ckv_blk = Q_LORA // KV_LORA
    gate_blk = (Q_LORA + KV_LORA) // GATE_W
    kr_blk = (Q_LORA + KV_LORA + GATE_W) // ROPE_W
    nb = Q_LORA + KV_LORA + GATE_W + ROPE_W

    tab = _rope_table(lp)
    h, u = _prep(x[0], meta, g_mix_pre[0][None])

    out = None
    for l in range(depth):
        w = _prep_layer_weights(w_in[l], w_uq[l], w_ukv[l], w_out[l], w_gu[l], w_down[l],
                                mw=mw, hk=hk, ffp=ffp, bf=bf)
        v_m = _matmul([(u, w["w_v"])], BF16, bm=bm, bn=_div_tile(mw, 512, 128), name="inproj_v")
        a_qko = _matmul([(u, w["w_a"])], F32, bm=bm, bn=_div_tile(2 * hk + mw, 512, 128), name="inproj_qko")
        b_misc = _matmul([(u, w["w_b"])], F32, bm=bm, bn=_div_tile(nb, 768, 128), name="inproj_misc")

        gates_t = b_misc[:, Q_LORA + KV_LORA:Q_LORA + KV_LORA + 2 * M_HEADS].T
        bias_c = jnp.pad(b_gates[l], (0, GATE_W - 2 * M_HEADS))[None, :]
        bias_r = b_gates[l][:, None]
        h_m = _mlstm(a_qko, v_m, b_misc, gates_t, conv_w[l], bias_c, bias_r, g_mnorm[l][None],
                     heads=M_HEADS, dk=dk, dv=dv, gate_blk=gate_blk)

        bm_a = _div_tile(lp, 832, 64)
        q_a = _qproj(b_misc, g_cq[l][None], w["uq"], tab, bm=bm_a, hpb=4, scale=scale)
        k_a, v_a = _kvproj(b_misc, g_ckv[l][None], w["ukv"], tab, bm=bm_a, hpb=4, av=av,
                           ckv_blk=ckv_blk, kr_blk=kr_blk)
        h_a = _flash(q_a, k_a, v_a, heads=A_HEADS, av=av, tile=tile)

        mix = _matmul([(h_m, w["w_out_m"]), (h_a, w["w_out_a"])], F32, bm=bm,
                      bn=_div_tile(d, 512, 128), name="outproj")
        h, u = _resnorm(h, mix, g_mix_post[l][None], g_ffn_pre[l][None])

        act = _matmul([(u, w["w_gu"])], BF16, bm=_div_tile(lp, 832, 64), bn=bf, swiglu=True, name="ffn_up")
        y = _matmul([(act, w["w_down"])], F32, bm=bm, bn=_div_tile(d, 2048, 128), bk=1024, name="ffn_down")
        if l + 1 < depth:
            h, u = _resnorm(h, y, g_ffn_post[l][None], g_mix_pre[l + 1][None])
        else:
            out = _resnorm_final(h, y, g_ffn_post[l][None])
    return out[None]
```

```python
import functools

import jax
import jax.numpy as jnp
from jax import lax
from jax.experimental import pallas as pl
from jax.experimental.pallas import tpu as pltpu

N_META = 16
M_HEADS = 4
CONV_W = 4
A_HEADS = 16
NOPE = 128
ROPE = 64
Q_LORA = 1536
KV_LORA = 512
ROPE_THETA = 10000.0
NORM_EPS = 1e-6
NEG_SCORE = -1e30
LOG2_E = 1.4426950408889634

T0 = 512
MLSTM_CHUNK = 256
ROW_BLOCK = 256
HALO = 8
QK_W = NOPE + 2 * ROPE
GATE_W = 128
ROPE_W = 2 * ROPE

VMEM_LIMIT_BYTES = 56 * 1024 * 1024
MM_SUB_ROWS = 512
ATT_SLAB = 512
ATT_CHUNK = 768
ATT_TILE = 2 * ATT_CHUNK

F32 = jnp.float32
BF16 = jnp.bfloat16


def _div_tile(n, target, mult):
    best = None
    t = mult
    while t <= min(n, target):
        if n % t == 0:
            best = t
        t += mult
    if best is None:
        raise ValueError(f"no tile for {n} (target {target}, multiple of {mult})")
    return best


def _params(sem, flags=None):
    return pltpu.CompilerParams(dimension_semantics=sem, vmem_limit_bytes=VMEM_LIMIT_BYTES, flags=flags)


def _rms(t, g):
    return t * lax.rsqrt(jnp.mean(t * t, axis=-1, keepdims=True) + NORM_EPS) * g


def _row_loop(nrows, sub, fn):
    def step(r, carry):
        fn(pl.ds(pl.multiple_of(r * sub, sub), sub))
        return carry
    lax.fori_loop(0, nrows // sub, step, 0)


def _prep_body(x_ref, meta_ref, g_ref, h_ref, u_ref):
    i = pl.program_id(0)

    @pl.when(i == 0)
    def _():
        h_ref[...] = jnp.zeros_like(h_ref)
        h_ref[T0 - N_META:T0, :] = meta_ref[...]

    @pl.when(i > 0)
    def _():
        h_ref[...] = x_ref[...]

    u_ref[...] = _rms(h_ref[...], g_ref[...]).astype(BF16)


def _prep(x2d, meta, g):
    seq, d = x2d.shape
    lp = T0 + seq
    return pl.pallas_call(
        _prep_body,
        grid=(lp // T0,),
        in_specs=[
            pl.BlockSpec((T0, d), lambda i: (jnp.maximum(i - 1, 0), 0)),
            pl.BlockSpec((N_META, d), lambda i: (0, 0)),
            pl.BlockSpec((1, d), lambda i: (0, 0)),
        ],
        out_specs=[
            pl.BlockSpec((T0, d), lambda i: (i, 0)),
            pl.BlockSpec((T0, d), lambda i: (i, 0)),
        ],
        out_shape=[jax.ShapeDtypeStruct((lp, d), F32), jax.ShapeDtypeStruct((lp, d), BF16)],
        compiler_params=_params(("parallel",)),
        name="prep_norm",
    )(x2d, meta, g)


def _mm_body(*refs, n_lhs, nk, swiglu, bm, sub):
    lhs = refs[0:n_lhs]
    rhs = refs[n_lhs:-1]
    o_ref = refs[-1]

    def product(rows, b):
        acc = None
        for a, w in zip(lhs, b):
            d = jnp.dot(a[rows, :], w[...], preferred_element_type=F32)
            acc = d if acc is None else acc + d
        return acc

    def finish(rows):
        if swiglu:
            gate = product(rows, rhs[0:1])
            acc = gate * jax.nn.sigmoid(gate) * product(rows, rhs[1:2])
        else:
            acc = product(rows, rhs)
        o_ref[rows, :] = acc.astype(o_ref.dtype)

    if nk == 1:
        _row_loop(bm, sub, finish)
        return

    k = pl.program_id(2)

    def first(rows):
        o_ref[rows, :] = product(rows, rhs)

    def accumulate(rows):
        o_ref[rows, :] += product(rows, rhs)

    @pl.when(k == 0)
    def _():
        _row_loop(bm, sub, first)

    @pl.when(k > 0)
    def _():
        _row_loop(bm, sub, accumulate)


def _matmul(lhs, rhs, n, out_dtype, *, bm, bn, bk=None, swiglu=False, name):
    m = lhs[0].shape[0]
    if bk is None:
        nk = 1
    else:
        assert len(lhs) == 1 and not swiglu and out_dtype == F32
        nk = lhs[0].shape[1] // bk
    in_specs = []
    for a in lhs:
        kk = a.shape[1] if bk is None else bk
        in_specs.append(pl.BlockSpec((bm, kk), lambda i, j, k: (i, k)))
    for p, (w, rb, cb) in enumerate(rhs):
        a = lhs[0] if swiglu else lhs[p]
        kk = a.shape[1] if bk is None else bk
        in_specs.append(pl.BlockSpec((kk, bn), lambda i, j, k, rb=rb, cb=cb: (k + rb, j + cb)))
    return pl.pallas_call(
        functools.partial(_mm_body, n_lhs=len(lhs), nk=nk, swiglu=swiglu, bm=bm,
                          sub=_div_tile(bm, MM_SUB_ROWS, 16)),
        grid=(m // bm, n // bn, nk),
        in_specs=in_specs,
        out_specs=pl.BlockSpec((bm, bn), lambda i, j, k: (i, j)),
        out_shape=jax.ShapeDtypeStruct((m, n), out_dtype),
        compiler_params=_params(("parallel", "parallel", "arbitrary")),
        name=name,
    )(*lhs, *[w for w, _, _ in rhs])


def _resnorm_body(x_ref, y_ref, gp_ref, gn_ref, xo_ref, u_ref, *, rows):
    i = pl.program_id(0)
    row = i * rows + lax.broadcasted_iota(jnp.int32, (rows, 1), 0)
    xn = x_ref[...] + _rms(y_ref[...], gp_ref[...])
    xn = jnp.where(row >= T0 - N_META, xn, 0.0)
    xo_ref[...] = xn
    u_ref[...] = _rms(xn, gn_ref[...]).astype(BF16)


def _resnorm(x, y, g_post, g_next):
    lp, d = x.shape
    rows = ROW_BLOCK
    blk = pl.BlockSpec((rows, d), lambda i: (i, 0))
    vec = pl.BlockSpec((1, d), lambda i: (0, 0))
    return pl.pallas_call(
        functools.partial(_resnorm_body, rows=rows),
        grid=(lp // rows,),
        in_specs=[blk, blk, vec, vec],
        out_specs=[blk, blk],
        out_shape=[jax.ShapeDtypeStruct((lp, d), F32), jax.ShapeDtypeStruct((lp, d), BF16)],
        compiler_params=_params(("parallel",)),
        name="resnorm",
    )(x, y, g_post, g_next)


def _resnorm_final_body(x_ref, y_ref, gp_ref, o_ref):
    o_ref[...] = x_ref[...] + _rms(y_ref[...], gp_ref[...])


def _resnorm_final(x, y, g_post):
    lp, d = x.shape
    rows = ROW_BLOCK
    skip = T0 // rows
    blk_in = pl.BlockSpec((rows, d), lambda i: (i + skip, 0))
    return pl.pallas_call(
        _resnorm_final_body,
        grid=((lp - T0) // rows,),
        in_specs=[blk_in, blk_in, pl.BlockSpec((1, d), lambda i: (0, 0))],
        out_specs=pl.BlockSpec((rows, d), lambda i: (i, 0)),
        out_shape=jax.ShapeDtypeStruct((lp - T0, d), F32),
        compiler_params=_params(("parallel",)),
        name="resnorm_final",
    )(x, y, g_post)


def _log_sigmoid(x):
    return jnp.minimum(x, 0.0) - jnp.log1p(jnp.exp(-jnp.abs(x)))


def _mlstm_body(q_ref, k_ref, v_ref, o_ref, gc_ref, gr_ref, cw_ref, bc_ref, br_ref, gn_ref,
                out_ref, xext, c_sc, n_sc, m_sc, *, heads, dk, dv, lc):
    c = pl.program_id(0)
    hk = heads * dk
    meta0 = T0 - N_META

    @pl.when(c == 0)
    def _():
        xext[0:HALO, :] = jnp.zeros((HALO, 2 * hk), F32)
        c_sc[...] = jnp.zeros_like(c_sc)
        n_sc[...] = jnp.zeros_like(n_sc)
        m_sc[...] = jnp.zeros_like(m_sc)

    xext[HALO:HALO + lc, 0:hk] = q_ref[...]
    xext[HALO:HALO + lc, hk:2 * hk] = k_ref[...]
    conv = None
    for j in range(CONV_W):
        term = cw_ref[j:j + 1, :] * xext[pl.ds(HALO - (CONV_W - 1) + j, lc), :]
        conv = term if conv is None else conv + term
    xext[0:HALO, :] = xext[lc:lc + HALO, :]
    qk = conv * jax.nn.sigmoid(conv)

    row = c * lc + lax.broadcasted_iota(jnp.int32, (lc, 1), 0)
    col = c * lc + lax.broadcasted_iota(jnp.int32, (1, lc), 1)
    valid_c = row >= meta0
    valid_r = col >= meta0
    gc = gc_ref[...] + bc_ref[...]
    gr = gr_ref[...] + br_ref[...]
    tt = lax.broadcasted_iota(jnp.int32, (lc, lc), 0)
    ss = lax.broadcasted_iota(jnp.int32, (lc, lc), 1)
    tril = ss <= tt
    triu = tt <= ss
    neg_inf = -jnp.inf

    for h in range(heads):
        li_c = jnp.where(valid_c, gc[:, h:h + 1], neg_inf)
        lf_c = jnp.where(valid_c, _log_sigmoid(gc[:, heads + h:heads + h + 1]), 0.0)
        li_r = jnp.where(valid_r, gr[h:h + 1, :], neg_inf)
        lf_r = jnp.where(valid_r, _log_sigmoid(gr[heads + h:heads + h + 1, :]), 0.0)
        b_c = jnp.sum(jnp.where(tril, lf_r, 0.0), axis=1, keepdims=True)
        b_r = jnp.sum(jnp.where(triu, lf_c, 0.0), axis=0, keepdims=True)
        g = jnp.sum(lf_r, axis=1, keepdims=True)
        m = m_sc[h:h + 1, 0:1]

        d = jnp.where(tril, b_c - b_r + li_r, neg_inf)
        inter = b_c + m
        m_t = jnp.maximum(inter, jnp.max(d, axis=1, keepdims=True))
        w_inter = jnp.exp(inter - m_t)
        p = jnp.exp(d - m_t)

        qh = qk[:, h * dk:(h + 1) * dk] * (dk ** -0.5)
        kh = qk[:, hk + h * dk:hk + (h + 1) * dk]
        qb = qh.astype(BF16)
        s = lax.dot_general(qb, kh.astype(BF16), (((1,), (1,)), ((), ())),
                            preferred_element_type=F32) * p
        vh = v_ref[:, h * dv:(h + 1) * dv]
        ch = c_sc[h]
        nh = n_sc[h:h + 1, :]
        num = (w_inter * jnp.dot(qb, ch.astype(BF16), preferred_element_type=F32)
               + jnp.dot(s.astype(BF16), vh, preferred_element_type=F32))
        den = (w_inter * jnp.sum(qh * nh, axis=1, keepdims=True)
               + jnp.sum(s, axis=1, keepdims=True))
        hh = num / jnp.maximum(jnp.abs(den), jnp.exp(-m_t))

        a_c = g - b_c + li_c
        m_new = jnp.maximum(g + m, jnp.max(a_c, axis=0, keepdims=True))
        decay = jnp.exp(g + m - m_new)
        wk = kh * jnp.exp(a_c - m_new)
        c_sc[h] = decay * ch + lax.dot_general(wk.astype(BF16), vh, (((0,), (0,)), ((), ())),
                                               preferred_element_type=F32)
        n_sc[h:h + 1, :] = decay * nh + jnp.sum(wk, axis=0, keepdims=True)
        m_sc[h:h + 1, :] = jnp.broadcast_to(m_new, (1, m_sc.shape[1]))

        hn = _rms(hh, gn_ref[:, h * dv:(h + 1) * dv])
        og = jax.nn.sigmoid(o_ref[:, h * dv:(h + 1) * dv])
        out_ref[:, h * dv:(h + 1) * dv] = (og * hn).astype(BF16)


def _mlstm(a_qk, v, a_o, b_misc, gates_t, conv_w, bias_c, bias_r, g_mnorm, *, heads, dk, dv, gate_blk):
    lp = v.shape[0]
    lc = MLSTM_CHUNK
    hk = heads * dk
    mw = heads * dv
    assert 2 * heads <= 8
    return pl.pallas_call(
        functools.partial(_mlstm_body, heads=heads, dk=dk, dv=dv, lc=lc),
        grid=(lp // lc,),
        in_specs=[
            pl.BlockSpec((lc, hk), lambda c: (c, 0)),
            pl.BlockSpec((lc, hk), lambda c: (c, 1)),
            pl.BlockSpec((lc, mw), lambda c: (c, 0)),
            pl.BlockSpec((lc, mw), lambda c: (c, 0)),
            pl.BlockSpec((lc, GATE_W), lambda c: (c, gate_blk)),
            pl.BlockSpec((2 * heads, lc), lambda c: (0, c)),
            pl.BlockSpec((CONV_W, 2 * hk), lambda c: (0, 0)),
            pl.BlockSpec((1, GATE_W), lambda c: (0, 0)),
            pl.BlockSpec((2 * heads, 1), lambda c: (0, 0)),
            pl.BlockSpec((1, mw), lambda c: (0, 0)),
        ],
        out_specs=pl.BlockSpec((lc, mw), lambda c: (c, 0)),
        out_shape=jax.ShapeDtypeStruct((lp, mw), BF16),
        scratch_shapes=[
            pltpu.VMEM((lc + HALO, 2 * hk), F32),
            pltpu.VMEM((heads, dk, dv), F32),
            pltpu.VMEM((8, dk), F32),
            pltpu.VMEM((8, 128), F32),
        ],
        compiler_params=_params(("arbitrary",)),
        name="mlstm",
    )(a_qk, a_qk, v, a_o, b_misc, gates_t, conv_w, bias_c, bias_r, g_mnorm)


def _rope128(t, tab):
    pr = t * tab
    rr = pr + pltpu.roll(pr, ROPE, axis=1)
    lane = lax.broadcasted_iota(jnp.int32, pr.shape, 1)
    return jnp.where(lane < ROPE, rr, 0.0)


def _qproj_body(c_ref, g_ref, w_ref, tab_ref, o_ref, cn, *, hpb, scale, bm, sub):
    @pl.when(pl.program_id(1) == 0)
    def _():
        def norm(rows):
            cn[rows, :] = _rms(c_ref[rows, :], g_ref[...]).astype(BF16)
        _row_loop(bm, sub, norm)

    def project(rows):
        acc = jnp.dot(cn[rows, :], w_ref[...], preferred_element_type=F32)
        tab = tab_ref[rows, :]
        for hh in range(hpb):
            base = hh * QK_W
            o_ref[rows, base:base + NOPE] = (acc[:, base:base + NOPE] * scale).astype(BF16)
            rr = _rope128(acc[:, base + NOPE:base + QK_W], tab)
            o_ref[rows, base + NOPE:base + QK_W] = (rr * scale).astype(BF16)
    _row_loop(bm, sub, project)


def _kvproj_body(c_ref, g_ref, w_ref, kr_ref, tab_ref, k_ref, vt_ref, *, hpb, av):
    cn = _rms(c_ref[...], g_ref[...]).astype(BF16)
    acc = jnp.dot(cn, w_ref[...], preferred_element_type=F32)
    kr = _rope128(kr_ref[...], tab_ref[...]).astype(BF16)
    for hh in range(hpb):
        src = hh * (NOPE + av)
        k_ref[:, hh * QK_W:hh * QK_W + NOPE] = acc[:, src:src + NOPE].astype(BF16)
        k_ref[:, hh * QK_W + NOPE:(hh + 1) * QK_W] = kr
        vt_ref[hh, 0] = acc[:, src + NOPE:src + NOPE + av].T.astype(BF16)


def _qproj(b_misc, g_cq, w_uq_r, tab, *, bm, hpb, scale):
    lp = b_misc.shape[0]
    cw = w_uq_r.shape[0]
    n = w_uq_r.shape[1]
    bn = hpb * QK_W
    return pl.pallas_call(
        functools.partial(_qproj_body, hpb=hpb, scale=scale, bm=bm, sub=_div_tile(bm, MM_SUB_ROWS, 16)),
        grid=(lp // bm, n // bn),
        in_specs=[
            pl.BlockSpec((bm, cw), lambda i, j: (i, 0)),
            pl.BlockSpec((1, cw), lambda i, j: (0, 0)),
            pl.BlockSpec((cw, bn), lambda i, j: (0, j)),
            pl.BlockSpec((bm, ROPE_W), lambda i, j: (i, 0)),
        ],
        out_specs=pl.BlockSpec((bm, bn), lambda i, j: (i, j)),
        out_shape=jax.ShapeDtypeStruct((lp, n), BF16),
        scratch_shapes=[pltpu.VMEM((bm, cw), BF16)],
        compiler_params=_params(("parallel", "arbitrary")),
        name="mla_qproj",
    )(b_misc, g_cq, w_uq_r, tab)


def _kvproj(b_misc, g_ckv, w_ukv, tab, *, hpb, av, ckv_blk, kr_blk):
    lp = b_misc.shape[0]
    cw = w_ukv.shape[0]
    n = w_ukv.shape[1]
    heads = n // (NOPE + av)
    bn = hpb * (NOPE + av)
    bm = ATT_CHUNK
    return pl.pallas_call(
        functools.partial(_kvproj_body, hpb=hpb, av=av),
        grid=(lp // bm, n // bn),
        in_specs=[
            pl.BlockSpec((bm, cw), lambda i, j: (i, ckv_blk)),
            pl.BlockSpec((1, cw), lambda i, j: (0, 0)),
            pl.BlockSpec((cw, bn), lambda i, j: (0, j)),
            pl.BlockSpec((bm, ROPE_W), lambda i, j: (i, kr_blk)),
            pl.BlockSpec((bm, ROPE_W), lambda i, j: (i, 0)),
        ],
        out_specs=[
            pl.BlockSpec((bm, hpb * QK_W), lambda i, j: (i, j)),
            pl.BlockSpec((hpb, 1, av, bm), lambda i, j: (j, i, 0, 0)),
        ],
        out_shape=[jax.ShapeDtypeStruct((lp, heads * QK_W), BF16),
                   jax.ShapeDtypeStruct((heads, lp // bm, av, bm), BF16)],
        compiler_params=_params(("parallel", "parallel")),
        name="mla_kvproj",
    )(b_misc, g_ckv, w_ukv, b_misc, tab)


def _attn_body(q_ref, k_ref, vt_ref, o_ref, m_sc, l_sc, acc_sc, s_even, s_odd):
    i = pl.program_id(1)
    n_slab = ATT_TILE // ATT_SLAB
    meta0 = T0 - N_META
    all_slabs = tuple(range(n_slab))
    late_slabs = tuple(s for s in all_slabs if (s + 1) * ATT_SLAB > ATT_CHUNK)

    m_sc[...] = jnp.full_like(m_sc, NEG_SCORE)
    l_sc[...] = jnp.zeros_like(l_sc)
    acc_sc[...] = jnp.zeros_like(acc_sc)

    def scores(j, s_buf):
        start = j * ATT_CHUNK
        if not isinstance(j, int):
            start = pl.multiple_of(start, ATT_CHUNK)
        k = k_ref[pl.ds(start, ATT_CHUNK), :]
        for s in all_slabs:
            q = q_ref[s * ATT_SLAB:(s + 1) * ATT_SLAB, :]
            s_buf[s] = lax.dot_general(k, q, (((1,), (1,)), ((), ())), preferred_element_type=F32)

    def absorb(j, s_buf, slabs, masked):
        vt = vt_ref[0, j]
        if masked:
            kpos = j * ATT_CHUNK + lax.broadcasted_iota(jnp.int32, (ATT_CHUNK, ATT_SLAB), 0)
            qlane = lax.broadcasted_iota(jnp.int32, (ATT_CHUNK, ATT_SLAB), 1)
        for s in slabs:
            def staged():
                st = s_buf[s]
                if masked:
                    qpos = i * ATT_TILE + s * ATT_SLAB + qlane
                    keep = jnp.logical_and(kpos <= qpos, kpos >= meta0)
                    st = jnp.where(keep, st, NEG_SCORE)
                return st
            m_prev = m_sc[s]
            m_new = jnp.maximum(m_prev, jnp.max(staged(), axis=0, keepdims=True))
            alpha = jnp.exp2(m_prev - m_new)
            p = jnp.exp2(staged() - m_new)
            l_sc[s] = alpha * l_sc[s] + jnp.sum(p, axis=0, keepdims=True)
            acc_sc[s] = alpha * acc_sc[s] + jnp.dot(vt, p.astype(BF16), preferred_element_type=F32)
            m_sc[s] = m_new

    def pair(p, mask_even, mask_odd, slabs_odd, more):
        scores(2 * p + 1, s_odd)
        absorb(2 * p, s_even, all_slabs, mask_even)
        if more:
            scores(2 * p + 2, s_even)
        absorb(2 * p + 1, s_odd, slabs_odd, mask_odd)

    scores(0, s_even)

    @pl.when(i > 0)
    def _():
        pair(0, True, False, all_slabs, True)

        def mid(p, carry):
            pair(p, False, False, all_slabs, True)
            return carry
        lax.fori_loop(1, i, mid, 0)

    pair(i, True, True, late_slabs, False)

    for s in all_slabs:
        out = (acc_sc[s] / l_sc[s]).T
        o_ref[s * ATT_SLAB:(s + 1) * ATT_SLAB, :] = out.astype(o_ref.dtype)


def _attention(q, k, vt, *, heads, av):
    lp = q.shape[0]
    n_slab = ATT_TILE // ATT_SLAB
    assert lp % ATT_TILE == 0 and ATT_TILE % ATT_SLAB == 0 and ATT_TILE == 2 * ATT_CHUNK
    return pl.pallas_call(
        _attn_body,
        grid=(heads, lp // ATT_TILE),
        in_specs=[
            pl.BlockSpec((ATT_TILE, QK_W), lambda h, i: (i, h)),
            pl.BlockSpec((lp, QK_W), lambda h, i: (0, h)),
            pl.BlockSpec((1, lp // ATT_CHUNK, av, ATT_CHUNK), lambda h, i: (h, 0, 0, 0)),
        ],
        out_specs=pl.BlockSpec((ATT_TILE, av), lambda h, i: (i, h)),
        out_shape=jax.ShapeDtypeStruct((lp, heads * av), BF16),
        scratch_shapes=[
            pltpu.VMEM((n_slab, 1, ATT_SLAB), F32),
            pltpu.VMEM((n_slab, 1, ATT_SLAB), F32),
            pltpu.VMEM((n_slab, av, ATT_SLAB), F32),
            pltpu.VMEM((n_slab, ATT_CHUNK, ATT_SLAB), F32),
            pltpu.VMEM((n_slab, ATT_CHUNK, ATT_SLAB), F32),
        ],
        compiler_params=_params(("parallel", "parallel")),
        name="mla_attention",
    )(q, k, vt)


def _swap_half(w):
    half = w.shape[-1] // 2
    return jnp.concatenate([-w[..., half:], w[..., :half]], axis=-1)


def _prep_layer_weights(w_in, w_uq, w_ukv, w_out, w_gu, w_down, *, mw, hk, ffp):
    d = w_in.shape[0]
    o_gate = 2 * hk + 2 * mw
    o_cq = o_gate + 2 * M_HEADS
    o_ckv = o_cq + Q_LORA
    o_kr = o_ckv + KV_LORA
    wkr = w_in[:, o_kr:o_kr + ROPE]
    gate_pad = jnp.zeros((d, GATE_W - 2 * M_HEADS), w_in.dtype)
    w_b = jnp.concatenate([w_in[:, o_cq:o_kr], w_in[:, o_gate:o_cq], gate_pad, wkr, _swap_half(wkr)],
                          axis=1).astype(BF16)

    uq = w_uq.reshape(Q_LORA, A_HEADS, NOPE + ROPE)
    uq_r = jnp.concatenate([uq[..., :NOPE], uq[..., NOPE:], _swap_half(uq[..., NOPE:])], axis=-1)
    uq_r = uq_r.reshape(Q_LORA, A_HEADS * QK_W).astype(BF16)

    ff = w_down.shape[0]
    w_gate = jnp.pad(w_gu[:, :ff], ((0, 0), (0, ffp - ff))).astype(BF16)
    w_up = jnp.pad(w_gu[:, ff:], ((0, 0), (0, ffp - ff))).astype(BF16)
    w_down_r = jnp.pad(w_down, ((0, ffp - ff), (0, 0))).astype(BF16)
    return dict(w_in=w_in.astype(BF16), w_b=w_b, uq=uq_r, ukv=w_ukv.astype(BF16),
                w_out=w_out.astype(BF16), w_gate=w_gate, w_up=w_up, w_down=w_down_r)


def _rope_table(lp):
    meta0 = T0 - N_META
    pos = jnp.maximum(jnp.arange(lp, dtype=jnp.int32) - meta0, 0).astype(F32)
    inv_freq = ROPE_THETA ** (-jnp.arange(ROPE // 2, dtype=F32) / (ROPE // 2))
    ang = pos[:, None] * inv_freq[None, :]
    cos, sin = jnp.cos(ang), jnp.sin(ang)
    return jnp.concatenate([cos, cos, sin, sin], axis=-1)


def kernel(x, meta, g_mix_pre, w_in, conv_w, b_gates, g_mnorm, g_cq, w_uq, g_ckv, w_ukv, w_out,
           g_mix_post, g_ffn_pre, w_gu, w_down, g_ffn_post):
    batch, seq, d = x.shape
    assert batch == 1 and seq % T0 == 0
    depth = w_in.shape[0]
    lp = T0 + seq
    mw = d // 2
    dv = mw // M_HEADS
    dk = dv // 2
    hk = M_HEADS * dk
    aw = d - mw
    av = aw // A_HEADS
    ff = w_down.shape[1]
    assert Q_LORA % KV_LORA == 0 and KV_LORA % ROPE_W == 0 and ROPE_W == GATE_W and 2 * hk == mw

    bf = 512
    ffp = -(-ff // 1024) * 1024
    bm = _div_tile(lp, 1664, 128)
    bn_in = _div_tile(mw, 512, 128)
    scale = (NOPE + ROPE) ** -0.5 * LOG2_E

    ckv_blk = Q_LORA // KV_LORA
    gate_blk = (Q_LORA + KV_LORA) // GATE_W
    kr_blk = (Q_LORA + KV_LORA + GATE_W) // ROPE_W
    nb = Q_LORA + KV_LORA + GATE_W + ROPE_W

    tab = _rope_table(lp)
    h, u = _prep(x[0], meta, g_mix_pre[0][None])

    out = None
    for l in range(depth):
        w = _prep_layer_weights(w_in[l], w_uq[l], w_ukv[l], w_out[l], w_gu[l], w_down[l],
                                mw=mw, hk=hk, ffp=ffp)
        a_qk = _matmul([u], [(w["w_in"], 0, 0)], mw, F32, bm=bm, bn=bn_in, name="inproj_qk")
        v_m = _matmul([u], [(w["w_in"], 0, mw // bn_in)], mw, BF16, bm=bm, bn=bn_in, name="inproj_v")
        a_o = _matmul([u], [(w["w_in"], 0, 2 * mw // bn_in)], mw, F32, bm=bm, bn=bn_in, name="inproj_o")
        b_misc = _matmul([u], [(w["w_b"], 0, 0)], nb, F32, bm=bm, bn=_div_tile(nb, 768, 128),
                         name="inproj_misc")

        gates_t = b_misc[:, Q_LORA + KV_LORA:Q_LORA + KV_LORA + 2 * M_HEADS].T
        bias_c = jnp.pad(b_gates[l], (0, GATE_W - 2 * M_HEADS))[None, :]
        bias_r = b_gates[l][:, None]
        h_m = _mlstm(a_qk, v_m, a_o, b_misc, gates_t, conv_w[l], bias_c, bias_r, g_mnorm[l][None],
                     heads=M_HEADS, dk=dk, dv=dv, gate_blk=gate_blk)

        q_a = _qproj(b_misc, g_cq[l][None], w["uq"], tab, bm=_div_tile(lp, 832, 64), hpb=4, scale=scale)
        k_a, vt_a = _kvproj(b_misc, g_ckv[l][None], w["ukv"], tab, hpb=4, av=av,
                            ckv_blk=ckv_blk, kr_blk=kr_blk)
        h_a = _attention(q_a, k_a, vt_a, heads=A_HEADS, av=av)

        mix = _matmul([h_m, h_a], [(w["w_out"], 0, 0), (w["w_out"], 1, 0)], d, F32, bm=bm,
                      bn=_div_tile(d, 512, 128), name="outproj")
        h, u = _resnorm(h, mix, g_mix_post[l][None], g_ffn_pre[l][None])

        act = _matmul([u], [(w["w_gate"], 0, 0), (w["w_up"], 0, 0)], ffp, BF16,
                      bm=_div_tile(lp, 832, 64), bn=bf, swiglu=True, name="ffn_up")
        y = _matmul([act], [(w["w_down"], 0, 0)], d, F32, bm=bm, bn=_div_tile(d, 2048, 128), bk=1024,
                    name="ffn_down")
        if l + 1 < depth:
            h, u = _resnorm(h, y, g_ffn_post[l][None], g_mix_pre[l + 1][None])
        else:
            out = _resnorm_final(h, y, g_ffn_post[l][None])
    return out[None]
```

```python
import functools

import jax
import jax.numpy as jnp
from jax import lax
from jax.experimental import pallas as pl
from jax.experimental.pallas import tpu as pltpu

N_META = 16
M_HEADS = 4
CONV_W = 4
A_HEADS = 16
NOPE = 128
ROPE = 64
Q_LORA = 1536
KV_LORA = 512
ROPE_THETA = 10000.0
NORM_EPS = 1e-6
NEG_SCORE = -1e30
LOG2_E = 1.4426950408889634

T0 = 512
MLSTM_CHUNK = 256
ROW_BLOCK = 256
HALO = 8
QK_W = NOPE + 2 * ROPE
GATE_W = 128
ROPE_W = 2 * ROPE

VMEM_LIMIT_BYTES = 56 * 1024 * 1024
MM_SUB_ROWS = 512
CAST_ROWS = 128
ONES_ROWS = 16
ATT_SLAB = 512
ATT_CHUNK = 768
ATT_TILE = 2 * ATT_CHUNK

F32 = jnp.float32
BF16 = jnp.bfloat16


def _div_tile(n, target, mult):
    best = None
    t = mult
    while t <= min(n, target):
        if n % t == 0:
            best = t
        t += mult
    if best is None:
        raise ValueError(f"no tile for {n} (target {target}, multiple of {mult})")
    return best


def _params(sem, flags=None):
    return pltpu.CompilerParams(dimension_semantics=sem, vmem_limit_bytes=VMEM_LIMIT_BYTES, flags=flags)


def _rms(t, g):
    return t * lax.rsqrt(jnp.mean(t * t, axis=-1, keepdims=True) + NORM_EPS) * g


def _row_loop(nrows, sub, fn):
    def step(r, carry):
        fn(pl.ds(pl.multiple_of(r * sub, sub), sub))
        return carry
    lax.fori_loop(0, nrows // sub, step, 0)


def _prep_body(x_ref, meta_ref, g_ref, h_ref, u_ref):
    i = pl.program_id(0)

    @pl.when(i == 0)
    def _():
        h_ref[...] = jnp.zeros_like(h_ref)
        h_ref[T0 - N_META:T0, :] = meta_ref[...]

    @pl.when(i > 0)
    def _():
        h_ref[...] = x_ref[...]

    u_ref[...] = _rms(h_ref[...], g_ref[...]).astype(BF16)


def _prep(x2d, meta, g):
    seq, d = x2d.shape
    lp = T0 + seq
    return pl.pallas_call(
        _prep_body,
        grid=(lp // T0,),
        in_specs=[
            pl.BlockSpec((T0, d), lambda i: (jnp.maximum(i - 1, 0), 0)),
            pl.BlockSpec((N_META, d), lambda i: (0, 0)),
            pl.BlockSpec((1, d), lambda i: (0, 0)),
        ],
        out_specs=[
            pl.BlockSpec((T0, d), lambda i: (i, 0)),
            pl.BlockSpec((T0, d), lambda i: (i, 0)),
        ],
        out_shape=[jax.ShapeDtypeStruct((lp, d), F32), jax.ShapeDtypeStruct((lp, d), BF16)],
        compiler_params=_params(("parallel",)),
        name="prep_norm",
    )(x2d, meta, g)


def _mm_body(*refs, n_lhs, nk, swiglu, bm, sub):
    lhs = refs[0:n_lhs]
    rhs = refs[n_lhs:-1]
    o_ref = refs[-1]

    def product(rows, b):
        acc = None
        for a, w in zip(lhs, b):
            d = jnp.dot(a[rows, :], w[...], preferred_element_type=F32)
            acc = d if acc is None else acc + d
        return acc

    def finish(rows):
        if swiglu:
            gate = product(rows, rhs[0:1])
            acc = gate * jax.nn.sigmoid(gate) * product(rows, rhs[1:2])
        else:
            acc = product(rows, rhs)
        o_ref[rows, :] = acc.astype(o_ref.dtype)

    if nk == 1:
        _row_loop(bm, sub, finish)
        return

    k = pl.program_id(2)

    def first(rows):
        o_ref[rows, :] = product(rows, rhs)

    def accumulate(rows):
        o_ref[rows, :] += product(rows, rhs)

    @pl.when(k == 0)
    def _():
        _row_loop(bm, sub, first)

    @pl.when(k > 0)
    def _():
        _row_loop(bm, sub, accumulate)


def _matmul(lhs, rhs, n, out_dtype, *, bm, bn, bk=None, swiglu=False, name):
    m = lhs[0].shape[0]
    if bk is None:
        nk = 1
    else:
        assert len(lhs) == 1 and not swiglu and out_dtype == F32
        nk = lhs[0].shape[1] // bk
    in_specs = []
    for a in lhs:
        kk = a.shape[1] if bk is None else bk
        in_specs.append(pl.BlockSpec((bm, kk), lambda i, j, k: (i, k)))
    for p, (w, rb, cb) in enumerate(rhs):
        a = lhs[0] if swiglu else lhs[p]
        kk = a.shape[1] if bk is None else bk
        in_specs.append(pl.BlockSpec((kk, bn), lambda i, j, k, rb=rb, cb=cb: (k + rb, j + cb)))
    return pl.pallas_call(
        functools.partial(_mm_body, n_lhs=len(lhs), nk=nk, swiglu=swiglu, bm=bm,
                          sub=_div_tile(bm, MM_SUB_ROWS, 16)),
        grid=(m // bm, n // bn, nk),
        in_specs=in_specs,
        out_specs=pl.BlockSpec((bm, bn), lambda i, j, k: (i, j)),
        out_shape=jax.ShapeDtypeStruct((m, n), out_dtype),
        compiler_params=_params(("parallel", "parallel", "arbitrary")),
        name=name,
    )(*lhs, *[w for w, _, _ in rhs])


def _resnorm_body(x_ref, y_ref, gp_ref, gn_ref, xo_ref, u_ref, *, rows):
    i = pl.program_id(0)
    row = i * rows + lax.broadcasted_iota(jnp.int32, (rows, 1), 0)
    xn = x_ref[...] + _rms(y_ref[...], gp_ref[...])
    xn = jnp.where(row >= T0 - N_META, xn, 0.0)
    xo_ref[...] = xn
    u_ref[...] = _rms(xn, gn_ref[...]).astype(BF16)


def _resnorm(x, y, g_post, g_next):
    lp, d = x.shape
    rows = ROW_BLOCK
    blk = pl.BlockSpec((rows, d), lambda i: (i, 0))
    vec = pl.BlockSpec((1, d), lambda i: (0, 0))
    return pl.pallas_call(
        functools.partial(_resnorm_body, rows=rows),
        grid=(lp // rows,),
        in_specs=[blk, blk, vec, vec],
        out_specs=[blk, blk],
        out_shape=[jax.ShapeDtypeStruct((lp, d), F32), jax.ShapeDtypeStruct((lp, d), BF16)],
        compiler_params=_params(("parallel",)),
        name="resnorm",
    )(x, y, g_post, g_next)


def _resnorm_final_body(x_ref, y_ref, gp_ref, o_ref):
    o_ref[...] = x_ref[...] + _rms(y_ref[...], gp_ref[...])


def _resnorm_final(x, y, g_post):
    lp, d = x.shape
    rows = ROW_BLOCK
    skip = T0 // rows
    blk_in = pl.BlockSpec((rows, d), lambda i: (i + skip, 0))
    return pl.pallas_call(
        _resnorm_final_body,
        grid=((lp - T0) // rows,),
        in_specs=[blk_in, blk_in, pl.BlockSpec((1, d), lambda i: (0, 0))],
        out_specs=pl.BlockSpec((rows, d), lambda i: (i, 0)),
        out_shape=jax.ShapeDtypeStruct((lp - T0, d), F32),
        compiler_params=_params(("parallel",)),
        name="resnorm_final",
    )(x, y, g_post)


def _log_sigmoid(x):
    return jnp.minimum(x, 0.0) - jnp.log1p(jnp.exp(-jnp.abs(x)))


def _mlstm_body(q_ref, k_ref, v_ref, o_ref, gc_ref, gr_ref, cw_ref, bc_ref, br_ref, gn_ref,
                out_ref, xext, c_sc, n_sc, m_sc, *, heads, dk, dv, lc):
    c = pl.program_id(0)
    hk = heads * dk
    meta0 = T0 - N_META

    @pl.when(c == 0)
    def _():
        xext[0:HALO, :] = jnp.zeros((HALO, 2 * hk), F32)
        c_sc[...] = jnp.zeros_like(c_sc)
        n_sc[...] = jnp.zeros_like(n_sc)
        m_sc[...] = jnp.zeros_like(m_sc)

    xext[HALO:HALO + lc, 0:hk] = q_ref[...]
    xext[HALO:HALO + lc, hk:2 * hk] = k_ref[...]
    conv = None
    for j in range(CONV_W):
        term = cw_ref[j:j + 1, :] * xext[pl.ds(HALO - (CONV_W - 1) + j, lc), :]
        conv = term if conv is None else conv + term
    xext[0:HALO, :] = xext[lc:lc + HALO, :]
    qk = conv * jax.nn.sigmoid(conv)

    row = c * lc + lax.broadcasted_iota(jnp.int32, (lc, 1), 0)
    col = c * lc + lax.broadcasted_iota(jnp.int32, (1, lc), 1)
    valid_c = row >= meta0
    valid_r = col >= meta0
    gc = gc_ref[...] + bc_ref[...]
    gr = gr_ref[...] + br_ref[...]
    tt = lax.broadcasted_iota(jnp.int32, (lc, lc), 0)
    ss = lax.broadcasted_iota(jnp.int32, (lc, lc), 1)
    tril = ss <= tt
    triu = tt <= ss
    neg_inf = -jnp.inf

    for h in range(heads):
        li_c = jnp.where(valid_c, gc[:, h:h + 1], neg_inf)
        lf_c = jnp.where(valid_c, _log_sigmoid(gc[:, heads + h:heads + h + 1]), 0.0)
        li_r = jnp.where(valid_r, gr[h:h + 1, :], neg_inf)
        lf_r = jnp.where(valid_r, _log_sigmoid(gr[heads + h:heads + h + 1, :]), 0.0)
        b_c = jnp.sum(jnp.where(tril, lf_r, 0.0), axis=1, keepdims=True)
        b_r = jnp.sum(jnp.where(triu, lf_c, 0.0), axis=0, keepdims=True)
        g = jnp.sum(lf_r, axis=1, keepdims=True)
        m = m_sc[h:h + 1, 0:1]

        d = jnp.where(tril, b_c - b_r + li_r, neg_inf)
        inter = b_c + m
        m_t = jnp.maximum(inter, jnp.max(d, axis=1, keepdims=True))
        w_inter = jnp.exp(inter - m_t)
        p = jnp.exp(d - m_t)

        qh = qk[:, h * dk:(h + 1) * dk] * (dk ** -0.5)
        kh = qk[:, hk + h * dk:hk + (h + 1) * dk]
        qb = qh.astype(BF16)
        s = lax.dot_general(qb, kh.astype(BF16), (((1,), (1,)), ((), ())),
                            preferred_element_type=F32) * p
        vh = v_ref[:, h * dv:(h + 1) * dv]
        ch = c_sc[h]
        nh = n_sc[h:h + 1, :]
        num = (w_inter * jnp.dot(qb, ch.astype(BF16), preferred_element_type=F32)
               + jnp.dot(s.astype(BF16), vh, preferred_element_type=F32))
        den = (w_inter * jnp.sum(qh * nh, axis=1, keepdims=True)
               + jnp.sum(s, axis=1, keepdims=True))
        hh = num / jnp.maximum(jnp.abs(den), jnp.exp(-m_t))

        a_c = g - b_c + li_c
        m_new = jnp.maximum(g + m, jnp.max(a_c, axis=0, keepdims=True))
        decay = jnp.exp(g + m - m_new)
        wk = kh * jnp.exp(a_c - m_new)
        c_sc[h] = decay * ch + lax.dot_general(wk.astype(BF16), vh, (((0,), (0,)), ((), ())),
                                               preferred_element_type=F32)
        n_sc[h:h + 1, :] = decay * nh + jnp.sum(wk, axis=0, keepdims=True)
        m_sc[h:h + 1, :] = jnp.broadcast_to(m_new, (1, m_sc.shape[1]))

        hn = _rms(hh, gn_ref[:, h * dv:(h + 1) * dv])
        og = jax.nn.sigmoid(o_ref[:, h * dv:(h + 1) * dv])
        out_ref[:, h * dv:(h + 1) * dv] = (og * hn).astype(BF16)


def _mlstm(a_qk, v, a_o, b_misc, gates_t, conv_w, bias_c, bias_r, g_mnorm, *, heads, dk, dv, gate_blk):
    lp = v.shape[0]
    lc = MLSTM_CHUNK
    hk = heads * dk
    mw = heads * dv
    assert 2 * heads <= 8
    return pl.pallas_call(
        functools.partial(_mlstm_body, heads=heads, dk=dk, dv=dv, lc=lc),
        grid=(lp // lc,),
        in_specs=[
            pl.BlockSpec((lc, hk), lambda c: (c, 0)),
            pl.BlockSpec((lc, hk), lambda c: (c, 1)),
            pl.BlockSpec((lc, mw), lambda c: (c, 0)),
            pl.BlockSpec((lc, mw), lambda c: (c, 0)),
            pl.BlockSpec((lc, GATE_W), lambda c: (c, gate_blk)),
            pl.BlockSpec((2 * heads, lc), lambda c: (0, c)),
            pl.BlockSpec((CONV_W, 2 * hk), lambda c: (0, 0)),
            pl.BlockSpec((1, GATE_W), lambda c: (0, 0)),
            pl.BlockSpec((2 * heads, 1), lambda c: (0, 0)),
            pl.BlockSpec((1, mw), lambda c: (0, 0)),
        ],
        out_specs=pl.BlockSpec((lc, mw), lambda c: (c, 0)),
        out_shape=jax.ShapeDtypeStruct((lp, mw), BF16),
        scratch_shapes=[
            pltpu.VMEM((lc + HALO, 2 * hk), F32),
            pltpu.VMEM((heads, dk, dv), F32),
            pltpu.VMEM((8, dk), F32),
            pltpu.VMEM((8, 128), F32),
        ],
        compiler_params=_params(("arbitrary",)),
        name="mlstm",
    )(a_qk, a_qk, v, a_o, b_misc, gates_t, conv_w, bias_c, bias_r, g_mnorm)


def _rope128(t, tab):
    pr = t * tab
    rr = pr + pltpu.roll(pr, ROPE, axis=1)
    lane = lax.broadcasted_iota(jnp.int32, pr.shape, 1)
    return jnp.where(lane < ROPE, rr, 0.0)


def _qproj_body(c_ref, g_ref, w_ref, tab_ref, o_ref, cn, *, hpb, scale, bm, sub):
    @pl.when(pl.program_id(1) == 0)
    def _():
        def norm(rows):
            cn[rows, :] = _rms(c_ref[rows, :], g_ref[...]).astype(BF16)
        _row_loop(bm, sub, norm)

    def project(rows):
        acc = jnp.dot(cn[rows, :], w_ref[...], preferred_element_type=F32)
        tab = tab_ref[rows, :]
        for hh in range(hpb):
            base = hh * QK_W
            o_ref[rows, base:base + NOPE] = (acc[:, base:base + NOPE] * scale).astype(BF16)
            rr = _rope128(acc[:, base + NOPE:base + QK_W], tab)
            o_ref[rows, base + NOPE:base + QK_W] = (rr * scale).astype(BF16)
    _row_loop(bm, sub, project)


def _kvproj_body(c_ref, g_ref, w_ref, kr_ref, tab_ref, k_ref, vt_ref, *, hpb, av):
    cn = _rms(c_ref[...], g_ref[...]).astype(BF16)
    acc = jnp.dot(cn, w_ref[...], preferred_element_type=F32)
    kr = _rope128(kr_ref[...], tab_ref[...]).astype(BF16)
    for hh in range(hpb):
        src = hh * (NOPE + av)
        k_ref[:, hh * QK_W:hh * QK_W + NOPE] = acc[:, src:src + NOPE].astype(BF16)
        k_ref[:, hh * QK_W + NOPE:(hh + 1) * QK_W] = kr
        vt_ref[hh, 0, 0:av, :] = acc[:, src + NOPE:src + NOPE + av].T.astype(BF16)
        vt_ref[hh, 0, av:av + ONES_ROWS, :] = jnp.ones((ONES_ROWS, vt_ref.shape[3]), BF16)


def _qproj(b_misc, g_cq, w_uq_r, tab, *, bm, hpb, scale):
    lp = b_misc.shape[0]
    cw = w_uq_r.shape[0]
    n = w_uq_r.shape[1]
    bn = hpb * QK_W
    return pl.pallas_call(
        functools.partial(_qproj_body, hpb=hpb, scale=scale, bm=bm, sub=_div_tile(bm, MM_SUB_ROWS, 16)),
        grid=(lp // bm, n // bn),
        in_specs=[
            pl.BlockSpec((bm, cw), lambda i, j: (i, 0)),
            pl.BlockSpec((1, cw), lambda i, j: (0, 0)),
            pl.BlockSpec((cw, bn), lambda i, j: (0, j)),
            pl.BlockSpec((bm, ROPE_W), lambda i, j: (i, 0)),
        ],
        out_specs=pl.BlockSpec((bm, bn), lambda i, j: (i, j)),
        out_shape=jax.ShapeDtypeStruct((lp, n), BF16),
        scratch_shapes=[pltpu.VMEM((bm, cw), BF16)],
        compiler_params=_params(("parallel", "arbitrary")),
        name="mla_qproj",
    )(b_misc, g_cq, w_uq_r, tab)


def _kvproj(b_misc, g_ckv, w_ukv, tab, *, hpb, av, ckv_blk, kr_blk):
    lp = b_misc.shape[0]
    cw = w_ukv.shape[0]
    n = w_ukv.shape[1]
    heads = n // (NOPE + av)
    bn = hpb * (NOPE + av)
    bm = ATT_CHUNK
    return pl.pallas_call(
        functools.partial(_kvproj_body, hpb=hpb, av=av),
        grid=(lp // bm, n // bn),
        in_specs=[
            pl.BlockSpec((bm, cw), lambda i, j: (i, ckv_blk)),
            pl.BlockSpec((1, cw), lambda i, j: (0, 0)),
            pl.BlockSpec((cw, bn), lambda i, j: (0, j)),
            pl.BlockSpec((bm, ROPE_W), lambda i, j: (i, kr_blk)),
            pl.BlockSpec((bm, ROPE_W), lambda i, j: (i, 0)),
        ],
        out_specs=[
            pl.BlockSpec((bm, hpb * QK_W), lambda i, j: (i, j)),
            pl.BlockSpec((hpb, 1, av + ONES_ROWS, bm), lambda i, j: (j, i, 0, 0)),
        ],
        out_shape=[jax.ShapeDtypeStruct((lp, heads * QK_W), BF16),
                   jax.ShapeDtypeStruct((heads, lp // bm, av + ONES_ROWS, bm), BF16)],
        compiler_params=_params(("parallel", "parallel")),
        name="mla_kvproj",
    )(b_misc, g_ckv, w_ukv, b_misc, tab)


def _attn_body(q_ref, k_ref, vt_ref, o_ref, m_sc, acc_sc, s_even, s_odd):
    i = pl.program_id(1)
    n_slab = ATT_TILE // ATT_SLAB
    meta0 = T0 - N_META
    all_slabs = tuple(range(n_slab))
    late_slabs = tuple(s for s in all_slabs if (s + 1) * ATT_SLAB > ATT_CHUNK)

    av = o_ref.shape[1]
    m_sc[...] = jnp.full_like(m_sc, NEG_SCORE)
    acc_sc[...] = jnp.zeros_like(acc_sc)

    def scores(j, s_buf):
        start = j * ATT_CHUNK
        if not isinstance(j, int):
            start = pl.multiple_of(start, ATT_CHUNK)
        k = k_ref[pl.ds(start, ATT_CHUNK), :]
        for s in all_slabs:
            q = q_ref[s * ATT_SLAB:(s + 1) * ATT_SLAB, :]
            s_buf[s] = lax.dot_general(k, q, (((1,), (1,)), ((), ())), preferred_element_type=F32)

    def absorb(j, s_buf, slabs, masked):
        vt = vt_ref[0, j]
        if masked:
            kpos = j * ATT_CHUNK + lax.broadcasted_iota(jnp.int32, (ATT_CHUNK, ATT_SLAB), 0)
            qlane = lax.broadcasted_iota(jnp.int32, (ATT_CHUNK, ATT_SLAB), 1)
        for s in slabs:
            def staged():
                st = s_buf[s]
                if masked:
                    qpos = i * ATT_TILE + s * ATT_SLAB + qlane
                    keep = jnp.logical_and(kpos <= qpos, kpos >= meta0)
                    st = jnp.where(keep, st, NEG_SCORE)
                return st
            m_prev = m_sc[s]
            m_new = jnp.maximum(m_prev, jnp.max(staged(), axis=0, keepdims=True))
            alpha = jnp.exp2(m_prev - m_new)
            p = jnp.exp2(staged() - m_new)
            acc_sc[s] = alpha * acc_sc[s] + jnp.dot(vt, p.astype(BF16), preferred_element_type=F32)
            m_sc[s] = m_new

    def pair(p, mask_even, mask_odd, slabs_odd, more):
        scores(2 * p + 1, s_odd)
        absorb(2 * p, s_even, all_slabs, mask_even)
        if more:
            scores(2 * p + 2, s_even)
        absorb(2 * p + 1, s_odd, slabs_odd, mask_odd)

    scores(0, s_even)

    @pl.when(i > 0)
    def _():
        pair(0, True, False, all_slabs, True)

        def mid(p, carry):
            pair(p, False, False, all_slabs, True)
            return carry
        lax.fori_loop(1, i, mid, 0)

    pair(i, True, True, late_slabs, False)

    for s in all_slabs:
        out = (acc_sc[s, 0:av, :] / acc_sc[s, av:av + 1, :]).T
        o_ref[s * ATT_SLAB:(s + 1) * ATT_SLAB, :] = out.astype(o_ref.dtype)


def _attention(q, k, vt, *, heads, av):
    lp = q.shape[0]
    n_slab = ATT_TILE // ATT_SLAB
    assert lp % ATT_TILE == 0 and ATT_TILE % ATT_SLAB == 0 and ATT_TILE == 2 * ATT_CHUNK
    return pl.pallas_call(
        _attn_body,
        grid=(heads, lp // ATT_TILE),
        in_specs=[
            pl.BlockSpec((ATT_TILE, QK_W), lambda h, i: (i, h)),
            pl.BlockSpec((lp, QK_W), lambda h, i: (0, h)),
            pl.BlockSpec((1, lp // ATT_CHUNK, av + ONES_ROWS, ATT_CHUNK), lambda h, i: (h, 0, 0, 0)),
        ],
        out_specs=pl.BlockSpec((ATT_TILE, av), lambda h, i: (i, h)),
        out_shape=jax.ShapeDtypeStruct((lp, heads * av), BF16),
        scratch_shapes=[
            pltpu.VMEM((n_slab, 1, ATT_SLAB), F32),
            pltpu.VMEM((n_slab, av + ONES_ROWS, ATT_SLAB), F32),
            pltpu.VMEM((n_slab, ATT_CHUNK, ATT_SLAB), F32),
            pltpu.VMEM((n_slab, ATT_CHUNK, ATT_SLAB), F32),
        ],
        compiler_params=_params(("parallel", "parallel")),
        name="mla_attention",
    )(q, k, vt)


def _cast_rows_body(x_ref, o_ref, *, n_src):
    @pl.when(pl.program_id(0) < n_src)
    def _():
        o_ref[...] = x_ref[...].astype(BF16)

    @pl.when(pl.program_id(0) >= n_src)
    def _():
        o_ref[...] = jnp.zeros_like(o_ref)


def _cast_rows(w, rows_out):
    rows, cols = w.shape
    rb = _div_tile(rows, CAST_ROWS, 8)
    assert rows_out % rb == 0
    n_src = rows // rb
    return pl.pallas_call(
        functools.partial(_cast_rows_body, n_src=n_src),
        grid=(rows_out // rb,),
        in_specs=[pl.BlockSpec((rb, cols), lambda i: (jnp.minimum(i, n_src - 1), 0))],
        out_specs=pl.BlockSpec((rb, cols), lambda i: (i, 0)),
        out_shape=jax.ShapeDtypeStruct((rows_out, cols), BF16),
        compiler_params=_params(("parallel",)),
        name="cast_bf16",
    )(w)


def _cast_gate_up_body(x_ref, g_ref, u_ref, *, ff):
    pad = g_ref.shape[1] - ff
    g_ref[:, :ff] = x_ref[:, :ff].astype(BF16)
    u_ref[:, :ff] = x_ref[:, ff:].astype(BF16)
    if pad:
        g_ref[:, ff:] = jnp.zeros((g_ref.shape[0], pad), BF16)
        u_ref[:, ff:] = jnp.zeros((u_ref.shape[0], pad), BF16)


def _cast_gate_up(w_gu, ffp):
    d, ff2 = w_gu.shape
    ff = ff2 // 2
    rb = _div_tile(d, CAST_ROWS // 2, 8)
    out = jax.ShapeDtypeStruct((d, ffp), BF16)
    return pl.pallas_call(
        functools.partial(_cast_gate_up_body, ff=ff),
        grid=(d // rb,),
        in_specs=[pl.BlockSpec((rb, ff2), lambda i: (i, 0))],
        out_specs=[pl.BlockSpec((rb, ffp), lambda i: (i, 0))] * 2,
        out_shape=[out, out],
        compiler_params=_params(("parallel",)),
        name="cast_gate_up",
    )(w_gu)


def _swap_half(w):
    half = w.shape[-1] // 2
    return jnp.concatenate([-w[..., half:], w[..., :half]], axis=-1)


def _prep_layer_weights(w_in, w_uq, w_ukv, w_out, w_gu, w_down, *, mw, hk, ffp):
    d = w_in.shape[0]
    o_gate = 2 * hk + 2 * mw
    o_cq = o_gate + 2 * M_HEADS
    o_ckv = o_cq + Q_LORA
    o_kr = o_ckv + KV_LORA
    wkr = w_in[:, o_kr:o_kr + ROPE]
    gate_pad = jnp.zeros((d, GATE_W - 2 * M_HEADS), w_in.dtype)
    w_b = jnp.concatenate([w_in[:, o_cq:o_kr], w_in[:, o_gate:o_cq], gate_pad, wkr, _swap_half(wkr)],
                          axis=1).astype(BF16)

    uq = w_uq.reshape(Q_LORA, A_HEADS, NOPE + ROPE)
    uq_r = jnp.concatenate([uq[..., :NOPE], uq[..., NOPE:], _swap_half(uq[..., NOPE:])], axis=-1)
    uq_r = uq_r.reshape(Q_LORA, A_HEADS * QK_W).astype(BF16)

    w_gate, w_up = _cast_gate_up(w_gu, ffp)
    return dict(w_in=_cast_rows(w_in, w_in.shape[0]), w_b=w_b, uq=uq_r, ukv=_cast_rows(w_ukv, w_ukv.shape[0]),
                w_out=_cast_rows(w_out, w_out.shape[0]), w_gate=w_gate, w_up=w_up,
                w_down=_cast_rows(w_down, ffp))


def _rope_table(lp):
    meta0 = T0 - N_META
    pos = jnp.maximum(jnp.arange(lp, dtype=jnp.int32) - meta0, 0).astype(F32)
    inv_freq = ROPE_THETA ** (-jnp.arange(ROPE // 2, dtype=F32) / (ROPE // 2))
    ang = pos[:, None] * inv_freq[None, :]
    cos, sin = jnp.cos(ang), jnp.sin(ang)
    return jnp.concatenate([cos, cos, sin, sin], axis=-1)


def kernel(x, meta, g_mix_pre, w_in, conv_w, b_gates, g_mnorm, g_cq, w_uq, g_ckv, w_ukv, w_out,
           g_mix_post, g_ffn_pre, w_gu, w_down, g_ffn_post):
    batch, seq, d = x.shape
    assert batch == 1 and seq % T0 == 0
    depth = w_in.shape[0]
    lp = T0 + seq
    mw = d // 2
    dv = mw // M_HEADS
    dk = dv // 2
    hk = M_HEADS * dk
    aw = d - mw
    av = aw // A_HEADS
    ff = w_down.shape[1]
    assert Q_LORA % KV_LORA == 0 and KV_LORA % ROPE_W == 0 and ROPE_W == GATE_W and 2 * hk == mw

    bf = 512
    ffp = -(-ff // 1024) * 1024
    bm = _div_tile(lp, 1664, 128)
    bn_in = _div_tile(mw, 512, 128)
    scale = (NOPE + ROPE) ** -0.5 * LOG2_E

    ckv_blk = Q_LORA // KV_LORA
    gate_blk = (Q_LORA + KV_LORA) // GATE_W
    kr_blk = (Q_LORA + KV_LORA + GATE_W) // ROPE_W
    nb = Q_LORA + KV_LORA + GATE_W + ROPE_W

    tab = _rope_table(lp)
    h, u = _prep(x[0], meta, g_mix_pre[0][None])

    out = None
    for l in range(depth):
        w = _prep_layer_weights(w_in[l], w_uq[l], w_ukv[l], w_out[l], w_gu[l], w_down[l],
                                mw=mw, hk=hk, ffp=ffp)
        a_qk = _matmul([u], [(w["w_in"], 0, 0)], mw, F32, bm=bm, bn=bn_in, name="inproj_qk")
        v_m = _matmul([u], [(w["w_in"], 0, mw // bn_in)], mw, BF16, bm=bm, bn=bn_in, name="inproj_v")
        a_o = _matmul([u], [(w["w_in"], 0, 2 * mw // bn_in)], mw, F32, bm=bm, bn=bn_in, name="inproj_o")
        b_misc = _matmul([u], [(w["w_b"], 0, 0)], nb, F32, bm=bm, bn=_div_tile(nb, 768, 128),
                         name="inproj_misc")

        gates_t = b_misc[:, Q_LORA + KV_LORA:Q_LORA + KV_LORA + 2 * M_HEADS].T
        bias_c = jnp.pad(b_gates[l], (0, GATE_W - 2 * M_HEADS))[None, :]
        bias_r = b_gates[l][:, None]
        h_m = _mlstm(a_qk, v_m, a_o, b_misc, gates_t, conv_w[l], bias_c, bias_r, g_mnorm[l][None],
                     heads=M_HEADS, dk=dk, dv=dv, gate_blk=gate_blk)

        q_a = _qproj(b_misc, g_cq[l][None], w["uq"], tab, bm=_div_tile(lp, 832, 64), hpb=4, scale=scale)
        k_a, vt_a = _kvproj(b_misc, g_ckv[l][None], w["ukv"], tab, hpb=4, av=av,
                            ckv_blk=ckv_blk, kr_blk=kr_blk)
        h_a = _attention(q_a, k_a, vt_a, heads=A_HEADS, av=av)

        mix = _matmul([h_m, h_a], [(w["w_out"], 0, 0), (w["w_out"], 1, 0)], d, F32, bm=bm,
                      bn=_div_tile(d, 512, 128), name="outproj")
        h, u = _resnorm(h, mix, g_mix_post[l][None], g_ffn_pre[l][None])

        act = _matmul([u], [(w["w_gate"], 0, 0), (w["w_up"], 0, 0)], ffp, BF16,
                      bm=bm, bn=bf, swiglu=True, name="ffn_up")
        y = _matmul([act], [(w["w_down"], 0, 0)], d, F32, bm=bm, bn=_div_tile(d, 2048, 128), bk=1024,
                    name="ffn_down")
        if l + 1 < depth:
            h, u = _resnorm(h, y, g_ffn_post[l][None], g_mix_pre[l + 1][None])
        else:
            out = _resnorm_final(h, y, g_ffn_post[l][None])
    return out[None]
```

```python
import functools

import jax
import jax.numpy as jnp
from jax import lax
from jax.experimental import pallas as pl
from jax.experimental.pallas import tpu as pltpu

N_META = 16
M_HEADS = 4
CONV_W = 4
A_HEADS = 16
NOPE = 128
ROPE = 64
Q_LORA = 1536
KV_LORA = 512
ROPE_THETA = 10000.0
NORM_EPS = 1e-6
NEG_SCORE = -1e30
LOG2_E = 1.4426950408889634

T0 = 512
MLSTM_CHUNK = 256
ROW_BLOCK = 256
HALO = 8
QK_W = NOPE + 2 * ROPE
GATE_W = 128
ROPE_W = 2 * ROPE

VMEM_LIMIT_BYTES = 56 * 1024 * 1024
MM_SUB_ROWS = 512
CAST_ROWS = 128
ONES_ROWS = 16
ATT_SLAB = 512
ATT_CHUNK = 768
ATT_TILE = 2 * ATT_CHUNK

F32 = jnp.float32
BF16 = jnp.bfloat16


def _div_tile(n, target, mult):
    best = None
    t = mult
    while t <= min(n, target):
        if n % t == 0:
            best = t
        t += mult
    if best is None:
        raise ValueError(f"no tile for {n} (target {target}, multiple of {mult})")
    return best


def _params(sem, flags=None):
    return pltpu.CompilerParams(dimension_semantics=sem, vmem_limit_bytes=VMEM_LIMIT_BYTES, flags=flags)


def _rms(t, g):
    return t * lax.rsqrt(jnp.mean(t * t, axis=-1, keepdims=True) + NORM_EPS) * g


def _row_loop(nrows, sub, fn):
    def step(r, carry):
        fn(pl.ds(pl.multiple_of(r * sub, sub), sub))
        return carry
    lax.fori_loop(0, nrows // sub, step, 0, unroll=True)


def _prep_body(x_ref, meta_ref, g_ref, h_ref, u_ref):
    i = pl.program_id(0)

    @pl.when(i == 0)
    def _():
        h_ref[...] = jnp.zeros_like(h_ref)
        h_ref[T0 - N_META:T0, :] = meta_ref[...]

    @pl.when(i > 0)
    def _():
        h_ref[...] = x_ref[...]

    u_ref[...] = _rms(h_ref[...], g_ref[...]).astype(BF16)


def _prep(x2d, meta, g):
    seq, d = x2d.shape
    lp = T0 + seq
    return pl.pallas_call(
        _prep_body,
        grid=(lp // T0,),
        in_specs=[
            pl.BlockSpec((T0, d), lambda i: (jnp.maximum(i - 1, 0), 0)),
            pl.BlockSpec((N_META, d), lambda i: (0, 0)),
            pl.BlockSpec((1, d), lambda i: (0, 0)),
        ],
        out_specs=[
            pl.BlockSpec((T0, d), lambda i: (i, 0)),
            pl.BlockSpec((T0, d), lambda i: (i, 0)),
        ],
        out_shape=[jax.ShapeDtypeStruct((lp, d), F32), jax.ShapeDtypeStruct((lp, d), BF16)],
        compiler_params=_params(("parallel",)),
        name="prep_norm",
    )(x2d, meta, g)


def _mm_body(*refs, n_lhs, nk, swiglu, bm, sub):
    lhs = refs[0:n_lhs]
    rhs = refs[n_lhs:-1]
    o_ref = refs[-1]

    def product(rows, b):
        acc = None
        for a, w in zip(lhs, b):
            d = jnp.dot(a[rows, :], w[...], preferred_element_type=F32)
            acc = d if acc is None else acc + d
        return acc

    def finish(rows):
        if swiglu:
            gate = product(rows, rhs[0:1])
            acc = gate * jax.nn.sigmoid(gate) * product(rows, rhs[1:2])
        else:
            acc = product(rows, rhs)
        o_ref[rows, :] = acc.astype(o_ref.dtype)

    if nk == 1:
        _row_loop(bm, sub, finish)
        return

    k = pl.program_id(2)

    def first(rows):
        o_ref[rows, :] = product(rows, rhs)

    def accumulate(rows):
        o_ref[rows, :] += product(rows, rhs)

    @pl.when(k == 0)
    def _():
        _row_loop(bm, sub, first)

    @pl.when(k > 0)
    def _():
        _row_loop(bm, sub, accumulate)


def _matmul(lhs, rhs, n, out_dtype, *, bm, bn, bk=None, swiglu=False, name):
    m = lhs[0].shape[0]
    if bk is None:
        nk = 1
    else:
        assert len(lhs) == 1 and not swiglu and out_dtype == F32
        nk = lhs[0].shape[1] // bk
    in_specs = []
    for a in lhs:
        kk = a.shape[1] if bk is None else bk
        in_specs.append(pl.BlockSpec((bm, kk), lambda i, j, k: (i, k)))
    for p, (w, rb, cb) in enumerate(rhs):
        a = lhs[0] if swiglu else lhs[p]
        kk = a.shape[1] if bk is None else bk
        in_specs.append(pl.BlockSpec((kk, bn), lambda i, j, k, rb=rb, cb=cb: (k + rb, j + cb)))
    return pl.pallas_call(
        functools.partial(_mm_body, n_lhs=len(lhs), nk=nk, swiglu=swiglu, bm=bm,
                          sub=_div_tile(bm, MM_SUB_ROWS, 16)),
        grid=(m // bm, n // bn, nk),
        in_specs=in_specs,
        out_specs=pl.BlockSpec((bm, bn), lambda i, j, k: (i, j)),
        out_shape=jax.ShapeDtypeStruct((m, n), out_dtype),
        compiler_params=_params(("parallel", "parallel", "arbitrary")),
        name=name,
    )(*lhs, *[w for w, _, _ in rhs])


def _resnorm_body(x_ref, y_ref, gp_ref, gn_ref, xo_ref, u_ref, *, rows):
    i = pl.program_id(0)
    row = i * rows + lax.broadcasted_iota(jnp.int32, (rows, 1), 0)
    xn = x_ref[...] + _rms(y_ref[...], gp_ref[...])
    xn = jnp.where(row >= T0 - N_META, xn, 0.0)
    xo_ref[...] = xn
    u_ref[...] = _rms(xn, gn_ref[...]).astype(BF16)


def _resnorm(x, y, g_post, g_next):
    lp, d = x.shape
    rows = ROW_BLOCK
    blk = pl.BlockSpec((rows, d), lambda i: (i, 0))
    vec = pl.BlockSpec((1, d), lambda i: (0, 0))
    return pl.pallas_call(
        functools.partial(_resnorm_body, rows=rows),
        grid=(lp // rows,),
        in_specs=[blk, blk, vec, vec],
        out_specs=[blk, blk],
        out_shape=[jax.ShapeDtypeStruct((lp, d), F32), jax.ShapeDtypeStruct((lp, d), BF16)],
        compiler_params=_params(("parallel",)),
        name="resnorm",
    )(x, y, g_post, g_next)


def _resnorm_final_body(x_ref, y_ref, gp_ref, o_ref):
    o_ref[...] = x_ref[...] + _rms(y_ref[...], gp_ref[...])


def _resnorm_final(x, y, g_post):
    lp, d = x.shape
    rows = ROW_BLOCK
    skip = T0 // rows
    blk_in = pl.BlockSpec((rows, d), lambda i: (i + skip, 0))
    return pl.pallas_call(
        _resnorm_final_body,
        grid=((lp - T0) // rows,),
        in_specs=[blk_in, blk_in, pl.BlockSpec((1, d), lambda i: (0, 0))],
        out_specs=pl.BlockSpec((rows, d), lambda i: (i, 0)),
        out_shape=jax.ShapeDtypeStruct((lp - T0, d), F32),
        compiler_params=_params(("parallel",)),
        name="resnorm_final",
    )(x, y, g_post)


def _log_sigmoid(x):
    return jnp.minimum(x, 0.0) - jnp.log1p(jnp.exp(-jnp.abs(x)))


def _mlstm_body(q_ref, k_ref, v_ref, o_ref, gc_ref, gr_ref, cw_ref, bc_ref, br_ref, gn_ref,
                out_ref, xext, c_sc, n_sc, m_sc, *, heads, dk, dv, lc):
    c = pl.program_id(0)
    hk = heads * dk
    meta0 = T0 - N_META

    @pl.when(c == 0)
    def _():
        xext[0:HALO, :] = jnp.zeros((HALO, 2 * hk), F32)
        c_sc[...] = jnp.zeros_like(c_sc)
        n_sc[...] = jnp.zeros_like(n_sc)
        m_sc[...] = jnp.zeros_like(m_sc)

    xext[HALO:HALO + lc, 0:hk] = q_ref[...]
    xext[HALO:HALO + lc, hk:2 * hk] = k_ref[...]
    conv = None
    for j in range(CONV_W):
        term = cw_ref[j:j + 1, :] * xext[pl.ds(HALO - (CONV_W - 1) + j, lc), :]
        conv = term if conv is None else conv + term
    xext[0:HALO, :] = xext[lc:lc + HALO, :]
    qk = conv * jax.nn.sigmoid(conv)

    row = c * lc + lax.broadcasted_iota(jnp.int32, (lc, 1), 0)
    col = c * lc + lax.broadcasted_iota(jnp.int32, (1, lc), 1)
    valid_c = row >= meta0
    valid_r = col >= meta0
    gc = gc_ref[...] + bc_ref[...]
    gr = gr_ref[...] + br_ref[...]
    tt = lax.broadcasted_iota(jnp.int32, (lc, lc), 0)
    ss = lax.broadcasted_iota(jnp.int32, (lc, lc), 1)
    tril = ss <= tt
    triu = tt <= ss
    neg_inf = -jnp.inf

    for h in range(heads):
        li_c = jnp.where(valid_c, gc[:, h:h + 1], neg_inf)
        lf_c = jnp.where(valid_c, _log_sigmoid(gc[:, heads + h:heads + h + 1]), 0.0)
        li_r = jnp.where(valid_r, gr[h:h + 1, :], neg_inf)
        lf_r = jnp.where(valid_r, _log_sigmoid(gr[heads + h:heads + h + 1, :]), 0.0)
        b_c = jnp.sum(jnp.where(tril, lf_r, 0.0), axis=1, keepdims=True)
        b_r = jnp.sum(jnp.where(triu, lf_c, 0.0), axis=0, keepdims=True)
        g = jnp.sum(lf_r, axis=1, keepdims=True)
        m = m_sc[h:h + 1, 0:1]

        d = jnp.where(tril, b_c - b_r + li_r, neg_inf)
        inter = b_c + m
        m_t = jnp.maximum(inter, jnp.max(d, axis=1, keepdims=True))
        w_inter = jnp.exp(inter - m_t)
        p = jnp.exp(d - m_t)

        qh = qk[:, h * dk:(h + 1) * dk] * (dk ** -0.5)
        kh = qk[:, hk + h * dk:hk + (h + 1) * dk]
        qb = qh.astype(BF16)
        s = lax.dot_general(qb, kh.astype(BF16), (((1,), (1,)), ((), ())),
                            preferred_element_type=F32) * p
        vh = v_ref[:, h * dv:(h + 1) * dv]
        ch = c_sc[h]
        nh = n_sc[h:h + 1, :]
        num = (w_inter * jnp.dot(qb, ch.astype(BF16), preferred_element_type=F32)
               + jnp.dot(s.astype(BF16), vh, preferred_element_type=F32))
        den = (w_inter * jnp.sum(qh * nh, axis=1, keepdims=True)
               + jnp.sum(s, axis=1, keepdims=True))
        hh = num / jnp.maximum(jnp.abs(den), jnp.exp(-m_t))

        a_c = g - b_c + li_c
        m_new = jnp.maximum(g + m, jnp.max(a_c, axis=0, keepdims=True))
        decay = jnp.exp(g + m - m_new)
        wk = kh * jnp.exp(a_c - m_new)
        c_sc[h] = decay * ch + lax.dot_general(wk.astype(BF16), vh, (((0,), (0,)), ((), ())),
                                               preferred_element_type=F32)
        n_sc[h:h + 1, :] = decay * nh + jnp.sum(wk, axis=0, keepdims=True)
        m_sc[h:h + 1, :] = jnp.broadcast_to(m_new, (1, m_sc.shape[1]))

        hn = _rms(hh, gn_ref[:, h * dv:(h + 1) * dv])
        og = jax.nn.sigmoid(o_ref[:, h * dv:(h + 1) * dv])
        out_ref[:, h * dv:(h + 1) * dv] = (og * hn).astype(BF16)


def _mlstm(a_qk, v, a_o, b_misc, gates_t, conv_w, bias_c, bias_r, g_mnorm, *, heads, dk, dv, gate_blk):
    lp = v.shape[0]
    lc = MLSTM_CHUNK
    hk = heads * dk
    mw = heads * dv
    assert 2 * heads <= 8
    return pl.pallas_call(
        functools.partial(_mlstm_body, heads=heads, dk=dk, dv=dv, lc=lc),
        grid=(lp // lc,),
        in_specs=[
            pl.BlockSpec((lc, hk), lambda c: (c, 0)),
            pl.BlockSpec((lc, hk), lambda c: (c, 1)),
            pl.BlockSpec((lc, mw), lambda c: (c, 0)),
            pl.BlockSpec((lc, mw), lambda c: (c, 0)),
            pl.BlockSpec((lc, GATE_W), lambda c: (c, gate_blk)),
            pl.BlockSpec((2 * heads, lc), lambda c: (0, c)),
            pl.BlockSpec((CONV_W, 2 * hk), lambda c: (0, 0)),
            pl.BlockSpec((1, GATE_W), lambda c: (0, 0)),
            pl.BlockSpec((2 * heads, 1), lambda c: (0, 0)),
            pl.BlockSpec((1, mw), lambda c: (0, 0)),
        ],
        out_specs=pl.BlockSpec((lc, mw), lambda c: (c, 0)),
        out_shape=jax.ShapeDtypeStruct((lp, mw), BF16),
        scratch_shapes=[
            pltpu.VMEM((lc + HALO, 2 * hk), F32),
            pltpu.VMEM((heads, dk, dv), F32),
            pltpu.VMEM((8, dk), F32),
            pltpu.VMEM((8, 128), F32),
        ],
        compiler_params=_params(("arbitrary",)),
        name="mlstm",
    )(a_qk, a_qk, v, a_o, b_misc, gates_t, conv_w, bias_c, bias_r, g_mnorm)


def _rope128(t, tab):
    pr = t * tab
    rr = pr + pltpu.roll(pr, ROPE, axis=1)
    lane = lax.broadcasted_iota(jnp.int32, pr.shape, 1)
    return jnp.where(lane < ROPE, rr, 0.0)


def _qproj_body(c_ref, g_ref, w_ref, tab_ref, o_ref, cn, *, hpb, scale, bm, sub):
    @pl.when(pl.program_id(1) == 0)
    def _():
        def norm(rows):
            cn[rows, :] = _rms(c_ref[rows, :], g_ref[...]).astype(BF16)
        _row_loop(bm, sub, norm)

    def project(rows):
        acc = jnp.dot(cn[rows, :], w_ref[...], preferred_element_type=F32)
        tab = tab_ref[rows, :]
        for hh in range(hpb):
            base = hh * QK_W
            o_ref[rows, base:base + NOPE] = (acc[:, base:base + NOPE] * scale).astype(BF16)
            rr = _rope128(acc[:, base + NOPE:base + QK_W], tab)
            o_ref[rows, base + NOPE:base + QK_W] = (rr * scale).astype(BF16)
    _row_loop(bm, sub, project)


def _kvproj_body(c_ref, g_ref, w_ref, kr_ref, tab_ref, k_ref, vt_ref, *, hpb, av):
    cn = _rms(c_ref[...], g_ref[...]).astype(BF16)
    acc = jnp.dot(cn, w_ref[...], preferred_element_type=F32)
    kr = _rope128(kr_ref[...], tab_ref[...]).astype(BF16)
    for hh in range(hpb):
        src = hh * (NOPE + av)
        k_ref[:, hh * QK_W:hh * QK_W + NOPE] = acc[:, src:src + NOPE].astype(BF16)
        k_ref[:, hh * QK_W + NOPE:(hh + 1) * QK_W] = kr
        vt_ref[hh, 0, 0:av, :] = acc[:, src + NOPE:src + NOPE + av].T.astype(BF16)
        vt_ref[hh, 0, av:av + ONES_ROWS, :] = jnp.ones((ONES_ROWS, vt_ref.shape[3]), BF16)


def _qproj(b_misc, g_cq, w_uq_r, tab, *, bm, hpb, scale):
    lp = b_misc.shape[0]
    cw = w_uq_r.shape[0]
    n = w_uq_r.shape[1]
    bn = hpb * QK_W
    return pl.pallas_call(
        functools.partial(_qproj_body, hpb=hpb, scale=scale, bm=bm, sub=_div_tile(bm, MM_SUB_ROWS, 16)),
        grid=(lp // bm, n // bn),
        in_specs=[
            pl.BlockSpec((bm, cw), lambda i, j: (i, 0)),
            pl.BlockSpec((1, cw), lambda i, j: (0, 0)),
            pl.BlockSpec((cw, bn), lambda i, j: (0, j)),
            pl.BlockSpec((bm, ROPE_W), lambda i, j: (i, 0)),
        ],
        out_specs=pl.BlockSpec((bm, bn), lambda i, j: (i, j)),
        out_shape=jax.ShapeDtypeStruct((lp, n), BF16),
        scratch_shapes=[pltpu.VMEM((bm, cw), BF16)],
        compiler_params=_params(("parallel", "arbitrary")),
        name="mla_qproj",
    )(b_misc, g_cq, w_uq_r, tab)


def _kvproj(b_misc, g_ckv, w_ukv, tab, *, hpb, av, ckv_blk, kr_blk):
    lp = b_misc.shape[0]
    cw = w_ukv.shape[0]
    n = w_ukv.shape[1]
    heads = n // (NOPE + av)
    bn = hpb * (NOPE + av)
    bm = ATT_CHUNK
    return pl.pallas_call(
        functools.partial(_kvproj_body, hpb=hpb, av=av),
        grid=(lp // bm, n // bn),
        in_specs=[
            pl.BlockSpec((bm, cw), lambda i, j: (i, ckv_blk)),
            pl.BlockSpec((1, cw), lambda i, j: (0, 0)),
            pl.BlockSpec((cw, bn), lambda i, j: (0, j)),
            pl.BlockSpec((bm, ROPE_W), lambda i, j: (i, kr_blk)),
            pl.BlockSpec((bm, ROPE_W), lambda i, j: (i, 0)),
        ],
        out_specs=[
            pl.BlockSpec((bm, hpb * QK_W), lambda i, j: (i, j)),
            pl.BlockSpec((hpb, 1, av + ONES_ROWS, bm), lambda i, j: (j, i, 0, 0)),
        ],
        out_shape=[jax.ShapeDtypeStruct((lp, heads * QK_W), BF16),
                   jax.ShapeDtypeStruct((heads, lp // bm, av + ONES_ROWS, bm), BF16)],
        compiler_params=_params(("parallel", "parallel")),
        name="mla_kvproj",
    )(b_misc, g_ckv, w_ukv, b_misc, tab)


def _attn_body(q_ref, k_ref, vt_ref, o_ref, m_sc, acc_sc, s_even, s_odd):
    i = pl.program_id(1)
    n_slab = ATT_TILE // ATT_SLAB
    meta0 = T0 - N_META
    all_slabs = tuple(range(n_slab))
    late_slabs = tuple(s for s in all_slabs if (s + 1) * ATT_SLAB > ATT_CHUNK)

    av = o_ref.shape[1]
    m_sc[...] = jnp.full_like(m_sc, NEG_SCORE)
    acc_sc[...] = jnp.zeros_like(acc_sc)

    def scores(j, s_buf):
        start = j * ATT_CHUNK
        if not isinstance(j, int):
            start = pl.multiple_of(start, ATT_CHUNK)
        k = k_ref[pl.ds(start, ATT_CHUNK), :]
        for s in all_slabs:
            q = q_ref[s * ATT_SLAB:(s + 1) * ATT_SLAB, :]
            s_buf[s] = lax.dot_general(k, q, (((1,), (1,)), ((), ())), preferred_element_type=F32)

    def absorb(j, s_buf, slabs, masked):
        vt = vt_ref[0, j]
        if masked:
            kpos = j * ATT_CHUNK + lax.broadcasted_iota(jnp.int32, (ATT_CHUNK, ATT_SLAB), 0)
            qlane = lax.broadcasted_iota(jnp.int32, (ATT_CHUNK, ATT_SLAB), 1)
        for s in slabs:
            def staged():
                st = s_buf[s]
                if masked:
                    qpos = i * ATT_TILE + s * ATT_SLAB + qlane
                    keep = jnp.logical_and(kpos <= qpos, kpos >= meta0)
                    st = jnp.where(keep, st, NEG_SCORE)
                return st
            m_prev = m_sc[s]
            m_new = jnp.maximum(m_prev, jnp.max(staged(), axis=0, keepdims=True))
            alpha = jnp.exp2(m_prev - m_new)
            p = jnp.exp2(staged() - m_new)
            acc_sc[s] = alpha * acc_sc[s] + jnp.dot(vt, p.astype(BF16), preferred_element_type=F32)
            m_sc[s] = m_new

    def pair(p, mask_even, mask_odd, slabs_odd, more):
        scores(2 * p + 1, s_odd)
        absorb(2 * p, s_even, all_slabs, mask_even)
        if more:
            scores(2 * p + 2, s_even)
        absorb(2 * p + 1, s_odd, slabs_odd, mask_odd)

    scores(0, s_even)

    @pl.when(i > 0)
    def _():
        pair(0, True, False, all_slabs, True)

        def mid(p, carry):
            pair(p, False, False, all_slabs, True)
            return carry
        lax.fori_loop(1, i, mid, 0)

    pair(i, True, True, late_slabs, False)

    for s in all_slabs:
        out = (acc_sc[s, 0:av, :] / acc_sc[s, av:av + 1, :]).T
        o_ref[s * ATT_SLAB:(s + 1) * ATT_SLAB, :] = out.astype(o_ref.dtype)


def _attention(q, k, vt, *, heads, av):
    lp = q.shape[0]
    n_slab = ATT_TILE // ATT_SLAB
    assert lp % ATT_TILE == 0 and ATT_TILE % ATT_SLAB == 0 and ATT_TILE == 2 * ATT_CHUNK
    return pl.pallas_call(
        _attn_body,
        grid=(heads, lp // ATT_TILE),
        in_specs=[
            pl.BlockSpec((ATT_TILE, QK_W), lambda h, i: (i, h)),
            pl.BlockSpec((lp, QK_W), lambda h, i: (0, h)),
            pl.BlockSpec((1, lp // ATT_CHUNK, av + ONES_ROWS, ATT_CHUNK), lambda h, i: (h, 0, 0, 0)),
        ],
        out_specs=pl.BlockSpec((ATT_TILE, av), lambda h, i: (i, h)),
        out_shape=jax.ShapeDtypeStruct((lp, heads * av), BF16),
        scratch_shapes=[
            pltpu.VMEM((n_slab, 1, ATT_SLAB), F32),
            pltpu.VMEM((n_slab, av + ONES_ROWS, ATT_SLAB), F32),
            pltpu.VMEM((n_slab, ATT_CHUNK, ATT_SLAB), F32),
            pltpu.VMEM((n_slab, ATT_CHUNK, ATT_SLAB), F32),
        ],
        compiler_params=_params(("parallel", "parallel")),
        name="mla_attention",
    )(q, k, vt)


def _cast_rows_body(x_ref, o_ref, *, n_src):
    @pl.when(pl.program_id(0) < n_src)
    def _():
        o_ref[...] = x_ref[...].astype(BF16)

    @pl.when(pl.program_id(0) >= n_src)
    def _():
        o_ref[...] = jnp.zeros_like(o_ref)


def _cast_rows(w, layer, rows_out=None):
    _, rows, cols = w.shape
    rows_out = rows if rows_out is None else rows_out
    rb = _div_tile(rows, CAST_ROWS, 8)
    assert rows_out % rb == 0
    n_src = rows // rb
    return pl.pallas_call(
        functools.partial(_cast_rows_body, n_src=n_src),
        grid=(rows_out // rb,),
        in_specs=[pl.BlockSpec((None, rb, cols), lambda i: (layer, jnp.minimum(i, n_src - 1), 0))],
        out_specs=pl.BlockSpec((rb, cols), lambda i: (i, 0)),
        out_shape=jax.ShapeDtypeStruct((rows_out, cols), BF16),
        compiler_params=_params(("parallel",)),
        name="cast_bf16",
    )(w)


def _cast_gate_up_body(x_ref, g_ref, u_ref, *, ff):
    pad = g_ref.shape[1] - ff
    g_ref[:, :ff] = x_ref[:, :ff].astype(BF16)
    u_ref[:, :ff] = x_ref[:, ff:].astype(BF16)
    if pad:
        g_ref[:, ff:] = jnp.zeros((g_ref.shape[0], pad), BF16)
        u_ref[:, ff:] = jnp.zeros((u_ref.shape[0], pad), BF16)


def _cast_gate_up(w_gu, layer, ffp):
    _, d, ff2 = w_gu.shape
    ff = ff2 // 2
    rb = _div_tile(d, CAST_ROWS // 2, 8)
    out = jax.ShapeDtypeStruct((d, ffp), BF16)
    return pl.pallas_call(
        functools.partial(_cast_gate_up_body, ff=ff),
        grid=(d // rb,),
        in_specs=[pl.BlockSpec((None, rb, ff2), lambda i: (layer, i, 0))],
        out_specs=[pl.BlockSpec((rb, ffp), lambda i: (i, 0))] * 2,
        out_shape=[out, out],
        compiler_params=_params(("parallel",)),
        name="cast_gate_up",
    )(w_gu)


def _swap_half(w):
    half = w.shape[-1] // 2
    return jnp.concatenate([-w[..., half:], w[..., :half]], axis=-1)


def _prep_layer_weights(layer, w_in_all, w_uq_all, w_ukv_all, w_out_all, w_gu_all, w_down_all, *, mw, hk, ffp):
    w_in = w_in_all[layer]
    w_uq = w_uq_all[layer]
    d = w_in.shape[0]
    o_gate = 2 * hk + 2 * mw
    o_cq = o_gate + 2 * M_HEADS
    o_ckv = o_cq + Q_LORA
    o_kr = o_ckv + KV_LORA
    wkr = w_in[:, o_kr:o_kr + ROPE]
    gate_pad = jnp.zeros((d, GATE_W - 2 * M_HEADS), w_in.dtype)
    w_b = jnp.concatenate([w_in[:, o_cq:o_kr], w_in[:, o_gate:o_cq], gate_pad, wkr, _swap_half(wkr)],
                          axis=1).astype(BF16)

    uq = w_uq.reshape(Q_LORA, A_HEADS, NOPE + ROPE)
    uq_r = jnp.concatenate([uq[..., :NOPE], uq[..., NOPE:], _swap_half(uq[..., NOPE:])], axis=-1)
    uq_r = uq_r.reshape(Q_LORA, A_HEADS * QK_W).astype(BF16)

    w_gate, w_up = _cast_gate_up(w_gu_all, layer, ffp)
    return dict(w_in=_cast_rows(w_in_all, layer), w_b=w_b, uq=uq_r, ukv=_cast_rows(w_ukv_all, layer),
                w_out=_cast_rows(w_out_all, layer), w_gate=w_gate, w_up=w_up,
                w_down=_cast_rows(w_down_all, layer, ffp))


def _rope_table(lp):
    meta0 = T0 - N_META
    pos = jnp.maximum(jnp.arange(lp, dtype=jnp.int32) - meta0, 0).astype(F32)
    inv_freq = ROPE_THETA ** (-jnp.arange(ROPE // 2, dtype=F32) / (ROPE // 2))
    ang = pos[:, None] * inv_freq[None, :]
    cos, sin = jnp.cos(ang), jnp.sin(ang)
    return jnp.concatenate([cos, cos, sin, sin], axis=-1)


def kernel(x, meta, g_mix_pre, w_in, conv_w, b_gates, g_mnorm, g_cq, w_uq, g_ckv, w_ukv, w_out,
           g_mix_post, g_ffn_pre, w_gu, w_down, g_ffn_post):
    batch, seq, d = x.shape
    assert batch == 1 and seq % T0 == 0
    depth = w_in.shape[0]
    lp = T0 + seq
    mw = d // 2
    dv = mw // M_HEADS
    dk = dv // 2
    hk = M_HEADS * dk
    aw = d - mw
    av = aw // A_HEADS
    ff = w_down.shape[1]
    assert Q_LORA % KV_LORA == 0 and KV_LORA % ROPE_W == 0 and ROPE_W == GATE_W and 2 * hk == mw

    bf = 512
    ffp = -(-ff // 1024) * 1024
    bm = _div_tile(lp, 1664, 128)
    bn_in = _div_tile(mw, 512, 128)
    scale = (NOPE + ROPE) ** -0.5 * LOG2_E

    ckv_blk = Q_LORA // KV_LORA
    gate_blk = (Q_LORA + KV_LORA) // GATE_W
    kr_blk = (Q_LORA + KV_LORA + GATE_W) // ROPE_W
    nb = Q_LORA + KV_LORA + GATE_W + ROPE_W

    tab = _rope_table(lp)
    h, u = _prep(x[0], meta, g_mix_pre[0][None])

    out = None
    for l in range(depth):
        w = _prep_layer_weights(l, w_in, w_uq, w_ukv, w_out, w_gu, w_down, mw=mw, hk=hk, ffp=ffp)
        a_qk = _matmul([u], [(w["w_in"], 0, 0)], mw, F32, bm=bm, bn=bn_in, name="inproj_qk")
        v_m = _matmul([u], [(w["w_in"], 0, mw // bn_in)], mw, BF16, bm=bm, bn=bn_in, name="inproj_v")
        a_o = _matmul([u], [(w["w_in"], 0, 2 * mw // bn_in)], mw, F32, bm=bm, bn=bn_in, name="inproj_o")
        b_misc = _matmul([u], [(w["w_b"], 0, 0)], nb, F32, bm=bm, bn=_div_tile(nb, 768, 128),
                         name="inproj_misc")

        gates_t = b_misc[:, Q_LORA + KV_LORA:Q_LORA + KV_LORA + 2 * M_HEADS].T
        bias_c = jnp.pad(b_gates[l], (0, GATE_W - 2 * M_HEADS))[None, :]
        bias_r = b_gates[l][:, None]
        h_m = _mlstm(a_qk, v_m, a_o, b_misc, gates_t, conv_w[l], bias_c, bias_r, g_mnorm[l][None],
                     heads=M_HEADS, dk=dk, dv=dv, gate_blk=gate_blk)

        q_a = _qproj(b_misc, g_cq[l][None], w["uq"], tab, bm=_div_tile(lp, 832, 64), hpb=4, scale=scale)
        k_a, vt_a = _kvproj(b_misc, g_ckv[l][None], w["ukv"], tab, hpb=4, av=av,
                            ckv_blk=ckv_blk, kr_blk=kr_blk)
        h_a = _attention(q_a, k_a, vt_a, heads=A_HEADS, av=av)

        mix = _matmul([h_m, h_a], [(w["w_out"], 0, 0), (w["w_out"], 1, 0)], d, F32, bm=bm,
                      bn=_div_tile(d, 512, 128), name="outproj")
        h, u = _resnorm(h, mix, g_mix_post[l][None], g_ffn_pre[l][None])

        act = _matmul([u], [(w["w_gate"], 0, 0), (w["w_up"], 0, 0)], ffp, BF16,
                      bm=bm, bn=bf, swiglu=True, name="ffn_up")
        y = _matmul([act], [(w["w_down"], 0, 0)], d, F32, bm=bm, bn=_div_tile(d, 2048, 128), bk=1024,
                    name="ffn_down")
        if l + 1 < depth:
            h, u = _resnorm(h, y, g_ffn_post[l][None], g_mix_pre[l + 1][None])
        else:
            out = _resnorm_final(h, y, g_ffn_post[l][None])
    return out[None]
```

```python
import functools

import jax
import jax.numpy as jnp
from jax import lax
from jax.experimental import pallas as pl
from jax.experimental.pallas import tpu as pltpu

N_META = 16
M_HEADS = 4
CONV_W = 4
A_HEADS = 16
NOPE = 128
ROPE = 64
Q_LORA = 1536
KV_LORA = 512
ROPE_THETA = 10000.0
NORM_EPS = 1e-6
NEG_SCORE = -1e30
LOG2_E = 1.4426950408889634

T0 = 512
MLSTM_CHUNK = 256
ROW_BLOCK = 256
HALO = 8
QK_W = NOPE + 2 * ROPE
GATE_W = 128
ROPE_W = 2 * ROPE

VMEM_LIMIT_BYTES = 56 * 1024 * 1024
MM_SUB_ROWS = 512
CAST_ROWS = 128
ONES_ROWS = 16
ATT_SLAB = 512
ATT_CHUNK = 768
ATT_TILE = 2 * ATT_CHUNK

F32 = jnp.float32
BF16 = jnp.bfloat16


def _div_tile(n, target, mult):
    best = None
    t = mult
    while t <= min(n, target):
        if n % t == 0:
            best = t
        t += mult
    if best is None:
        raise ValueError(f"no tile for {n} (target {target}, multiple of {mult})")
    return best


def _params(sem, flags=None):
    return pltpu.CompilerParams(dimension_semantics=sem, vmem_limit_bytes=VMEM_LIMIT_BYTES, flags=flags)


def _rms(t, g):
    return t * lax.rsqrt(jnp.mean(t * t, axis=-1, keepdims=True) + NORM_EPS) * g


def _row_loop(nrows, sub, fn):
    def step(r, carry):
        fn(pl.ds(pl.multiple_of(r * sub, sub), sub))
        return carry
    lax.fori_loop(0, nrows // sub, step, 0, unroll=True)


def _prep_body(x_ref, meta_ref, g_ref, h_ref, u_ref):
    i = pl.program_id(0)

    @pl.when(i == 0)
    def _():
        h_ref[...] = jnp.zeros_like(h_ref)
        h_ref[T0 - N_META:T0, :] = meta_ref[...]

    @pl.when(i > 0)
    def _():
        h_ref[...] = x_ref[...]

    u_ref[...] = _rms(h_ref[...], g_ref[...]).astype(BF16)


def _prep(x2d, meta, g):
    seq, d = x2d.shape
    lp = T0 + seq
    return pl.pallas_call(
        _prep_body,
        grid=(lp // T0,),
        in_specs=[
            pl.BlockSpec((T0, d), lambda i: (jnp.maximum(i - 1, 0), 0)),
            pl.BlockSpec((N_META, d), lambda i: (0, 0)),
            pl.BlockSpec((1, d), lambda i: (0, 0)),
        ],
        out_specs=[
            pl.BlockSpec((T0, d), lambda i: (i, 0)),
            pl.BlockSpec((T0, d), lambda i: (i, 0)),
        ],
        out_shape=[jax.ShapeDtypeStruct((lp, d), F32), jax.ShapeDtypeStruct((lp, d), BF16)],
        compiler_params=_params(("parallel",)),
        name="prep_norm",
    )(x2d, meta, g)


def _mm_body(*refs, n_lhs, nk, swiglu, bm, sub):
    lhs = refs[0:n_lhs]
    rhs = refs[n_lhs:-1]
    o_ref = refs[-1]

    def product(rows, b):
        acc = None
        for a, w in zip(lhs, b):
            d = jnp.dot(a[rows, :], w[...], preferred_element_type=F32)
            acc = d if acc is None else acc + d
        return acc

    def finish(rows):
        if swiglu:
            gate = product(rows, rhs[0:1])
            acc = gate * jax.nn.sigmoid(gate) * product(rows, rhs[1:2])
        else:
            acc = product(rows, rhs)
        o_ref[rows, :] = acc.astype(o_ref.dtype)

    if nk == 1:
        _row_loop(bm, sub, finish)
        return

    k = pl.program_id(2)

    def first(rows):
        o_ref[rows, :] = product(rows, rhs)

    def accumulate(rows):
        o_ref[rows, :] += product(rows, rhs)

    @pl.when(k == 0)
    def _():
        _row_loop(bm, sub, first)

    @pl.when(k > 0)
    def _():
        _row_loop(bm, sub, accumulate)


def _matmul(lhs, rhs, n, out_dtype, *, bm, bn, bk=None, swiglu=False, name):
    m = lhs[0].shape[0]
    if bk is None:
        nk = 1
    else:
        assert len(lhs) == 1 and not swiglu and out_dtype == F32
        nk = lhs[0].shape[1] // bk
    in_specs = []
    for a in lhs:
        kk = a.shape[1] if bk is None else bk
        in_specs.append(pl.BlockSpec((bm, kk), lambda i, j, k: (i, k)))
    for p, (w, rb, cb) in enumerate(rhs):
        a = lhs[0] if swiglu else lhs[p]
        kk = a.shape[1] if bk is None else bk
        in_specs.append(pl.BlockSpec((kk, bn), lambda i, j, k, rb=rb, cb=cb: (k + rb, j + cb)))
    return pl.pallas_call(
        functools.partial(_mm_body, n_lhs=len(lhs), nk=nk, swiglu=swiglu, bm=bm,
                          sub=_div_tile(bm, MM_SUB_ROWS, 16)),
        grid=(m // bm, n // bn, nk),
        in_specs=in_specs,
        out_specs=pl.BlockSpec((bm, bn), lambda i, j, k: (i, j)),
        out_shape=jax.ShapeDtypeStruct((m, n), out_dtype),
        compiler_params=_params(("parallel", "parallel", "arbitrary")),
        name=name,
    )(*lhs, *[w for w, _, _ in rhs])


def _resnorm_body(x_ref, y_ref, gp_ref, gn_ref, xo_ref, u_ref, *, rows):
    i = pl.program_id(0)
    row = i * rows + lax.broadcasted_iota(jnp.int32, (rows, 1), 0)
    xn = x_ref[...] + _rms(y_ref[...], gp_ref[...])
    xn = jnp.where(row >= T0 - N_META, xn, 0.0)
    xo_ref[...] = xn
    u_ref[...] = _rms(xn, gn_ref[...]).astype(BF16)


def _resnorm(x, y, g_post, g_next):
    lp, d = x.shape
    rows = ROW_BLOCK
    blk = pl.BlockSpec((rows, d), lambda i: (i, 0))
    vec = pl.BlockSpec((1, d), lambda i: (0, 0))
    return pl.pallas_call(
        functools.partial(_resnorm_body, rows=rows),
        grid=(lp // rows,),
        in_specs=[blk, blk, vec, vec],
        out_specs=[blk, blk],
        out_shape=[jax.ShapeDtypeStruct((lp, d), F32), jax.ShapeDtypeStruct((lp, d), BF16)],
        compiler_params=_params(("parallel",)),
        name="resnorm",
    )(x, y, g_post, g_next)


def _resnorm_final_body(x_ref, y_ref, gp_ref, o_ref):
    o_ref[...] = x_ref[...] + _rms(y_ref[...], gp_ref[...])


def _resnorm_final(x, y, g_post):
    lp, d = x.shape
    rows = ROW_BLOCK
    skip = T0 // rows
    blk_in = pl.BlockSpec((rows, d), lambda i: (i + skip, 0))
    return pl.pallas_call(
        _resnorm_final_body,
        grid=((lp - T0) // rows,),
        in_specs=[blk_in, blk_in, pl.BlockSpec((1, d), lambda i: (0, 0))],
        out_specs=pl.BlockSpec((rows, d), lambda i: (i, 0)),
        out_shape=jax.ShapeDtypeStruct((lp - T0, d), F32),
        compiler_params=_params(("parallel",)),
        name="resnorm_final",
    )(x, y, g_post)


def _log_sigmoid(x):
    return jnp.minimum(x, 0.0) - jnp.log1p(jnp.exp(-jnp.abs(x)))


def _mlstm_body(q_ref, k_ref, v_ref, o_ref, gc_ref, gr_ref, cw_ref, bc_ref, br_ref, gn_ref,
                out_ref, xext, c_sc, n_sc, m_sc, *, heads, dk, dv, lc):
    c = pl.program_id(0)
    hk = heads * dk
    meta0 = T0 - N_META

    @pl.when(c == 0)
    def _():
        xext[0:HALO, :] = jnp.zeros((HALO, 2 * hk), F32)
        c_sc[...] = jnp.zeros_like(c_sc)
        n_sc[...] = jnp.zeros_like(n_sc)
        m_sc[...] = jnp.zeros_like(m_sc)

    xext[HALO:HALO + lc, 0:hk] = q_ref[...]
    xext[HALO:HALO + lc, hk:2 * hk] = k_ref[...]
    conv = None
    for j in range(CONV_W):
        term = cw_ref[j:j + 1, :] * xext[pl.ds(HALO - (CONV_W - 1) + j, lc), :]
        conv = term if conv is None else conv + term
    xext[0:HALO, :] = xext[lc:lc + HALO, :]
    qk = conv * jax.nn.sigmoid(conv)

    row = c * lc + lax.broadcasted_iota(jnp.int32, (lc, 1), 0)
    col = c * lc + lax.broadcasted_iota(jnp.int32, (1, lc), 1)
    valid_c = row >= meta0
    valid_r = col >= meta0
    gc = gc_ref[...] + bc_ref[...]
    gr = gr_ref[...] + br_ref[...]
    tt = lax.broadcasted_iota(jnp.int32, (lc, lc), 0)
    ss = lax.broadcasted_iota(jnp.int32, (lc, lc), 1)
    tril = ss <= tt
    triu = tt <= ss
    neg_inf = -jnp.inf

    for h in range(heads):
        li_c = jnp.where(valid_c, gc[:, h:h + 1], neg_inf)
        lf_c = jnp.where(valid_c, _log_sigmoid(gc[:, heads + h:heads + h + 1]), 0.0)
        li_r = jnp.where(valid_r, gr[h:h + 1, :], neg_inf)
        lf_r = jnp.where(valid_r, _log_sigmoid(gr[heads + h:heads + h + 1, :]), 0.0)
        b_c = jnp.sum(jnp.where(tril, lf_r, 0.0), axis=1, keepdims=True)
        b_r = jnp.sum(jnp.where(triu, lf_c, 0.0), axis=0, keepdims=True)
        g = jnp.sum(lf_r, axis=1, keepdims=True)
        m = m_sc[h:h + 1, 0:1]

        d = jnp.where(tril, b_c - b_r + li_r, neg_inf)
        inter = b_c + m
        m_t = jnp.maximum(inter, jnp.max(d, axis=1, keepdims=True))
        w_inter = jnp.exp(inter - m_t)
        p = jnp.exp(d - m_t)

        qh = qk[:, h * dk:(h + 1) * dk] * (dk ** -0.5)
        kh = qk[:, hk + h * dk:hk + (h + 1) * dk]
        qb = qh.astype(BF16)
        s = lax.dot_general(qb, kh.astype(BF16), (((1,), (1,)), ((), ())),
                            preferred_element_type=F32) * p
        vh = v_ref[:, h * dv:(h + 1) * dv]
        ch = c_sc[h]
        nh = n_sc[h:h + 1, :]
        num = (w_inter * jnp.dot(qb, ch.astype(BF16), preferred_element_type=F32)
               + jnp.dot(s.astype(BF16), vh, preferred_element_type=F32))
        den = (w_inter * jnp.sum(qh * nh, axis=1, keepdims=True)
               + jnp.sum(s, axis=1, keepdims=True))
        hh = num / jnp.maximum(jnp.abs(den), jnp.exp(-m_t))

        a_c = g - b_c + li_c
        m_new = jnp.maximum(g + m, jnp.max(a_c, axis=0, keepdims=True))
        decay = jnp.exp(g + m - m_new)
        wk = kh * jnp.exp(a_c - m_new)
        c_sc[h] = decay * ch + lax.dot_general(wk.astype(BF16), vh, (((0,), (0,)), ((), ())),
                                               preferred_element_type=F32)
        n_sc[h:h + 1, :] = decay * nh + jnp.sum(wk, axis=0, keepdims=True)
        m_sc[h:h + 1, :] = jnp.broadcast_to(m_new, (1, m_sc.shape[1]))

        hn = _rms(hh, gn_ref[:, h * dv:(h + 1) * dv])
        og = jax.nn.sigmoid(o_ref[:, h * dv:(h + 1) * dv])
        out_ref[:, h * dv:(h + 1) * dv] = (og * hn).astype(BF16)


def _mlstm(a_qk, v, a_o, b_misc, gates_t, conv_w, bias_c, bias_r, g_mnorm, *, heads, dk, dv, gate_blk):
    lp = v.shape[0]
    lc = MLSTM_CHUNK
    hk = heads * dk
    mw = heads * dv
    assert 2 * heads <= 8
    return pl.pallas_call(
        functools.partial(_mlstm_body, heads=heads, dk=dk, dv=dv, lc=lc),
        grid=(lp // lc,),
        in_specs=[
            pl.BlockSpec((lc, hk), lambda c: (c, 0)),
            pl.BlockSpec((lc, hk), lambda c: (c, 1)),
            pl.BlockSpec((lc, mw), lambda c: (c, 0)),
            pl.BlockSpec((lc, mw), lambda c: (c, 0)),
            pl.BlockSpec((lc, GATE_W), lambda c: (c, gate_blk)),
            pl.BlockSpec((2 * heads, lc), lambda c: (0, c)),
            pl.BlockSpec((CONV_W, 2 * hk), lambda c: (0, 0)),
            pl.BlockSpec((1, GATE_W), lambda c: (0, 0)),
            pl.BlockSpec((2 * heads, 1), lambda c: (0, 0)),
            pl.BlockSpec((1, mw), lambda c: (0, 0)),
        ],
        out_specs=pl.BlockSpec((lc, mw), lambda c: (c, 0)),
        out_shape=jax.ShapeDtypeStruct((lp, mw), BF16),
        scratch_shapes=[
            pltpu.VMEM((lc + HALO, 2 * hk), F32),
            pltpu.VMEM((heads, dk, dv), F32),
            pltpu.VMEM((8, dk), F32),
            pltpu.VMEM((8, 128), F32),
        ],
        compiler_params=_params(("arbitrary",)),
        name="mlstm",
    )(a_qk, a_qk, v, a_o, b_misc, gates_t, conv_w, bias_c, bias_r, g_mnorm)


def _rope128(t, tab):
    pr = t * tab
    rr = pr + pltpu.roll(pr, ROPE, axis=1)
    lane = lax.broadcasted_iota(jnp.int32, pr.shape, 1)
    return jnp.where(lane < ROPE, rr, 0.0)


def _qproj_body(c_ref, g_ref, w_ref, tab_ref, o_ref, cn, *, hpb, scale, bm, sub):
    @pl.when(pl.program_id(1) == 0)
    def _():
        def norm(rows):
            cn[rows, :] = _rms(c_ref[rows, :], g_ref[...]).astype(BF16)
        _row_loop(bm, sub, norm)

    def project(rows):
        acc = jnp.dot(cn[rows, :], w_ref[...], preferred_element_type=F32)
        tab = tab_ref[rows, :]
        for hh in range(hpb):
            base = hh * QK_W
            o_ref[rows, base:base + NOPE] = (acc[:, base:base + NOPE] * scale).astype(BF16)
            rr = _rope128(acc[:, base + NOPE:base + QK_W], tab)
            o_ref[rows, base + NOPE:base + QK_W] = (rr * scale).astype(BF16)
    _row_loop(bm, sub, project)


def _kvproj_body(c_ref, g_ref, w_ref, kr_ref, tab_ref, k_ref, vt_ref, *, hpb, av):
    cn = _rms(c_ref[...], g_ref[...]).astype(BF16)
    acc = jnp.dot(cn, w_ref[...], preferred_element_type=F32)
    kr = _rope128(kr_ref[...], tab_ref[...]).astype(BF16)
    for hh in range(hpb):
        src = hh * (NOPE + av)
        k_ref[:, hh * QK_W:hh * QK_W + NOPE] = acc[:, src:src + NOPE].astype(BF16)
        k_ref[:, hh * QK_W + NOPE:(hh + 1) * QK_W] = kr
        vt_ref[hh, 0, 0:av, :] = acc[:, src + NOPE:src + NOPE + av].T.astype(BF16)
        vt_ref[hh, 0, av:av + ONES_ROWS, :] = jnp.ones((ONES_ROWS, vt_ref.shape[3]), BF16)


def _qproj(b_misc, g_cq, w_uq_r, tab, *, bm, hpb, scale):
    lp = b_misc.shape[0]
    cw = w_uq_r.shape[0]
    n = w_uq_r.shape[1]
    bn = hpb * QK_W
    return pl.pallas_call(
        functools.partial(_qproj_body, hpb=hpb, scale=scale, bm=bm, sub=_div_tile(bm, MM_SUB_ROWS, 16)),
        grid=(lp // bm, n // bn),
        in_specs=[
            pl.BlockSpec((bm, cw), lambda i, j: (i, 0)),
            pl.BlockSpec((1, cw), lambda i, j: (0, 0)),
            pl.BlockSpec((cw, bn), lambda i, j: (0, j)),
            pl.BlockSpec((bm, ROPE_W), lambda i, j: (i, 0)),
        ],
        out_specs=pl.BlockSpec((bm, bn), lambda i, j: (i, j)),
        out_shape=jax.ShapeDtypeStruct((lp, n), BF16),
        scratch_shapes=[pltpu.VMEM((bm, cw), BF16)],
        compiler_params=_params(("parallel", "arbitrary")),
        name="mla_qproj",
    )(b_misc, g_cq, w_uq_r, tab)


def _kvproj(b_misc, g_ckv, w_ukv, tab, *, hpb, av, ckv_blk, kr_blk):
    lp = b_misc.shape[0]
    cw = w_ukv.shape[0]
    n = w_ukv.shape[1]
    heads = n // (NOPE + av)
    bn = hpb * (NOPE + av)
    bm = ATT_CHUNK
    return pl.pallas_call(
        functools.partial(_kvproj_body, hpb=hpb, av=av),
        grid=(lp // bm, n // bn),
        in_specs=[
            pl.BlockSpec((bm, cw), lambda i, j: (i, ckv_blk)),
            pl.BlockSpec((1, cw), lambda i, j: (0, 0)),
            pl.BlockSpec((cw, bn), lambda i, j: (0, j)),
            pl.BlockSpec((bm, ROPE_W), lambda i, j: (i, kr_blk)),
            pl.BlockSpec((bm, ROPE_W), lambda i, j: (i, 0)),
        ],
        out_specs=[
            pl.BlockSpec((bm, hpb * QK_W), lambda i, j: (i, j)),
            pl.BlockSpec((hpb, 1, av + ONES_ROWS, bm), lambda i, j: (j, i, 0, 0)),
        ],
        out_shape=[jax.ShapeDtypeStruct((lp, heads * QK_W), BF16),
                   jax.ShapeDtypeStruct((heads, lp // bm, av + ONES_ROWS, bm), BF16)],
        compiler_params=_params(("parallel", "parallel")),
        name="mla_kvproj",
    )(b_misc, g_ckv, w_ukv, b_misc, tab)


def _attn_body(q_ref, k_ref, vt_ref, o_ref, m_sc, acc_sc, s_even, s_odd, c_even, c_odd):
    i = pl.program_id(1)
    n_slab = ATT_TILE // ATT_SLAB
    meta0 = T0 - N_META
    all_slabs = tuple(range(n_slab))
    late_slabs = tuple(s for s in all_slabs if (s + 1) * ATT_SLAB > ATT_CHUNK)

    av = o_ref.shape[1]
    m_sc[...] = jnp.full_like(m_sc, NEG_SCORE)
    acc_sc[...] = jnp.zeros_like(acc_sc)

    def scores(j, buf, masked):
        s_buf, c_buf = buf
        start = j * ATT_CHUNK
        if not isinstance(j, int):
            start = pl.multiple_of(start, ATT_CHUNK)
        k = k_ref[pl.ds(start, ATT_CHUNK), :]
        if masked:
            kpos = j * ATT_CHUNK + lax.broadcasted_iota(jnp.int32, (ATT_CHUNK, ATT_SLAB), 0)
            qlane = lax.broadcasted_iota(jnp.int32, (ATT_CHUNK, ATT_SLAB), 1)
        for s in all_slabs:
            q = q_ref[s * ATT_SLAB:(s + 1) * ATT_SLAB, :]
            st = lax.dot_general(k, q, (((1,), (1,)), ((), ())), preferred_element_type=F32)
            if masked:
                qpos = i * ATT_TILE + s * ATT_SLAB + qlane
                keep = jnp.logical_and(kpos <= qpos, kpos >= meta0)
                st = jnp.where(keep, st, NEG_SCORE)
            s_buf[s] = st
            c_buf[s] = jnp.max(st, axis=0, keepdims=True)

    def absorb(j, buf, slabs):
        s_buf, c_buf = buf
        vt = vt_ref[0, j]
        for s in slabs:
            m_prev = m_sc[s]
            m_new = jnp.maximum(m_prev, c_buf[s])
            alpha = jnp.exp2(m_prev - m_new)
            p = jnp.exp2(s_buf[s] - m_new)
            acc_sc[s] = alpha * acc_sc[s] + jnp.dot(vt, p.astype(BF16), preferred_element_type=F32)
            m_sc[s] = m_new

    even = (s_even, c_even)
    odd = (s_odd, c_odd)

    def pair(p, mask_odd, slabs_odd, mask_next):
        scores(2 * p + 1, odd, mask_odd)
        absorb(2 * p, even, all_slabs)
        if mask_next is not None:
            scores(2 * p + 2, even, mask_next)
        absorb(2 * p + 1, odd, slabs_odd)

    scores(0, even, True)

    @pl.when(i == 1)
    def _():
        pair(0, False, all_slabs, True)

    @pl.when(i > 1)
    def _():
        pair(0, False, all_slabs, False)

        def mid(p, carry):
            pair(p, False, all_slabs, False)
            return carry
        lax.fori_loop(1, i - 1, mid, 0)
        pair(i - 1, False, all_slabs, True)

    pair(i, True, late_slabs, None)

    for s in all_slabs:
        out = (acc_sc[s, 0:av, :] / acc_sc[s, av:av + 1, :]).T
        o_ref[s * ATT_SLAB:(s + 1) * ATT_SLAB, :] = out.astype(o_ref.dtype)


def _attention(q, k, vt, *, heads, av):
    lp = q.shape[0]
    n_slab = ATT_TILE // ATT_SLAB
    assert lp % ATT_TILE == 0 and ATT_TILE % ATT_SLAB == 0 and ATT_TILE == 2 * ATT_CHUNK
    return pl.pallas_call(
        _attn_body,
        grid=(heads, lp // ATT_TILE),
        in_specs=[
            pl.BlockSpec((ATT_TILE, QK_W), lambda h, i: (i, h)),
            pl.BlockSpec((lp, QK_W), lambda h, i: (0, h)),
            pl.BlockSpec((1, lp // ATT_CHUNK, av + ONES_ROWS, ATT_CHUNK), lambda h, i: (h, 0, 0, 0)),
        ],
        out_specs=pl.BlockSpec((ATT_TILE, av), lambda h, i: (i, h)),
        out_shape=jax.ShapeDtypeStruct((lp, heads * av), BF16),
        scratch_shapes=[
            pltpu.VMEM((n_slab, 1, ATT_SLAB), F32),
            pltpu.VMEM((n_slab, av + ONES_ROWS, ATT_SLAB), F32),
            pltpu.VMEM((n_slab, ATT_CHUNK, ATT_SLAB), F32),
            pltpu.VMEM((n_slab, ATT_CHUNK, ATT_SLAB), F32),
            pltpu.VMEM((n_slab, 1, ATT_SLAB), F32),
            pltpu.VMEM((n_slab, 1, ATT_SLAB), F32),
        ],
        compiler_params=_params(("parallel", "parallel")),
        name="mla_attention",
    )(q, k, vt)


def _cast_rows_body(x_ref, o_ref, *, n_src):
    @pl.when(pl.program_id(0) < n_src)
    def _():
        o_ref[...] = x_ref[...].astype(BF16)

    @pl.when(pl.program_id(0) >= n_src)
    def _():
        o_ref[...] = jnp.zeros_like(o_ref)


def _cast_rows(w, layer, rows_out=None):
    _, rows, cols = w.shape
    rows_out = rows if rows_out is None else rows_out
    rb = _div_tile(rows, CAST_ROWS, 8)
    assert rows_out % rb == 0
    n_src = rows // rb
    return pl.pallas_call(
        functools.partial(_cast_rows_body, n_src=n_src),
        grid=(rows_out // rb,),
        in_specs=[pl.BlockSpec((None, rb, cols), lambda i: (layer, jnp.minimum(i, n_src - 1), 0))],
        out_specs=pl.BlockSpec((rb, cols), lambda i: (i, 0)),
        out_shape=jax.ShapeDtypeStruct((rows_out, cols), BF16),
        compiler_params=_params(("parallel",)),
        name="cast_bf16",
    )(w)


def _cast_gate_up_body(x_ref, g_ref, u_ref, *, ff):
    pad = g_ref.shape[1] - ff
    g_ref[:, :ff] = x_ref[:, :ff].astype(BF16)
    u_ref[:, :ff] = x_ref[:, ff:].astype(BF16)
    if pad:
        g_ref[:, ff:] = jnp.zeros((g_ref.shape[0], pad), BF16)
        u_ref[:, ff:] = jnp.zeros((u_ref.shape[0], pad), BF16)


def _cast_gate_up(w_gu, layer, ffp):
    _, d, ff2 = w_gu.shape
    ff = ff2 // 2
    rb = _div_tile(d, CAST_ROWS // 2, 8)
    out = jax.ShapeDtypeStruct((d, ffp), BF16)
    return pl.pallas_call(
        functools.partial(_cast_gate_up_body, ff=ff),
        grid=(d // rb,),
        in_specs=[pl.BlockSpec((None, rb, ff2), lambda i: (layer, i, 0))],
        out_specs=[pl.BlockSpec((rb, ffp), lambda i: (i, 0))] * 2,
        out_shape=[out, out],
        compiler_params=_params(("parallel",)),
        name="cast_gate_up",
    )(w_gu)


def _swap_half(w):
    half = w.shape[-1] // 2
    return jnp.concatenate([-w[..., half:], w[..., :half]], axis=-1)


def _prep_layer_weights(layer, w_in_all, w_uq_all, w_ukv_all, w_out_all, w_gu_all, w_down_all, *, mw, hk, ffp):
    w_uq = w_uq_all[layer]
    d = w_in_all.shape[1]
    o_gate = 2 * hk + 2 * mw
    o_cq = o_gate + 2 * M_HEADS
    o_ckv = o_cq + Q_LORA
    o_kr = o_ckv + KV_LORA
    w_main = w_in_all[layer, :, :o_gate].astype(BF16)
    w_tail = w_in_all[layer, :, o_gate:]
    wkr = w_tail[:, o_kr - o_gate:o_kr - o_gate + ROPE]
    gate_pad = jnp.zeros((d, GATE_W - 2 * M_HEADS), w_tail.dtype)
    w_b = jnp.concatenate([w_tail[:, o_cq - o_gate:o_kr - o_gate], w_tail[:, :o_cq - o_gate], gate_pad, wkr,
                           _swap_half(wkr)], axis=1).astype(BF16)

    uq = w_uq.reshape(Q_LORA, A_HEADS, NOPE + ROPE)
    uq_r = jnp.concatenate([uq[..., :NOPE], uq[..., NOPE:], _swap_half(uq[..., NOPE:])], axis=-1)
    uq_r = uq_r.reshape(Q_LORA, A_HEADS * QK_W).astype(BF16)

    w_gate, w_up = _cast_gate_up(w_gu_all, layer, ffp)
    return dict(w_in=w_main, w_b=w_b, uq=uq_r, ukv=_cast_rows(w_ukv_all, layer),
                w_out=_cast_rows(w_out_all, layer), w_gate=w_gate, w_up=w_up,
                w_down=_cast_rows(w_down_all, layer, ffp))


def _rope_table(lp):
    meta0 = T0 - N_META
    pos = jnp.maximum(jnp.arange(lp, dtype=jnp.int32) - meta0, 0).astype(F32)
    inv_freq = ROPE_THETA ** (-jnp.arange(ROPE // 2, dtype=F32) / (ROPE // 2))
    ang = pos[:, None] * inv_freq[None, :]
    cos, sin = jnp.cos(ang), jnp.sin(ang)
    return jnp.concatenate([cos, cos, sin, sin], axis=-1)


def kernel(x, meta, g_mix_pre, w_in, conv_w, b_gates, g_mnorm, g_cq, w_uq, g_ckv, w_ukv, w_out,
           g_mix_post, g_ffn_pre, w_gu, w_down, g_ffn_post):
    batch, seq, d = x.shape
    assert batch == 1 and seq % T0 == 0
    depth = w_in.shape[0]
    lp = T0 + seq
    mw = d // 2
    dv = mw // M_HEADS
    dk = dv // 2
    hk = M_HEADS * dk
    aw = d - mw
    av = aw // A_HEADS
    ff = w_down.shape[1]
    assert Q_LORA % KV_LORA == 0 and KV_LORA % ROPE_W == 0 and ROPE_W == GATE_W and 2 * hk == mw

    bf = 512
    ffp = -(-ff // 1024) * 1024
    bm = _div_tile(lp, 1664, 128)
    bn_in = _div_tile(mw, 512, 128)
    scale = (NOPE + ROPE) ** -0.5 * LOG2_E

    ckv_blk = Q_LORA // KV_LORA
    gate_blk = (Q_LORA + KV_LORA) // GATE_W
    kr_blk = (Q_LORA + KV_LORA + GATE_W) // ROPE_W
    nb = Q_LORA + KV_LORA + GATE_W + ROPE_W

    tab = _rope_table(lp)
    h, u = _prep(x[0], meta, g_mix_pre[0][None])

    out = None
    for l in range(depth):
        w = _prep_layer_weights(l, w_in, w_uq, w_ukv, w_out, w_gu, w_down, mw=mw, hk=hk, ffp=ffp)
        a_qk = _matmul([u], [(w["w_in"], 0, 0)], mw, F32, bm=bm, bn=bn_in, name="inproj_qk")
        v_m = _matmul([u], [(w["w_in"], 0, mw // bn_in)], mw, BF16, bm=bm, bn=bn_in, name="inproj_v")
        a_o = _matmul([u], [(w["w_in"], 0, 2 * mw // bn_in)], mw, F32, bm=bm, bn=bn_in, name="inproj_o")
        b_misc = _matmul([u], [(w["w_b"], 0, 0)], nb, F32, bm=bm, bn=_div_tile(nb, 768, 128),
                         name="inproj_misc")

        gates_t = b_misc[:, Q_LORA + KV_LORA:Q_LORA + KV_LORA + 2 * M_HEADS].T
        bias_c = jnp.pad(b_gates[l], (0, GATE_W - 2 * M_HEADS))[None, :]
        bias_r = b_gates[l][:, None]
        h_m = _mlstm(a_qk, v_m, a_o, b_misc, gates_t, conv_w[l], bias_c, bias_r, g_mnorm[l][None],
                     heads=M_HEADS, dk=dk, dv=dv, gate_blk=gate_blk)

        q_a = _qproj(b_misc, g_cq[l][None], w["uq"], tab, bm=_div_tile(lp, 832, 64), hpb=4, scale=scale)
        k_a, vt_a = _kvproj(b_misc, g_ckv[l][None], w["ukv"], tab, hpb=4, av=av,
                            ckv_blk=ckv_blk, kr_blk=kr_blk)
        h_a = _attention(q_a, k_a, vt_a, heads=A_HEADS, av=av)

        mix = _matmul([h_m, h_a], [(w["w_out"], 0, 0), (w["w_out"], 1, 0)], d, F32, bm=bm,
                      bn=_div_tile(d, 512, 128), name="outproj")
        h, u = _resnorm(h, mix, g_mix_post[l][None], g_ffn_pre[l][None])

        act = _matmul([u], [(w["w_gate"], 0, 0), (w["w_up"], 0, 0)], ffp, BF16,
                      bm=bm, bn=bf, swiglu=True, name="ffn_up")
        y = _matmul([act], [(w["w_down"], 0, 0)], d, F32, bm=bm, bn=_div_tile(d, 2048, 128), bk=1024,
                    name="ffn_down")
        if l + 1 < depth:
            h, u = _resnorm(h, y, g_ffn_post[l][None], g_mix_pre[l + 1][None])
        else:
            out = _resnorm_final(h, y, g_ffn_post[l][None])
    return out[None]
```

```python
import functools

import jax
import jax.numpy as jnp
from jax import lax
from jax.experimental import pallas as pl
from jax.experimental.pallas import tpu as pltpu

N_META = 16
M_HEADS = 4
CONV_W = 4
A_HEADS = 16
NOPE = 128
ROPE = 64
Q_LORA = 1536
KV_LORA = 512
ROPE_THETA = 10000.0
NORM_EPS = 1e-6
NEG_SCORE = -1e30
LOG2_E = 1.4426950408889634

T0 = 512
MLSTM_CHUNK = 256
ROW_BLOCK = 256
HALO = 8
QK_W = NOPE + 2 * ROPE
GATE_W = 128
ROPE_W = 2 * ROPE

VMEM_LIMIT_BYTES = 56 * 1024 * 1024
MM_SUB_ROWS = 512
CAST_ROWS = 128
MISC_BN = 512
ONES_ROWS = 16
ATT_SLAB = 512
ATT_CHUNK = 768
ATT_TILE = 2 * ATT_CHUNK

F32 = jnp.float32
BF16 = jnp.bfloat16


def _div_tile(n, target, mult):
    best = None
    t = mult
    while t <= min(n, target):
        if n % t == 0:
            best = t
        t += mult
    if best is None:
        raise ValueError(f"no tile for {n} (target {target}, multiple of {mult})")
    return best


def _params(sem, flags=None):
    return pltpu.CompilerParams(dimension_semantics=sem, vmem_limit_bytes=VMEM_LIMIT_BYTES, flags=flags)


def _rms(t, g):
    return t * lax.rsqrt(jnp.mean(t * t, axis=-1, keepdims=True) + NORM_EPS) * g


def _row_loop(nrows, sub, fn):
    def step(r, carry):
        fn(pl.ds(pl.multiple_of(r * sub, sub), sub))
        return carry
    lax.fori_loop(0, nrows // sub, step, 0, unroll=True)


def _prep_body(x_ref, meta_ref, g_ref, h_ref, u_ref):
    i = pl.program_id(0)

    @pl.when(i == 0)
    def _():
        h_ref[...] = jnp.zeros_like(h_ref)
        h_ref[T0 - N_META:T0, :] = meta_ref[...]

    @pl.when(i > 0)
    def _():
        h_ref[...] = x_ref[...]

    u_ref[...] = _rms(h_ref[...], g_ref[...]).astype(BF16)


def _prep(x2d, meta, g):
    seq, d = x2d.shape
    lp = T0 + seq
    return pl.pallas_call(
        _prep_body,
        grid=(lp // T0,),
        in_specs=[
            pl.BlockSpec((T0, d), lambda i: (jnp.maximum(i - 1, 0), 0)),
            pl.BlockSpec((N_META, d), lambda i: (0, 0)),
            pl.BlockSpec((1, d), lambda i: (0, 0)),
        ],
        out_specs=[
            pl.BlockSpec((T0, d), lambda i: (i, 0)),
            pl.BlockSpec((T0, d), lambda i: (i, 0)),
        ],
        out_shape=[jax.ShapeDtypeStruct((lp, d), F32), jax.ShapeDtypeStruct((lp, d), BF16)],
        compiler_params=_params(("parallel",)),
        name="prep_norm",
    )(x2d, meta, g)


def _mm_body(*refs, n_lhs, nk, swiglu, bm, sub):
    lhs = refs[0:n_lhs]
    rhs = refs[n_lhs:-1]
    o_ref = refs[-1]

    def product(rows, b):
        acc = None
        for a, w in zip(lhs, b):
            d = jnp.dot(a[rows, :], w[...], preferred_element_type=F32)
            acc = d if acc is None else acc + d
        return acc

    def finish(rows):
        if swiglu:
            gate = product(rows, rhs[0:1])
            acc = gate * jax.nn.sigmoid(gate) * product(rows, rhs[1:2])
        else:
            acc = product(rows, rhs)
        o_ref[rows, :] = acc.astype(o_ref.dtype)

    if nk == 1:
        _row_loop(bm, sub, finish)
        return

    k = pl.program_id(2)

    def first(rows):
        o_ref[rows, :] = product(rows, rhs)

    def accumulate(rows):
        o_ref[rows, :] += product(rows, rhs)

    @pl.when(k == 0)
    def _():
        _row_loop(bm, sub, first)

    @pl.when(k > 0)
    def _():
        _row_loop(bm, sub, accumulate)


def _matmul(lhs, rhs, n, out_dtype, *, bm, bn, bk=None, swiglu=False, name):
    m = lhs[0].shape[0]
    if bk is None:
        nk = 1
    else:
        assert len(lhs) == 1 and not swiglu and out_dtype == F32
        nk = lhs[0].shape[1] // bk
    in_specs = []
    for a in lhs:
        kk = a.shape[1] if bk is None else bk
        in_specs.append(pl.BlockSpec((bm, kk), lambda i, j, k: (i, k)))
    for p, (w, rb, cb) in enumerate(rhs):
        a = lhs[0] if swiglu else lhs[p]
        kk = a.shape[1] if bk is None else bk
        in_specs.append(pl.BlockSpec((kk, bn), lambda i, j, k, rb=rb, cb=cb: (k + rb, j + cb)))
    return pl.pallas_call(
        functools.partial(_mm_body, n_lhs=len(lhs), nk=nk, swiglu=swiglu, bm=bm,
                          sub=_div_tile(bm, MM_SUB_ROWS, 16)),
        grid=(m // bm, n // bn, nk),
        in_specs=in_specs,
        out_specs=pl.BlockSpec((bm, bn), lambda i, j, k: (i, j)),
        out_shape=jax.ShapeDtypeStruct((m, n), out_dtype),
        compiler_params=_params(("parallel", "parallel", "arbitrary")),
        name=name,
    )(*lhs, *[w for w, _, _ in rhs])


def _resnorm_body(x_ref, y_ref, gp_ref, gn_ref, xo_ref, u_ref, *, rows):
    i = pl.program_id(0)
    row = i * rows + lax.broadcasted_iota(jnp.int32, (rows, 1), 0)
    xn = x_ref[...] + _rms(y_ref[...], gp_ref[...])
    xn = jnp.where(row >= T0 - N_META, xn, 0.0)
    xo_ref[...] = xn
    u_ref[...] = _rms(xn, gn_ref[...]).astype(BF16)


def _resnorm(x, y, g_post, g_next):
    lp, d = x.shape
    rows = ROW_BLOCK
    blk = pl.BlockSpec((rows, d), lambda i: (i, 0))
    vec = pl.BlockSpec((1, d), lambda i: (0, 0))
    return pl.pallas_call(
        functools.partial(_resnorm_body, rows=rows),
        grid=(lp // rows,),
        in_specs=[blk, blk, vec, vec],
        out_specs=[blk, blk],
        out_shape=[jax.ShapeDtypeStruct((lp, d), F32), jax.ShapeDtypeStruct((lp, d), BF16)],
        compiler_params=_params(("parallel",)),
        name="resnorm",
    )(x, y, g_post, g_next)


def _resnorm_final_body(x_ref, y_ref, gp_ref, o_ref):
    o_ref[...] = x_ref[...] + _rms(y_ref[...], gp_ref[...])


def _resnorm_final(x, y, g_post):
    lp, d = x.shape
    rows = ROW_BLOCK
    skip = T0 // rows
    blk_in = pl.BlockSpec((rows, d), lambda i: (i + skip, 0))
    return pl.pallas_call(
        _resnorm_final_body,
        grid=((lp - T0) // rows,),
        in_specs=[blk_in, blk_in, pl.BlockSpec((1, d), lambda i: (0, 0))],
        out_specs=pl.BlockSpec((rows, d), lambda i: (i, 0)),
        out_shape=jax.ShapeDtypeStruct((lp - T0, d), F32),
        compiler_params=_params(("parallel",)),
        name="resnorm_final",
    )(x, y, g_post)


def _log_sigmoid(x):
    return jnp.minimum(x, 0.0) - jnp.log1p(jnp.exp(-jnp.abs(x)))


def _mlstm_body(q_ref, k_ref, v_ref, o_ref, gc_ref, gr_ref, cw_ref, bc_ref, br_ref, gn_ref,
                out_ref, xext, c_sc, n_sc, m_sc, *, heads, dk, dv, lc):
    c = pl.program_id(0)
    hk = heads * dk
    meta0 = T0 - N_META

    @pl.when(c == 0)
    def _():
        xext[0:HALO, :] = jnp.zeros((HALO, 2 * hk), F32)
        c_sc[...] = jnp.zeros_like(c_sc)
        n_sc[...] = jnp.zeros_like(n_sc)
        m_sc[...] = jnp.zeros_like(m_sc)

    xext[HALO:HALO + lc, 0:hk] = q_ref[...]
    xext[HALO:HALO + lc, hk:2 * hk] = k_ref[...]
    conv = None
    for j in range(CONV_W):
        term = cw_ref[j:j + 1, :] * xext[pl.ds(HALO - (CONV_W - 1) + j, lc), :]
        conv = term if conv is None else conv + term
    xext[0:HALO, :] = xext[lc:lc + HALO, :]
    qk = conv * jax.nn.sigmoid(conv)

    row = c * lc + lax.broadcasted_iota(jnp.int32, (lc, 1), 0)
    col = c * lc + lax.broadcasted_iota(jnp.int32, (1, lc), 1)
    valid_c = row >= meta0
    valid_r = col >= meta0
    gc = gc_ref[...] + bc_ref[...]
    gr = gr_ref[...] + br_ref[...]
    tt = lax.broadcasted_iota(jnp.int32, (lc, lc), 0)
    ss = lax.broadcasted_iota(jnp.int32, (lc, lc), 1)
    tril = ss <= tt
    triu = tt <= ss
    neg_inf = -jnp.inf

    for h in range(heads):
        li_c = jnp.where(valid_c, gc[:, h:h + 1], neg_inf)
        lf_c = jnp.where(valid_c, _log_sigmoid(gc[:, heads + h:heads + h + 1]), 0.0)
        li_r = jnp.where(valid_r, gr[h:h + 1, :], neg_inf)
        lf_r = jnp.where(valid_r, _log_sigmoid(gr[heads + h:heads + h + 1, :]), 0.0)
        b_c = jnp.sum(jnp.where(tril, lf_r, 0.0), axis=1, keepdims=True)
        b_r = jnp.sum(jnp.where(triu, lf_c, 0.0), axis=0, keepdims=True)
        g = jnp.sum(lf_r, axis=1, keepdims=True)
        m = m_sc[h:h + 1, 0:1]

        d = jnp.where(tril, b_c - b_r + li_r, neg_inf)
        inter = b_c + m
        m_t = jnp.maximum(inter, jnp.max(d, axis=1, keepdims=True))
        w_inter = jnp.exp(inter - m_t)
        p = jnp.exp(d - m_t)

        qh = qk[:, h * dk:(h + 1) * dk] * (dk ** -0.5)
        kh = qk[:, hk + h * dk:hk + (h + 1) * dk]
        qb = qh.astype(BF16)
        s = lax.dot_general(qb, kh.astype(BF16), (((1,), (1,)), ((), ())),
                            preferred_element_type=F32) * p
        vh = v_ref[:, h * dv:(h + 1) * dv]
        ch = c_sc[h]
        nh = n_sc[h:h + 1, :]
        num = (w_inter * jnp.dot(qb, ch.astype(BF16), preferred_element_type=F32)
               + jnp.dot(s.astype(BF16), vh, preferred_element_type=F32))
        den = (w_inter * jnp.sum(qh * nh, axis=1, keepdims=True)
               + jnp.sum(s, axis=1, keepdims=True))
        hh = num / jnp.maximum(jnp.abs(den), jnp.exp(-m_t))

        a_c = g - b_c + li_c
        m_new = jnp.maximum(g + m, jnp.max(a_c, axis=0, keepdims=True))
        decay = jnp.exp(g + m - m_new)
        wk = kh * jnp.exp(a_c - m_new)
        c_sc[h] = decay * ch + lax.dot_general(wk.astype(BF16), vh, (((0,), (0,)), ((), ())),
                                               preferred_element_type=F32)
        n_sc[h:h + 1, :] = decay * nh + jnp.sum(wk, axis=0, keepdims=True)
        m_sc[h:h + 1, :] = jnp.broadcast_to(m_new, (1, m_sc.shape[1]))

        hn = _rms(hh, gn_ref[:, h * dv:(h + 1) * dv])
        og = jax.nn.sigmoid(o_ref[:, h * dv:(h + 1) * dv])
        out_ref[:, h * dv:(h + 1) * dv] = (og * hn).astype(BF16)


def _mlstm(a_qk, v, a_o, b_misc, gates_t, conv_w, bias_c, bias_r, g_mnorm, *, heads, dk, dv, gate_blk):
    lp = v.shape[0]
    lc = MLSTM_CHUNK
    hk = heads * dk
    mw = heads * dv
    assert 2 * heads <= 8
    return pl.pallas_call(
        functools.partial(_mlstm_body, heads=heads, dk=dk, dv=dv, lc=lc),
        grid=(lp // lc,),
        in_specs=[
            pl.BlockSpec((lc, hk), lambda c: (c, 0)),
            pl.BlockSpec((lc, hk), lambda c: (c, 1)),
            pl.BlockSpec((lc, mw), lambda c: (c, 0)),
            pl.BlockSpec((lc, mw), lambda c: (c, 0)),
            pl.BlockSpec((lc, GATE_W), lambda c: (c, gate_blk)),
            pl.BlockSpec((2 * heads, lc), lambda c: (0, c)),
            pl.BlockSpec((CONV_W, 2 * hk), lambda c: (0, 0)),
            pl.BlockSpec((1, GATE_W), lambda c: (0, 0)),
            pl.BlockSpec((2 * heads, 1), lambda c: (0, 0)),
            pl.BlockSpec((1, mw), lambda c: (0, 0)),
        ],
        out_specs=pl.BlockSpec((lc, mw), lambda c: (c, 0)),
        out_shape=jax.ShapeDtypeStruct((lp, mw), BF16),
        scratch_shapes=[
            pltpu.VMEM((lc + HALO, 2 * hk), F32),
            pltpu.VMEM((heads, dk, dv), F32),
            pltpu.VMEM((8, dk), F32),
            pltpu.VMEM((8, 128), F32),
        ],
        compiler_params=_params(("arbitrary",)),
        name="mlstm",
    )(a_qk, a_qk, v, a_o, b_misc, gates_t, conv_w, bias_c, bias_r, g_mnorm)


def _rope128(t, tab):
    pr = t * tab
    rr = pr + pltpu.roll(pr, ROPE, axis=1)
    lane = lax.broadcasted_iota(jnp.int32, pr.shape, 1)
    return jnp.where(lane < ROPE, rr, 0.0)


def _qproj_body(c_ref, g_ref, w_ref, tab_ref, o_ref, cn, *, hpb, scale, bm, sub):
    @pl.when(pl.program_id(1) == 0)
    def _():
        def norm(rows):
            cn[rows, :] = _rms(c_ref[rows, :], g_ref[...]).astype(BF16)
        _row_loop(bm, sub, norm)

    def project(rows):
        acc = jnp.dot(cn[rows, :], w_ref[...], preferred_element_type=F32)
        tab = tab_ref[rows, :]
        for hh in range(hpb):
            base = hh * QK_W
            o_ref[rows, base:base + NOPE] = (acc[:, base:base + NOPE] * scale).astype(BF16)
            rr = _rope128(acc[:, base + NOPE:base + QK_W], tab)
            o_ref[rows, base + NOPE:base + QK_W] = (rr * scale).astype(BF16)
    _row_loop(bm, sub, project)


def _kvproj_body(c_ref, g_ref, w_ref, kr_ref, tab_ref, k_ref, vt_ref, *, hpb, av):
    cn = _rms(c_ref[...], g_ref[...]).astype(BF16)
    acc = jnp.dot(cn, w_ref[...], preferred_element_type=F32)
    kr = _rope128(kr_ref[...], tab_ref[...]).astype(BF16)
    for hh in range(hpb):
        src = hh * (NOPE + av)
        k_ref[:, hh * QK_W:hh * QK_W + NOPE] = acc[:, src:src + NOPE].astype(BF16)
        k_ref[:, hh * QK_W + NOPE:(hh + 1) * QK_W] = kr
        vt_ref[hh, 0, 0:av, :] = acc[:, src + NOPE:src + NOPE + av].T.astype(BF16)
        vt_ref[hh, 0, av:av + ONES_ROWS, :] = jnp.ones((ONES_ROWS, vt_ref.shape[3]), BF16)


def _qproj(b_misc, g_cq, w_uq_r, tab, *, bm, hpb, scale):
    lp = b_misc.shape[0]
    cw = w_uq_r.shape[0]
    n = w_uq_r.shape[1]
    bn = hpb * QK_W
    return pl.pallas_call(
        functools.partial(_qproj_body, hpb=hpb, scale=scale, bm=bm, sub=_div_tile(bm, MM_SUB_ROWS, 16)),
        grid=(lp // bm, n // bn),
        in_specs=[
            pl.BlockSpec((bm, cw), lambda i, j: (i, 0)),
            pl.BlockSpec((1, cw), lambda i, j: (0, 0)),
            pl.BlockSpec((cw, bn), lambda i, j: (0, j)),
            pl.BlockSpec((bm, ROPE_W), lambda i, j: (i, 0)),
        ],
        out_specs=pl.BlockSpec((bm, bn), lambda i, j: (i, j)),
        out_shape=jax.ShapeDtypeStruct((lp, n), BF16),
        scratch_shapes=[pltpu.VMEM((bm, cw), BF16)],
        compiler_params=_params(("parallel", "arbitrary")),
        name="mla_qproj",
    )(b_misc, g_cq, w_uq_r, tab)


def _kvproj(b_misc, g_ckv, w_ukv, tab, *, hpb, av, ckv_blk, kr_blk):
    lp = b_misc.shape[0]
    cw = w_ukv.shape[0]
    n = w_ukv.shape[1]
    heads = n // (NOPE + av)
    bn = hpb * (NOPE + av)
    bm = ATT_CHUNK
    return pl.pallas_call(
        functools.partial(_kvproj_body, hpb=hpb, av=av),
        grid=(lp // bm, n // bn),
        in_specs=[
            pl.BlockSpec((bm, cw), lambda i, j: (i, ckv_blk)),
            pl.BlockSpec((1, cw), lambda i, j: (0, 0)),
            pl.BlockSpec((cw, bn), lambda i, j: (0, j)),
            pl.BlockSpec((bm, ROPE_W), lambda i, j: (i, kr_blk)),
            pl.BlockSpec((bm, ROPE_W), lambda i, j: (i, 0)),
        ],
        out_specs=[
            pl.BlockSpec((bm, hpb * QK_W), lambda i, j: (i, j)),
            pl.BlockSpec((hpb, 1, av + ONES_ROWS, bm), lambda i, j: (j, i, 0, 0)),
        ],
        out_shape=[jax.ShapeDtypeStruct((lp, heads * QK_W), BF16),
                   jax.ShapeDtypeStruct((heads, lp // bm, av + ONES_ROWS, bm), BF16)],
        compiler_params=_params(("parallel", "parallel")),
        name="mla_kvproj",
    )(b_misc, g_ckv, w_ukv, b_misc, tab)


def _attn_body(q_ref, k_ref, vt_ref, o_ref, m_sc, acc_sc, s_even, s_odd, c_even, c_odd):
    i = pl.program_id(1)
    n_slab = ATT_TILE // ATT_SLAB
    meta0 = T0 - N_META
    all_slabs = tuple(range(n_slab))
    late_slabs = tuple(s for s in all_slabs if (s + 1) * ATT_SLAB > ATT_CHUNK)

    av = o_ref.shape[1]
    m_sc[...] = jnp.full_like(m_sc, NEG_SCORE)
    acc_sc[...] = jnp.zeros_like(acc_sc)

    def scores(j, buf, masked):
        s_buf, c_buf = buf
        start = j * ATT_CHUNK
        if not isinstance(j, int):
            start = pl.multiple_of(start, ATT_CHUNK)
        k = k_ref[pl.ds(start, ATT_CHUNK), :]
        if masked:
            kpos = j * ATT_CHUNK + lax.broadcasted_iota(jnp.int32, (ATT_CHUNK, ATT_SLAB), 0)
            qlane = lax.broadcasted_iota(jnp.int32, (ATT_CHUNK, ATT_SLAB), 1)
        for s in all_slabs:
            q = q_ref[s * ATT_SLAB:(s + 1) * ATT_SLAB, :]
            st = lax.dot_general(k, q, (((1,), (1,)), ((), ())), preferred_element_type=F32)
            if masked:
                qpos = i * ATT_TILE + s * ATT_SLAB + qlane
                keep = jnp.logical_and(kpos <= qpos, kpos >= meta0)
                st = jnp.where(keep, st, NEG_SCORE)
            s_buf[s] = st
            c_buf[s] = jnp.max(st, axis=0, keepdims=True)

    def absorb(j, buf, slabs):
        s_buf, c_buf = buf
        vt = vt_ref[0, j]
        for s in slabs:
            m_prev = m_sc[s]
            m_new = jnp.maximum(m_prev, c_buf[s])
            alpha = jnp.exp2(m_prev - m_new)
            p = jnp.exp2(s_buf[s] - m_new)
            acc_sc[s] = alpha * acc_sc[s] + jnp.dot(vt, p.astype(BF16), preferred_element_type=F32)
            m_sc[s] = m_new

    even = (s_even, c_even)
    odd = (s_odd, c_odd)

    def pair(p, mask_odd, slabs_odd, mask_next):
        scores(2 * p + 1, odd, mask_odd)
        absorb(2 * p, even, all_slabs)
        if mask_next is not None:
            scores(2 * p + 2, even, mask_next)
        absorb(2 * p + 1, odd, slabs_odd)

    scores(0, even, True)

    @pl.when(i == 1)
    def _():
        pair(0, False, all_slabs, True)

    @pl.when(i > 1)
    def _():
        pair(0, False, all_slabs, False)

        def mid(p, carry):
            pair(p, False, all_slabs, False)
            return carry
        lax.fori_loop(1, i - 1, mid, 0)
        pair(i - 1, False, all_slabs, True)

    pair(i, True, late_slabs, None)

    for s in all_slabs:
        out = (acc_sc[s, 0:av, :] / acc_sc[s, av:av + 1, :]).T
        o_ref[s * ATT_SLAB:(s + 1) * ATT_SLAB, :] = out.astype(o_ref.dtype)


def _attention(q, k, vt, *, heads, av):
    lp = q.shape[0]
    n_slab = ATT_TILE // ATT_SLAB
    assert lp % ATT_TILE == 0 and ATT_TILE % ATT_SLAB == 0 and ATT_TILE == 2 * ATT_CHUNK
    return pl.pallas_call(
        _attn_body,
        grid=(heads, lp // ATT_TILE),
        in_specs=[
            pl.BlockSpec((ATT_TILE, QK_W), lambda h, i: (i, h)),
            pl.BlockSpec((lp, QK_W), lambda h, i: (0, h)),
            pl.BlockSpec((1, lp // ATT_CHUNK, av + ONES_ROWS, ATT_CHUNK), lambda h, i: (h, 0, 0, 0)),
        ],
        out_specs=pl.BlockSpec((ATT_TILE, av), lambda h, i: (i, h)),
        out_shape=jax.ShapeDtypeStruct((lp, heads * av), BF16),
        scratch_shapes=[
            pltpu.VMEM((n_slab, 1, ATT_SLAB), F32),
            pltpu.VMEM((n_slab, av + ONES_ROWS, ATT_SLAB), F32),
            pltpu.VMEM((n_slab, ATT_CHUNK, ATT_SLAB), F32),
            pltpu.VMEM((n_slab, ATT_CHUNK, ATT_SLAB), F32),
            pltpu.VMEM((n_slab, 1, ATT_SLAB), F32),
            pltpu.VMEM((n_slab, 1, ATT_SLAB), F32),
        ],
        compiler_params=_params(("parallel", "parallel")),
        name="mla_attention",
    )(q, k, vt)


def _cast_rows_body(x_ref, o_ref, *, n_src):
    @pl.when(pl.program_id(0) < n_src)
    def _():
        o_ref[...] = x_ref[...].astype(BF16)

    @pl.when(pl.program_id(0) >= n_src)
    def _():
        o_ref[...] = jnp.zeros_like(o_ref)


def _cast_rows(w, layer, rows_out=None):
    _, rows, cols = w.shape
    rows_out = rows if rows_out is None else rows_out
    rb = _div_tile(rows, CAST_ROWS, 8)
    assert rows_out % rb == 0
    n_src = rows // rb
    return pl.pallas_call(
        functools.partial(_cast_rows_body, n_src=n_src),
        grid=(rows_out // rb,),
        in_specs=[pl.BlockSpec((None, rb, cols), lambda i: (layer, jnp.minimum(i, n_src - 1), 0))],
        out_specs=pl.BlockSpec((rb, cols), lambda i: (i, 0)),
        out_shape=jax.ShapeDtypeStruct((rows_out, cols), BF16),
        compiler_params=_params(("parallel",)),
        name="cast_bf16",
    )(w)


def _cast_gate_up_body(x_ref, g_ref, u_ref, *, ff):
    pad = g_ref.shape[1] - ff
    g_ref[:, :ff] = x_ref[:, :ff].astype(BF16)
    u_ref[:, :ff] = x_ref[:, ff:].astype(BF16)
    if pad:
        g_ref[:, ff:] = jnp.zeros((g_ref.shape[0], pad), BF16)
        u_ref[:, ff:] = jnp.zeros((u_ref.shape[0], pad), BF16)


def _cast_gate_up(w_gu, layer, ffp):
    _, d, ff2 = w_gu.shape
    ff = ff2 // 2
    rb = _div_tile(d, CAST_ROWS // 2, 8)
    out = jax.ShapeDtypeStruct((d, ffp), BF16)
    return pl.pallas_call(
        functools.partial(_cast_gate_up_body, ff=ff),
        grid=(d // rb,),
        in_specs=[pl.BlockSpec((None, rb, ff2), lambda i: (layer, i, 0))],
        out_specs=[pl.BlockSpec((rb, ffp), lambda i: (i, 0))] * 2,
        out_shape=[out, out],
        compiler_params=_params(("parallel",)),
        name="cast_gate_up",
    )(w_gu)


def _swap_half(w):
    half = w.shape[-1] // 2
    return jnp.concatenate([-w[..., half:], w[..., :half]], axis=-1)


def _prep_layer_weights(layer, w_in_all, w_uq_all, w_ukv_all, w_out_all, w_gu_all, w_down_all, *, mw, hk, ffp):
    w_uq = _cast_rows(w_uq_all, layer)
    d = w_in_all.shape[1]
    o_gate = 2 * hk + 2 * mw
    o_cq = o_gate + 2 * M_HEADS
    o_ckv = o_cq + Q_LORA
    o_kr = o_ckv + KV_LORA
    w_main = _cast_rows(w_in_all, layer)
    w_tail = w_main[:, o_gate:]
    wkr = w_tail[:, o_kr - o_gate:o_kr - o_gate + ROPE]
    gate_pad = jnp.zeros((d, GATE_W - 2 * M_HEADS), w_tail.dtype)
    used = Q_LORA + KV_LORA + GATE_W + ROPE_W
    tail_pad = jnp.zeros((d, -used % MISC_BN), w_tail.dtype)
    w_b = jnp.concatenate([w_tail[:, o_cq - o_gate:o_kr - o_gate], w_tail[:, :o_cq - o_gate], gate_pad, wkr,
                           _swap_half(wkr), tail_pad], axis=1)

    uq = w_uq.reshape(Q_LORA, A_HEADS, NOPE + ROPE)
    uq_r = jnp.concatenate([uq[..., :NOPE], uq[..., NOPE:], _swap_half(uq[..., NOPE:])], axis=-1)
    uq_r = uq_r.reshape(Q_LORA, A_HEADS * QK_W)

    w_gate, w_up = _cast_gate_up(w_gu_all, layer, ffp)
    return dict(w_in=w_main, w_b=w_b, uq=uq_r, ukv=_cast_rows(w_ukv_all, layer),
                w_out=_cast_rows(w_out_all, layer), w_gate=w_gate, w_up=w_up,
                w_down=_cast_rows(w_down_all, layer, ffp))


def _rope_table(lp):
    meta0 = T0 - N_META
    pos = jnp.maximum(jnp.arange(lp, dtype=jnp.int32) - meta0, 0).astype(F32)
    inv_freq = ROPE_THETA ** (-jnp.arange(ROPE // 2, dtype=F32) / (ROPE // 2))
    ang = pos[:, None] * inv_freq[None, :]
    cos, sin = jnp.cos(ang), jnp.sin(ang)
    return jnp.concatenate([cos, cos, sin, sin], axis=-1)


def kernel(x, meta, g_mix_pre, w_in, conv_w, b_gates, g_mnorm, g_cq, w_uq, g_ckv, w_ukv, w_out,
           g_mix_post, g_ffn_pre, w_gu, w_down, g_ffn_post):
    batch, seq, d = x.shape
    assert batch == 1 and seq % T0 == 0
    depth = w_in.shape[0]
    lp = T0 + seq
    mw = d // 2
    dv = mw // M_HEADS
    dk = dv // 2
    hk = M_HEADS * dk
    aw = d - mw
    av = aw // A_HEADS
    ff = w_down.shape[1]
    assert Q_LORA % KV_LORA == 0 and KV_LORA % ROPE_W == 0 and ROPE_W == GATE_W and 2 * hk == mw

    bf = 512
    ffp = -(-ff // 1024) * 1024
    bm = _div_tile(lp, 1664, 128)
    bn_in = _div_tile(mw, 512, 128)
    scale = (NOPE + ROPE) ** -0.5 * LOG2_E

    ckv_blk = Q_LORA // KV_LORA
    gate_blk = (Q_LORA + KV_LORA) // GATE_W
    kr_blk = (Q_LORA + KV_LORA + GATE_W) // ROPE_W
    nb = -(-(Q_LORA + KV_LORA + GATE_W + ROPE_W) // MISC_BN) * MISC_BN

    tab = _rope_table(lp)
    h, u = _prep(x[0], meta, g_mix_pre[0][None])

    out = None
    for l in range(depth):
        w = _prep_layer_weights(l, w_in, w_uq, w_ukv, w_out, w_gu, w_down, mw=mw, hk=hk, ffp=ffp)
        a_qk = _matmul([u], [(w["w_in"], 0, 0)], mw, F32, bm=bm, bn=bn_in, name="inproj_qk")
        v_m = _matmul([u], [(w["w_in"], 0, mw // bn_in)], mw, BF16, bm=bm, bn=bn_in, name="inproj_v")
        a_o = _matmul([u], [(w["w_in"], 0, 2 * mw // bn_in)], mw, F32, bm=bm, bn=bn_in, name="inproj_o")
        b_misc = _matmul([u], [(w["w_b"], 0, 0)], nb, F32, bm=bm, bn=MISC_BN,
                         name="inproj_misc")

        gates_t = b_misc[:, Q_LORA + KV_LORA:Q_LORA + KV_LORA + 2 * M_HEADS].T
        bias_c = jnp.pad(b_gates[l], (0, GATE_W - 2 * M_HEADS))[None, :]
        bias_r = b_gates[l][:, None]
        h_m = _mlstm(a_qk, v_m, a_o, b_misc, gates_t, conv_w[l], bias_c, bias_r, g_mnorm[l][None],
                     heads=M_HEADS, dk=dk, dv=dv, gate_blk=gate_blk)

        q_a = _qproj(b_misc, g_cq[l][None], w["uq"], tab, bm=_div_tile(lp, 832, 64), hpb=4, scale=scale)
        k_a, vt_a = _kvproj(b_misc, g_ckv[l][None], w["ukv"], tab, hpb=4, av=av,
                            ckv_blk=ckv_blk, kr_blk=kr_blk)
        h_a = _attention(q_a, k_a, vt_a, heads=A_HEADS, av=av)

        mix = _matmul([h_m, h_a], [(w["w_out"], 0, 0), (w["w_out"], 1, 0)], d, F32, bm=bm,
                      bn=_div_tile(d, 512, 128), name="outproj")
        h, u = _resnorm(h, mix, g_mix_post[l][None], g_ffn_pre[l][None])

        act = _matmul([u], [(w["w_gate"], 0, 0), (w["w_up"], 0, 0)], ffp, BF16,
                      bm=bm, bn=bf, swiglu=True, name="ffn_up")
        y = _matmul([act], [(w["w_down"], 0, 0)], d, F32, bm=bm, bn=_div_tile(d, 1024, 128),
                    bk=_div_tile(ffp, 2816, 128), name="ffn_down")
        if l + 1 < depth:
            h, u = _resnorm(h, y, g_ffn_post[l][None], g_mix_pre[l + 1][None])
        else:
            out = _resnorm_final(h, y, g_ffn_post[l][None])
    return out[None]
```

```python
import functools

import jax
import jax.numpy as jnp
from jax import lax
from jax.experimental import pallas as pl
from jax.experimental.pallas import tpu as pltpu

N_META = 16
M_HEADS = 4
CONV_W = 4
A_HEADS = 16
NOPE = 128
ROPE = 64
Q_LORA = 1536
KV_LORA = 512
ROPE_THETA = 10000.0
NORM_EPS = 1e-6
NEG_SCORE = -1e30
LOG2_E = 1.4426950408889634

T0 = 512
MLSTM_CHUNK = 256
ROW_BLOCK = 256
HALO = 8
QK_W = NOPE + 2 * ROPE
GATE_W = 128
ROPE_W = 2 * ROPE

VMEM_LIMIT_BYTES = 56 * 1024 * 1024
MM_SUB_ROWS = 512
CAST_ROWS = 128
ONES_ROWS = 16
ATT_SLAB = 512
ATT_CHUNK = 768
ATT_TILE = 2 * ATT_CHUNK

F32 = jnp.float32
BF16 = jnp.bfloat16


def _div_tile(n, target, mult):
    best = None
    t = mult
    while t <= min(n, target):
        if n % t == 0:
            best = t
        t += mult
    if best is None:
        raise ValueError(f"no tile for {n} (target {target}, multiple of {mult})")
    return best


def _params(sem, flags=None):
    return pltpu.CompilerParams(dimension_semantics=sem, vmem_limit_bytes=VMEM_LIMIT_BYTES, flags=flags)


def _rms(t, g):
    return t * lax.rsqrt(jnp.mean(t * t, axis=-1, keepdims=True) + NORM_EPS) * g


def _row_loop(nrows, sub, fn):
    def step(r, carry):
        fn(pl.ds(pl.multiple_of(r * sub, sub), sub))
        return carry
    lax.fori_loop(0, nrows // sub, step, 0, unroll=True)


def _prep_body(x_ref, meta_ref, g_ref, h_ref, u_ref):
    i = pl.program_id(0)

    @pl.when(i == 0)
    def _():
        h_ref[...] = jnp.zeros_like(h_ref)
        h_ref[T0 - N_META:T0, :] = meta_ref[...]

    @pl.when(i > 0)
    def _():
        h_ref[...] = x_ref[...]

    u_ref[...] = _rms(h_ref[...], g_ref[...]).astype(BF16)


def _prep(x2d, meta, g):
    seq, d = x2d.shape
    lp = T0 + seq
    return pl.pallas_call(
        _prep_body,
        grid=(lp // T0,),
        in_specs=[
            pl.BlockSpec((T0, d), lambda i: (jnp.maximum(i - 1, 0), 0)),
            pl.BlockSpec((N_META, d), lambda i: (0, 0)),
            pl.BlockSpec((1, d), lambda i: (0, 0)),
        ],
        out_specs=[
            pl.BlockSpec((T0, d), lambda i: (i, 0)),
            pl.BlockSpec((T0, d), lambda i: (i, 0)),
        ],
        out_shape=[jax.ShapeDtypeStruct((lp, d), F32), jax.ShapeDtypeStruct((lp, d), BF16)],
        compiler_params=_params(("parallel",)),
        name="prep_norm",
    )(x2d, meta, g)


def _mm_body(*refs, n_lhs, nk, swiglu, bm, sub):
    lhs = refs[0:n_lhs]
    rhs = refs[n_lhs:-1]
    o_ref = refs[-1]

    def product(rows, b):
        acc = None
        for a, w in zip(lhs, b):
            d = jnp.dot(a[rows, :], w[...], preferred_element_type=F32)
            acc = d if acc is None else acc + d
        return acc

    def finish(rows):
        if swiglu:
            gate = product(rows, rhs[0:1])
            acc = gate * jax.nn.sigmoid(gate) * product(rows, rhs[1:2])
        else:
            acc = product(rows, rhs)
        o_ref[rows, :] = acc.astype(o_ref.dtype)

    if nk == 1:
        _row_loop(bm, sub, finish)
        return

    k = pl.program_id(2)

    def first(rows):
        o_ref[rows, :] = product(rows, rhs)

    def accumulate(rows):
        o_ref[rows, :] += product(rows, rhs)

    @pl.when(k == 0)
    def _():
        _row_loop(bm, sub, first)

    @pl.when(k > 0)
    def _():
        _row_loop(bm, sub, accumulate)


def _matmul(lhs, rhs, n, out_dtype, *, bm, bn, bk=None, swiglu=False, name):
    m = lhs[0].shape[0]
    if bk is None:
        nk = 1
    else:
        assert len(lhs) == 1 and not swiglu and out_dtype == F32
        nk = lhs[0].shape[1] // bk
    in_specs = []
    for a in lhs:
        kk = a.shape[1] if bk is None else bk
        in_specs.append(pl.BlockSpec((bm, kk), lambda i, j, k: (i, k)))
    for p, (w, rb, cb) in enumerate(rhs):
        a = lhs[0] if swiglu else lhs[p]
        kk = a.shape[1] if bk is None else bk
        in_specs.append(pl.BlockSpec((kk, bn), lambda i, j, k, rb=rb, cb=cb: (k + rb, j + cb)))
    return pl.pallas_call(
        functools.partial(_mm_body, n_lhs=len(lhs), nk=nk, swiglu=swiglu, bm=bm,
                          sub=_div_tile(bm, MM_SUB_ROWS, 16)),
        grid=(m // bm, n // bn, nk),
        in_specs=in_specs,
        out_specs=pl.BlockSpec((bm, bn), lambda i, j, k: (i, j)),
        out_shape=jax.ShapeDtypeStruct((m, n), out_dtype),
        compiler_params=_params(("parallel", "parallel", "arbitrary")),
        name=name,
    )(*lhs, *[w for w, _, _ in rhs])


def _resnorm_body(x_ref, y_ref, gp_ref, gn_ref, xo_ref, u_ref, *, rows):
    i = pl.program_id(0)
    row = i * rows + lax.broadcasted_iota(jnp.int32, (rows, 1), 0)
    xn = x_ref[...] + _rms(y_ref[...], gp_ref[...])
    xn = jnp.where(row >= T0 - N_META, xn, 0.0)
    xo_ref[...] = xn
    u_ref[...] = _rms(xn, gn_ref[...]).astype(BF16)


def _resnorm(x, y, g_post, g_next):
    lp, d = x.shape
    rows = ROW_BLOCK
    blk = pl.BlockSpec((rows, d), lambda i: (i, 0))
    vec = pl.BlockSpec((1, d), lambda i: (0, 0))
    return pl.pallas_call(
        functools.partial(_resnorm_body, rows=rows),
        grid=(lp // rows,),
        in_specs=[blk, blk, vec, vec],
        out_specs=[blk, blk],
        out_shape=[jax.ShapeDtypeStruct((lp, d), F32), jax.ShapeDtypeStruct((lp, d), BF16)],
        compiler_params=_params(("parallel",)),
        name="resnorm",
    )(x, y, g_post, g_next)


def _resnorm_final_body(x_ref, y_ref, gp_ref, o_ref):
    o_ref[...] = x_ref[...] + _rms(y_ref[...], gp_ref[...])


def _resnorm_final(x, y, g_post):
    lp, d = x.shape
    rows = ROW_BLOCK
    skip = T0 // rows
    blk_in = pl.BlockSpec((rows, d), lambda i: (i + skip, 0))
    return pl.pallas_call(
        _resnorm_final_body,
        grid=((lp - T0) // rows,),
        in_specs=[blk_in, blk_in, pl.BlockSpec((1, d), lambda i: (0, 0))],
        out_specs=pl.BlockSpec((rows, d), lambda i: (i, 0)),
        out_shape=jax.ShapeDtypeStruct((lp - T0, d), F32),
        compiler_params=_params(("parallel",)),
        name="resnorm_final",
    )(x, y, g_post)


def _log_sigmoid(x):
    return jnp.minimum(x, 0.0) - jnp.log1p(jnp.exp(-jnp.abs(x)))


def _mlstm_body(q_ref, k_ref, v_ref, o_ref, gc_ref, gr_ref, cw_ref, bc_ref, br_ref, gn_ref,
                out_ref, xext, c_sc, n_sc, m_sc, *, heads, dk, dv, lc):
    c = pl.program_id(0)
    hk = heads * dk
    meta0 = T0 - N_META

    @pl.when(c == 0)
    def _():
        xext[0:HALO, :] = jnp.zeros((HALO, 2 * hk), F32)
        c_sc[...] = jnp.zeros_like(c_sc)
        n_sc[...] = jnp.zeros_like(n_sc)
        m_sc[...] = jnp.zeros_like(m_sc)

    xext[HALO:HALO + lc, 0:hk] = q_ref[...]
    xext[HALO:HALO + lc, hk:2 * hk] = k_ref[...]
    conv = None
    for j in range(CONV_W):
        term = cw_ref[j:j + 1, :] * xext[pl.ds(HALO - (CONV_W - 1) + j, lc), :]
        conv = term if conv is None else conv + term
    xext[0:HALO, :] = xext[lc:lc + HALO, :]
    qk = conv * jax.nn.sigmoid(conv)

    row = c * lc + lax.broadcasted_iota(jnp.int32, (lc, 1), 0)
    col = c * lc + lax.broadcasted_iota(jnp.int32, (1, lc), 1)
    valid_c = row >= meta0
    valid_r = col >= meta0
    gc = gc_ref[...] + bc_ref[...]
    gr = gr_ref[...] + br_ref[...]
    tt = lax.broadcasted_iota(jnp.int32, (lc, lc), 0)
    ss = lax.broadcasted_iota(jnp.int32, (lc, lc), 1)
    tril = ss <= tt
    triu = tt <= ss
    neg_inf = -jnp.inf

    for h in range(heads):
        li_c = jnp.where(valid_c, gc[:, h:h + 1], neg_inf)
        lf_c = jnp.where(valid_c, _log_sigmoid(gc[:, heads + h:heads + h + 1]), 0.0)
        li_r = jnp.where(valid_r, gr[h:h + 1, :], neg_inf)
        lf_r = jnp.where(valid_r, _log_sigmoid(gr[heads + h:heads + h + 1, :]), 0.0)
        b_c = jnp.sum(jnp.where(tril, lf_r, 0.0), axis=1, keepdims=True)
        b_r = jnp.sum(jnp.where(triu, lf_c, 0.0), axis=0, keepdims=True)
        g = jnp.sum(lf_r, axis=1, keepdims=True)
        m = m_sc[h:h + 1, 0:1]

        d = jnp.where(tril, b_c - b_r + li_r, neg_inf)
        inter = b_c + m
        m_t = jnp.maximum(inter, jnp.max(d, axis=1, keepdims=True))
        w_inter = jnp.exp(inter - m_t)
        p = jnp.exp(d - m_t)

        qh = qk[:, h * dk:(h + 1) * dk] * (dk ** -0.5)
        kh = qk[:, hk + h * dk:hk + (h + 1) * dk]
        qb = qh.astype(BF16)
        s = lax.dot_general(qb, kh.astype(BF16), (((1,), (1,)), ((), ())),
                            preferred_element_type=F32) * p
        vh = v_ref[:, h * dv:(h + 1) * dv]
        ch = c_sc[h]
        nh = n_sc[h:h + 1, :]
        num = (w_inter * jnp.dot(qb, ch.astype(BF16), preferred_element_type=F32)
               + jnp.dot(s.astype(BF16), vh, preferred_element_type=F32))
        den = (w_inter * jnp.sum(qh * nh, axis=1, keepdims=True)
               + jnp.sum(s, axis=1, keepdims=True))
        hh = num / jnp.maximum(jnp.abs(den), jnp.exp(-m_t))

        a_c = g - b_c + li_c
        m_new = jnp.maximum(g + m, jnp.max(a_c, axis=0, keepdims=True))
        decay = jnp.exp(g + m - m_new)
        wk = kh * jnp.exp(a_c - m_new)
        c_sc[h] = decay * ch + lax.dot_general(wk.astype(BF16), vh, (((0,), (0,)), ((), ())),
                                               preferred_element_type=F32)
        n_sc[h:h + 1, :] = decay * nh + jnp.sum(wk, axis=0, keepdims=True)
        m_sc[h:h + 1, :] = jnp.broadcast_to(m_new, (1, m_sc.shape[1]))

        hn = _rms(hh, gn_ref[:, h * dv:(h + 1) * dv])
        og = jax.nn.sigmoid(o_ref[:, h * dv:(h + 1) * dv])
        out_ref[:, h * dv:(h + 1) * dv] = (og * hn).astype(BF16)


def _mlstm(a_qk, v, a_o, b_misc, gates_t, conv_w, bias_c, bias_r, g_mnorm, *, heads, dk, dv, gate_blk):
    lp = v.shape[0]
    lc = MLSTM_CHUNK
    hk = heads * dk
    mw = heads * dv
    assert 2 * heads <= 8
    return pl.pallas_call(
        functools.partial(_mlstm_body, heads=heads, dk=dk, dv=dv, lc=lc),
        grid=(lp // lc,),
        in_specs=[
            pl.BlockSpec((lc, hk), lambda c: (c, 0)),
            pl.BlockSpec((lc, hk), lambda c: (c, 1)),
            pl.BlockSpec((lc, mw), lambda c: (c, 0)),
            pl.BlockSpec((lc, mw), lambda c: (c, 0)),
            pl.BlockSpec((lc, GATE_W), lambda c: (c, gate_blk)),
            pl.BlockSpec((2 * heads, lc), lambda c: (0, c)),
            pl.BlockSpec((CONV_W, 2 * hk), lambda c: (0, 0)),
            pl.BlockSpec((1, GATE_W), lambda c: (0, 0)),
            pl.BlockSpec((2 * heads, 1), lambda c: (0, 0)),
            pl.BlockSpec((1, mw), lambda c: (0, 0)),
        ],
        out_specs=pl.BlockSpec((lc, mw), lambda c: (c, 0)),
        out_shape=jax.ShapeDtypeStruct((lp, mw), BF16),
        scratch_shapes=[
            pltpu.VMEM((lc + HALO, 2 * hk), F32),
            pltpu.VMEM((heads, dk, dv), F32),
            pltpu.VMEM((8, dk), F32),
            pltpu.VMEM((8, 128), F32),
        ],
        compiler_params=_params(("arbitrary",)),
        name="mlstm",
    )(a_qk, a_qk, v, a_o, b_misc, gates_t, conv_w, bias_c, bias_r, g_mnorm)


def _rope128(t, tab):
    pr = t * tab
    rr = pr + pltpu.roll(pr, ROPE, axis=1)
    lane = lax.broadcasted_iota(jnp.int32, pr.shape, 1)
    return jnp.where(lane < ROPE, rr, 0.0)


def _qproj_body(c_ref, g_ref, w_ref, tab_ref, o_ref, cn, *, hpb, scale, bm, sub):
    @pl.when(pl.program_id(1) == 0)
    def _():
        def norm(rows):
            cn[rows, :] = _rms(c_ref[rows, :], g_ref[...]).astype(BF16)
        _row_loop(bm, sub, norm)

    def project(rows):
        acc = jnp.dot(cn[rows, :], w_ref[...], preferred_element_type=F32)
        tab = tab_ref[rows, :]
        for hh in range(hpb):
            base = hh * QK_W
            o_ref[rows, base:base + NOPE] = (acc[:, base:base + NOPE] * scale).astype(BF16)
            rr = _rope128(acc[:, base + NOPE:base + QK_W], tab)
            o_ref[rows, base + NOPE:base + QK_W] = (rr * scale).astype(BF16)
    _row_loop(bm, sub, project)


def _kvproj_body(c_ref, g_ref, w_ref, kr_ref, tab_ref, k_ref, vt_ref, *, hpb, av):
    cn = _rms(c_ref[...], g_ref[...]).astype(BF16)
    acc = jnp.dot(cn, w_ref[...], preferred_element_type=F32)
    kr = _rope128(kr_ref[...], tab_ref[...]).astype(BF16)
    for hh in range(hpb):
        src = hh * (NOPE + av)
        k_ref[:, hh * QK_W:hh * QK_W + NOPE] = acc[:, src:src + NOPE].astype(BF16)
        k_ref[:, hh * QK_W + NOPE:(hh + 1) * QK_W] = kr
        vt_ref[hh, 0, 0:av, :] = acc[:, src + NOPE:src + NOPE + av].T.astype(BF16)
        vt_ref[hh, 0, av:av + ONES_ROWS, :] = jnp.ones((ONES_ROWS, vt_ref.shape[3]), BF16)


def _qproj(b_misc, g_cq, w_uq_r, tab, *, bm, hpb, scale):
    lp = b_misc.shape[0]
    cw = w_uq_r.shape[0]
    n = w_uq_r.shape[1]
    bn = hpb * QK_W
    return pl.pallas_call(
        functools.partial(_qproj_body, hpb=hpb, scale=scale, bm=bm, sub=_div_tile(bm, MM_SUB_ROWS, 16)),
        grid=(lp // bm, n // bn),
        in_specs=[
            pl.BlockSpec((bm, cw), lambda i, j: (i, 0)),
            pl.BlockSpec((1, cw), lambda i, j: (0, 0)),
            pl.BlockSpec((cw, bn), lambda i, j: (0, j)),
            pl.BlockSpec((bm, ROPE_W), lambda i, j: (i, 0)),
        ],
        out_specs=pl.BlockSpec((bm, bn), lambda i, j: (i, j)),
        out_shape=jax.ShapeDtypeStruct((lp, n), BF16),
        scratch_shapes=[pltpu.VMEM((bm, cw), BF16)],
        compiler_params=_params(("parallel", "arbitrary")),
        name="mla_qproj",
    )(b_misc, g_cq, w_uq_r, tab)


def _kvproj(b_misc, g_ckv, w_ukv, tab, *, hpb, av, ckv_blk, kr_blk):
    lp = b_misc.shape[0]
    cw = w_ukv.shape[0]
    n = w_ukv.shape[1]
    heads = n // (NOPE + av)
    bn = hpb * (NOPE + av)
    bm = ATT_CHUNK
    return pl.pallas_call(
        functools.partial(_kvproj_body, hpb=hpb, av=av),
        grid=(lp // bm, n // bn),
        in_specs=[
            pl.BlockSpec((bm, cw), lambda i, j: (i, ckv_blk)),
            pl.BlockSpec((1, cw), lambda i, j: (0, 0)),
            pl.BlockSpec((cw, bn), lambda i, j: (0, j)),
            pl.BlockSpec((bm, ROPE_W), lambda i, j: (i, kr_blk)),
            pl.BlockSpec((bm, ROPE_W), lambda i, j: (i, 0)),
        ],
        out_specs=[
            pl.BlockSpec((bm, hpb * QK_W), lambda i, j: (i, j)),
            pl.BlockSpec((hpb, 1, av + ONES_ROWS, bm), lambda i, j: (j, i, 0, 0)),
        ],
        out_shape=[jax.ShapeDtypeStruct((lp, heads * QK_W), BF16),
                   jax.ShapeDtypeStruct((heads, lp // bm, av + ONES_ROWS, bm), BF16)],
        compiler_params=_params(("parallel", "parallel")),
        name="mla_kvproj",
    )(b_misc, g_ckv, w_ukv, b_misc, tab)


def _attn_body(q_ref, k_ref, vt_ref, o_ref, m_sc, acc_sc, s_even, s_odd, c_even, c_odd):
    i = pl.program_id(1)
    n_slab = ATT_TILE // ATT_SLAB
    meta0 = T0 - N_META
    all_slabs = tuple(range(n_slab))
    late_slabs = tuple(s for s in all_slabs if (s + 1) * ATT_SLAB > ATT_CHUNK)

    av = o_ref.shape[1]
    m_sc[...] = jnp.full_like(m_sc, NEG_SCORE)
    acc_sc[...] = jnp.zeros_like(acc_sc)

    def scores(j, buf, masked):
        s_buf, c_buf = buf
        start = j * ATT_CHUNK
        if not isinstance(j, int):
            start = pl.multiple_of(start, ATT_CHUNK)
        k = k_ref[pl.ds(start, ATT_CHUNK), :]
        if masked:
            kpos = j * ATT_CHUNK + lax.broadcasted_iota(jnp.int32, (ATT_CHUNK, ATT_SLAB), 0)
            qlane = lax.broadcasted_iota(jnp.int32, (ATT_CHUNK, ATT_SLAB), 1)
        for s in all_slabs:
            q = q_ref[s * ATT_SLAB:(s + 1) * ATT_SLAB, :]
            st = lax.dot_general(k, q, (((1,), (1,)), ((), ())), preferred_element_type=F32)
            if masked:
                qpos = i * ATT_TILE + s * ATT_SLAB + qlane
                keep = jnp.logical_and(kpos <= qpos, kpos >= meta0)
                st = jnp.where(keep, st, NEG_SCORE)
            s_buf[s] = st
            c_buf[s] = jnp.max(st, axis=0, keepdims=True)

    def absorb(j, buf, slabs):
        s_buf, c_buf = buf
        vt = vt_ref[0, j]
        for s in slabs:
            m_prev = m_sc[s]
            m_new = jnp.maximum(m_prev, c_buf[s])
            alpha = jnp.exp2(m_prev - m_new)
            p = jnp.exp2(s_buf[s] - m_new)
            acc_sc[s] = alpha * acc_sc[s] + jnp.dot(vt, p.astype(BF16), preferred_element_type=F32)
            m_sc[s] = m_new

    even = (s_even, c_even)
    odd = (s_odd, c_odd)

    def pair(p, mask_odd, slabs_odd, mask_next):
        scores(2 * p + 1, odd, mask_odd)
        absorb(2 * p, even, all_slabs)
        if mask_next is not None:
            scores(2 * p + 2, even, mask_next)
        absorb(2 * p + 1, odd, slabs_odd)

    scores(0, even, True)

    @pl.when(i == 1)
    def _():
        pair(0, False, all_slabs, True)

    @pl.when(i > 1)
    def _():
        pair(0, False, all_slabs, False)

        def mid(p, carry):
            pair(p, False, all_slabs, False)
            return carry
        lax.fori_loop(1, i - 1, mid, 0)
        pair(i - 1, False, all_slabs, True)

    pair(i, True, late_slabs, None)

    for s in all_slabs:
        out = (acc_sc[s, 0:av, :] / acc_sc[s, av:av + 1, :]).T
        o_ref[s * ATT_SLAB:(s + 1) * ATT_SLAB, :] = out.astype(o_ref.dtype)


def _attention(q, k, vt, *, heads, av):
    lp = q.shape[0]
    n_slab = ATT_TILE // ATT_SLAB
    assert lp % ATT_TILE == 0 and ATT_TILE % ATT_SLAB == 0 and ATT_TILE == 2 * ATT_CHUNK
    return pl.pallas_call(
        _attn_body,
        grid=(heads, lp // ATT_TILE),
        in_specs=[
            pl.BlockSpec((ATT_TILE, QK_W), lambda h, i: (i, h)),
            pl.BlockSpec((lp, QK_W), lambda h, i: (0, h)),
            pl.BlockSpec((1, lp // ATT_CHUNK, av + ONES_ROWS, ATT_CHUNK), lambda h, i: (h, 0, 0, 0)),
        ],
        out_specs=pl.BlockSpec((ATT_TILE, av), lambda h, i: (i, h)),
        out_shape=jax.ShapeDtypeStruct((lp, heads * av), BF16),
        scratch_shapes=[
            pltpu.VMEM((n_slab, 1, ATT_SLAB), F32),
            pltpu.VMEM((n_slab, av + ONES_ROWS, ATT_SLAB), F32),
            pltpu.VMEM((n_slab, ATT_CHUNK, ATT_SLAB), F32),
            pltpu.VMEM((n_slab, ATT_CHUNK, ATT_SLAB), F32),
            pltpu.VMEM((n_slab, 1, ATT_SLAB), F32),
            pltpu.VMEM((n_slab, 1, ATT_SLAB), F32),
        ],
        compiler_params=_params(("parallel", "parallel")),
        name="mla_attention",
    )(q, k, vt)


def _cast_rows_body(x_ref, o_ref, *, n_src):
    @pl.when(pl.program_id(0) < n_src)
    def _():
        o_ref[...] = x_ref[...].astype(BF16)

    @pl.when(pl.program_id(0) >= n_src)
    def _():
        o_ref[...] = jnp.zeros_like(o_ref)


def _cast_rows(w, layer, rows_out=None):
    _, rows, cols = w.shape
    rows_out = rows if rows_out is None else rows_out
    rb = _div_tile(rows, CAST_ROWS, 8)
    assert rows_out % rb == 0
    n_src = rows // rb
    return pl.pallas_call(
        functools.partial(_cast_rows_body, n_src=n_src),
        grid=(rows_out // rb,),
        in_specs=[pl.BlockSpec((None, rb, cols), lambda i: (layer, jnp.minimum(i, n_src - 1), 0))],
        out_specs=pl.BlockSpec((rb, cols), lambda i: (i, 0)),
        out_shape=jax.ShapeDtypeStruct((rows_out, cols), BF16),
        compiler_params=_params(("parallel",)),
        name="cast_bf16",
    )(w)


def _cast_transposed_body(x_ref, o_ref):
    o_ref[...] = x_ref[...].T.astype(BF16)


def _cast_transposed(w_t, layer):
    _, cols, rows = w_t.shape
    cb = 2 * CAST_ROWS
    return pl.pallas_call(
        _cast_transposed_body,
        grid=(pl.cdiv(cols, cb),),
        in_specs=[pl.BlockSpec((None, cb, rows), lambda i: (layer, i, 0))],
        out_specs=pl.BlockSpec((rows, cb), lambda i: (0, i)),
        out_shape=jax.ShapeDtypeStruct((rows, cols), BF16),
        compiler_params=_params(("parallel",)),
        name="cast_transposed",
    )(w_t)


def _cast_gate_up_body(x_ref, g_ref, u_ref, *, ff):
    pad = g_ref.shape[1] - ff
    g_ref[:, :ff] = x_ref[:, :ff].astype(BF16)
    u_ref[:, :ff] = x_ref[:, ff:].astype(BF16)
    if pad:
        g_ref[:, ff:] = jnp.zeros((g_ref.shape[0], pad), BF16)
        u_ref[:, ff:] = jnp.zeros((u_ref.shape[0], pad), BF16)


def _cast_gate_up(w_gu, layer, ffp):
    _, d, ff2 = w_gu.shape
    ff = ff2 // 2
    rb = _div_tile(d, CAST_ROWS // 2, 8)
    out = jax.ShapeDtypeStruct((d, ffp), BF16)
    return pl.pallas_call(
        functools.partial(_cast_gate_up_body, ff=ff),
        grid=(d // rb,),
        in_specs=[pl.BlockSpec((None, rb, ff2), lambda i: (layer, i, 0))],
        out_specs=[pl.BlockSpec((rb, ffp), lambda i: (i, 0))] * 2,
        out_shape=[out, out],
        compiler_params=_params(("parallel",)),
        name="cast_gate_up",
    )(w_gu)


def _swap_half(w):
    half = w.shape[-1] // 2
    return jnp.concatenate([-w[..., half:], w[..., :half]], axis=-1)


def _prep_layer_weights(layer, w_in_all, w_uq_all, w_ukv_all, w_out_all, w_gu_all, w_down_all, *, mw, hk, ffp):
    w_uq = _cast_rows(w_uq_all, layer)
    d = w_in_all.shape[1]
    o_gate = 2 * hk + 2 * mw
    o_cq = o_gate + 2 * M_HEADS
    o_ckv = o_cq + Q_LORA
    o_kr = o_ckv + KV_LORA
    w_main = _cast_transposed(jnp.swapaxes(w_in_all, 1, 2), layer)
    w_tail = w_main[:, o_gate:]
    wkr = w_tail[:, o_kr - o_gate:o_kr - o_gate + ROPE]
    gate_pad = jnp.zeros((d, GATE_W - 2 * M_HEADS), w_tail.dtype)
    w_b = jnp.concatenate([w_tail[:, o_cq - o_gate:o_kr - o_gate], w_tail[:, :o_cq - o_gate], gate_pad, wkr,
                           _swap_half(wkr)], axis=1)

    uq = w_uq.reshape(Q_LORA, A_HEADS, NOPE + ROPE)
    uq_r = jnp.concatenate([uq[..., :NOPE], uq[..., NOPE:], _swap_half(uq[..., NOPE:])], axis=-1)
    uq_r = uq_r.reshape(Q_LORA, A_HEADS * QK_W)

    w_gate, w_up = _cast_gate_up(w_gu_all, layer, ffp)
    return dict(w_in=w_main, w_b=w_b, uq=uq_r, ukv=_cast_rows(w_ukv_all, layer),
                w_out=_cast_rows(w_out_all, layer), w_gate=w_gate, w_up=w_up,
                w_down=_cast_rows(w_down_all, layer, ffp))


def _rope_table(lp):
    meta0 = T0 - N_META
    pos = jnp.maximum(jnp.arange(lp, dtype=jnp.int32) - meta0, 0).astype(F32)
    inv_freq = ROPE_THETA ** (-jnp.arange(ROPE // 2, dtype=F32) / (ROPE // 2))
    ang = pos[:, None] * inv_freq[None, :]
    cos, sin = jnp.cos(ang), jnp.sin(ang)
    return jnp.concatenate([cos, cos, sin, sin], axis=-1)


def kernel(x, meta, g_mix_pre, w_in, conv_w, b_gates, g_mnorm, g_cq, w_uq, g_ckv, w_ukv, w_out,
           g_mix_post, g_ffn_pre, w_gu, w_down, g_ffn_post):
    batch, seq, d = x.shape
    assert batch == 1 and seq % T0 == 0
    depth = w_in.shape[0]
    lp = T0 + seq
    mw = d // 2
    dv = mw // M_HEADS
    dk = dv // 2
    hk = M_HEADS * dk
    aw = d - mw
    av = aw // A_HEADS
    ff = w_down.shape[1]
    assert Q_LORA % KV_LORA == 0 and KV_LORA % ROPE_W == 0 and ROPE_W == GATE_W and 2 * hk == mw

    bf = 512
    ffp = -(-ff // 1024) * 1024
    bm = _div_tile(lp, 1664, 128)
    bn_in = _div_tile(mw, 512, 128)
    scale = (NOPE + ROPE) ** -0.5 * LOG2_E

    ckv_blk = Q_LORA // KV_LORA
    gate_blk = (Q_LORA + KV_LORA) // GATE_W
    kr_blk = (Q_LORA + KV_LORA + GATE_W) // ROPE_W
    nb = Q_LORA + KV_LORA + GATE_W + ROPE_W

    tab = _rope_table(lp)
    h, u = _prep(x[0], meta, g_mix_pre[0][None])

    out = None
    for l in range(depth):
        w = _prep_layer_weights(l, w_in, w_uq, w_ukv, w_out, w_gu, w_down, mw=mw, hk=hk, ffp=ffp)
        a_qk = _matmul([u], [(w["w_in"], 0, 0)], mw, F32, bm=bm, bn=bn_in, name="inproj_qk")
        v_m = _matmul([u], [(w["w_in"], 0, mw // bn_in)], mw, BF16, bm=bm, bn=bn_in, name="inproj_v")
        a_o = _matmul([u], [(w["w_in"], 0, 2 * mw // bn_in)], mw, F32, bm=bm, bn=bn_in, name="inproj_o")
        b_misc = _matmul([u], [(w["w_b"], 0, 0)], nb, F32, bm=bm, bn=_div_tile(nb, 768, 128),
                         name="inproj_misc")

        gates_t = b_misc[:, Q_LORA + KV_LORA:Q_LORA + KV_LORA + 2 * M_HEADS].T
        bias_c = jnp.pad(b_gates[l], (0, GATE_W - 2 * M_HEADS))[None, :]
        bias_r = b_gates[l][:, None]
        h_m = _mlstm(a_qk, v_m, a_o, b_misc, gates_t, conv_w[l], bias_c, bias_r, g_mnorm[l][None],
                     heads=M_HEADS, dk=dk, dv=dv, gate_blk=gate_blk)

        q_a = _qproj(b_misc, g_cq[l][None], w["uq"], tab, bm=_div_tile(lp, 832, 64), hpb=4, scale=scale)
        k_a, vt_a = _kvproj(b_misc, g_ckv[l][None], w["ukv"], tab, hpb=4, av=av,
                            ckv_blk=ckv_blk, kr_blk=kr_blk)
        h_a = _attention(q_a, k_a, vt_a, heads=A_HEADS, av=av)

        mix = _matmul([h_m, h_a], [(w["w_out"], 0, 0), (w["w_out"], 1, 0)], d, F32, bm=bm,
                      bn=_div_tile(d, 512, 128), name="outproj")
        h, u = _resnorm(h, mix, g_mix_post[l][None], g_ffn_pre[l][None])

        act = _matmul([u], [(w["w_gate"], 0, 0), (w["w_up"], 0, 0)], ffp, BF16,
                      bm=bm, bn=bf, swiglu=True, name="ffn_up")
        y = _matmul([act], [(w["w_down"], 0, 0)], d, F32, bm=bm, bn=_div_tile(d, 1024, 128),
                    bk=_div_tile(ffp, 2816, 128), name="ffn_down")
        if l + 1 < depth:
            h, u = _resnorm(h, y, g_ffn_post[l][None], g_mix_pre[l + 1][None])
        else:
            out = _resnorm_final(h, y, g_ffn_post[l][None])
    return out[None]
```

```python
import functools

import jax
import jax.numpy as jnp
from jax import lax
from jax.experimental import pallas as pl
from jax.experimental.pallas import tpu as pltpu

N_META = 16
M_HEADS = 4
CONV_W = 4
A_HEADS = 16
NOPE = 128
ROPE = 64
Q_LORA = 1536
KV_LORA = 512
ROPE_THETA = 10000.0
NORM_EPS = 1e-6
NEG_SCORE = -1e30
LOG2_E = 1.4426950408889634

T0 = 512
MLSTM_CHUNK = 256
ROW_BLOCK = 256
HALO = 8
QK_W = NOPE + 2 * ROPE
GATE_W = 128
ROPE_W = 2 * ROPE

VMEM_LIMIT_BYTES = 56 * 1024 * 1024
MM_SUB_ROWS = 512
CAST_ROWS = 128
ONES_ROWS = 16
ATT_SLAB = 512
ATT_CHUNK = 768
ATT_TILE = 2 * ATT_CHUNK

F32 = jnp.float32
BF16 = jnp.bfloat16


def _div_tile(n, target, mult):
    best = None
    t = mult
    while t <= min(n, target):
        if n % t == 0:
            best = t
        t += mult
    if best is None:
        raise ValueError(f"no tile for {n} (target {target}, multiple of {mult})")
    return best


def _params(sem):
    return pltpu.CompilerParams(dimension_semantics=sem, vmem_limit_bytes=VMEM_LIMIT_BYTES)


def _rms(t, g):
    return t * lax.rsqrt(jnp.mean(t * t, axis=-1, keepdims=True) + NORM_EPS) * g


def _row_loop(nrows, sub, fn):
    def step(r, carry):
        fn(pl.ds(pl.multiple_of(r * sub, sub), sub))
        return carry
    lax.fori_loop(0, nrows // sub, step, 0, unroll=True)


def _prep_body(x_ref, meta_ref, g_ref, h_ref, u_ref):
    i = pl.program_id(0)

    @pl.when(i == 0)
    def _():
        h_ref[...] = jnp.zeros_like(h_ref)
        h_ref[T0 - N_META:T0, :] = meta_ref[...]

    @pl.when(i > 0)
    def _():
        h_ref[...] = x_ref[...]

    u_ref[...] = _rms(h_ref[...], g_ref[...]).astype(BF16)


def _prep(x2d, meta, g):
    seq, d = x2d.shape
    lp = T0 + seq
    return pl.pallas_call(
        _prep_body,
        grid=(lp // T0,),
        in_specs=[
            pl.BlockSpec((T0, d), lambda i: (jnp.maximum(i - 1, 0), 0)),
            pl.BlockSpec((N_META, d), lambda i: (0, 0)),
            pl.BlockSpec((1, d), lambda i: (0, 0)),
        ],
        out_specs=[
            pl.BlockSpec((T0, d), lambda i: (i, 0)),
            pl.BlockSpec((T0, d), lambda i: (i, 0)),
        ],
        out_shape=[jax.ShapeDtypeStruct((lp, d), F32), jax.ShapeDtypeStruct((lp, d), BF16)],
        compiler_params=_params(("parallel",)),
        name="prep_norm",
    )(x2d, meta, g)


def _mm_body(*refs, n_lhs, nk, swiglu, bm, sub):
    lhs = refs[0:n_lhs]
    rhs = refs[n_lhs:-1]
    o_ref = refs[-1]

    def product(rows, b):
        acc = None
        for a, w in zip(lhs, b):
            d = jnp.dot(a[rows, :], w[...], preferred_element_type=F32)
            acc = d if acc is None else acc + d
        return acc

    def finish(rows):
        if swiglu:
            gate = product(rows, rhs[0:1])
            acc = gate * jax.nn.sigmoid(gate) * product(rows, rhs[1:2])
        else:
            acc = product(rows, rhs)
        o_ref[rows, :] = acc.astype(o_ref.dtype)

    if nk == 1:
        _row_loop(bm, sub, finish)
        return

    k = pl.program_id(2)

    def first(rows):
        o_ref[rows, :] = product(rows, rhs)

    def accumulate(rows):
        o_ref[rows, :] += product(rows, rhs)

    @pl.when(k == 0)
    def _():
        _row_loop(bm, sub, first)

    @pl.when(k > 0)
    def _():
        _row_loop(bm, sub, accumulate)


def _matmul(lhs, rhs, n, out_dtype, *, bm, bn, bk=None, swiglu=False, name):
    m = lhs[0].shape[0]
    if bk is None:
        nk = 1
    else:
        assert len(lhs) == 1 and not swiglu and out_dtype == F32
        nk = lhs[0].shape[1] // bk
    in_specs = []
    for a in lhs:
        kk = a.shape[1] if bk is None else bk
        in_specs.append(pl.BlockSpec((bm, kk), lambda i, j, k: (i, k)))
    for p, (w, rb, cb) in enumerate(rhs):
        a = lhs[0] if swiglu else lhs[p]
        kk = a.shape[1] if bk is None else bk
        in_specs.append(pl.BlockSpec((kk, bn), lambda i, j, k, rb=rb, cb=cb: (k + rb, j + cb)))
    return pl.pallas_call(
        functools.partial(_mm_body, n_lhs=len(lhs), nk=nk, swiglu=swiglu, bm=bm,
                          sub=_div_tile(bm, MM_SUB_ROWS, 16)),
        grid=(m // bm, n // bn, nk),
        in_specs=in_specs,
        out_specs=pl.BlockSpec((bm, bn), lambda i, j, k: (i, j)),
        out_shape=jax.ShapeDtypeStruct((m, n), out_dtype),
        compiler_params=_params(("parallel", "parallel", "arbitrary")),
        name=name,
    )(*lhs, *[w for w, _, _ in rhs])


def _resnorm_body(x_ref, y_ref, gp_ref, gn_ref, xo_ref, u_ref, *, rows):
    i = pl.program_id(0)
    row = i * rows + lax.broadcasted_iota(jnp.int32, (rows, 1), 0)
    xn = x_ref[...] + _rms(y_ref[...], gp_ref[...])
    xn = jnp.where(row >= T0 - N_META, xn, 0.0)
    xo_ref[...] = xn
    u_ref[...] = _rms(xn, gn_ref[...]).astype(BF16)


def _resnorm(x, y, g_post, g_next):
    lp, d = x.shape
    rows = ROW_BLOCK
    blk = pl.BlockSpec((rows, d), lambda i: (i, 0))
    vec = pl.BlockSpec((1, d), lambda i: (0, 0))
    return pl.pallas_call(
        functools.partial(_resnorm_body, rows=rows),
        grid=(lp // rows,),
        in_specs=[blk, blk, vec, vec],
        out_specs=[blk, blk],
        out_shape=[jax.ShapeDtypeStruct((lp, d), F32), jax.ShapeDtypeStruct((lp, d), BF16)],
        compiler_params=_params(("parallel",)),
        name="resnorm",
    )(x, y, g_post, g_next)


def _resnorm_final_body(x_ref, y_ref, gp_ref, o_ref):
    o_ref[...] = x_ref[...] + _rms(y_ref[...], gp_ref[...])


def _resnorm_final(x, y, g_post):
    lp, d = x.shape
    rows = ROW_BLOCK
    skip = T0 // rows
    blk_in = pl.BlockSpec((rows, d), lambda i: (i + skip, 0))
    return pl.pallas_call(
        _resnorm_final_body,
        grid=((lp - T0) // rows,),
        in_specs=[blk_in, blk_in, pl.BlockSpec((1, d), lambda i: (0, 0))],
        out_specs=pl.BlockSpec((rows, d), lambda i: (i, 0)),
        out_shape=jax.ShapeDtypeStruct((lp - T0, d), F32),
        compiler_params=_params(("parallel",)),
        name="resnorm_final",
    )(x, y, g_post)


def _log_sigmoid(x):
    return jnp.minimum(x, 0.0) - jnp.log1p(jnp.exp(-jnp.abs(x)))


def _mlstm_body(q_ref, k_ref, v_ref, o_ref, gc_ref, gr_ref, cw_ref, bc_ref, br_ref, gn_ref,
                out_ref, xext, c_sc, n_sc, m_sc, *, heads, dk, dv, lc):
    c = pl.program_id(0)
    hk = heads * dk
    meta0 = T0 - N_META

    @pl.when(c == 0)
    def _():
        xext[0:HALO, :] = jnp.zeros((HALO, 2 * hk), F32)
        c_sc[...] = jnp.zeros_like(c_sc)
        n_sc[...] = jnp.zeros_like(n_sc)
        m_sc[...] = jnp.zeros_like(m_sc)

    xext[HALO:HALO + lc, 0:hk] = q_ref[...]
    xext[HALO:HALO + lc, hk:2 * hk] = k_ref[...]
    conv = None
    for j in range(CONV_W):
        term = cw_ref[j:j + 1, :] * xext[pl.ds(HALO - (CONV_W - 1) + j, lc), :]
        conv = term if conv is None else conv + term
    xext[0:HALO, :] = xext[lc:lc + HALO, :]
    qk = conv * jax.nn.sigmoid(conv)

    row = c * lc + lax.broadcasted_iota(jnp.int32, (lc, 1), 0)
    col = c * lc + lax.broadcasted_iota(jnp.int32, (1, lc), 1)
    valid_c = row >= meta0
    valid_r = col >= meta0
    gc = gc_ref[...] + bc_ref[...]
    gr = gr_ref[...] + br_ref[...]
    tt = lax.broadcasted_iota(jnp.int32, (lc, lc), 0)
    ss = lax.broadcasted_iota(jnp.int32, (lc, lc), 1)
    tril = ss <= tt
    triu = tt <= ss
    neg_inf = -jnp.inf

    for h in range(heads):
        li_c = jnp.where(valid_c, gc[:, h:h + 1], neg_inf)
        lf_c = jnp.where(valid_c, _log_sigmoid(gc[:, heads + h:heads + h + 1]), 0.0)
        li_r = jnp.where(valid_r, gr[h:h + 1, :], neg_inf)
        lf_r = jnp.where(valid_r, _log_sigmoid(gr[heads + h:heads + h + 1, :]), 0.0)
        b_c = jnp.sum(jnp.where(tril, lf_r, 0.0), axis=1, keepdims=True)
        b_r = jnp.sum(jnp.where(triu, lf_c, 0.0), axis=0, keepdims=True)
        g = jnp.sum(lf_r, axis=1, keepdims=True)
        m = m_sc[h:h + 1, 0:1]

        d = jnp.where(tril, b_c - b_r + li_r, neg_inf)
        inter = b_c + m
        m_t = jnp.maximum(inter, jnp.max(d, axis=1, keepdims=True))
        w_inter = jnp.exp(inter - m_t)
        p = jnp.exp(d - m_t)

        qh = qk[:, h * dk:(h + 1) * dk] * (dk ** -0.5)
        kh = qk[:, hk + h * dk:hk + (h + 1) * dk]
        qb = qh.astype(BF16)
        s = lax.dot_general(qb, kh.astype(BF16), (((1,), (1,)), ((), ())),
                            preferred_element_type=F32) * p
        vh = v_ref[:, h * dv:(h + 1) * dv]
        ch = c_sc[h]
        nh = n_sc[h:h + 1, :]
        num = (w_inter * jnp.dot(qb, ch.astype(BF16), preferred_element_type=F32)
               + jnp.dot(s.astype(BF16), vh, preferred_element_type=F32))
        den = (w_inter * jnp.sum(qh * nh, axis=1, keepdims=True)
               + jnp.sum(s, axis=1, keepdims=True))
        hh = num / jnp.maximum(jnp.abs(den), jnp.exp(-m_t))

        a_c = g - b_c + li_c
        m_new = jnp.maximum(g + m, jnp.max(a_c, axis=0, keepdims=True))
        decay = jnp.exp(g + m - m_new)
        wk = kh * jnp.exp(a_c - m_new)
        c_sc[h] = decay * ch + lax.dot_general(wk.astype(BF16), vh, (((0,), (0,)), ((), ())),
                                               preferred_element_type=F32)
        n_sc[h:h + 1, :] = decay * nh + jnp.sum(wk, axis=0, keepdims=True)
        m_sc[h:h + 1, :] = jnp.broadcast_to(m_new, (1, m_sc.shape[1]))

        hn = _rms(hh, gn_ref[:, h * dv:(h + 1) * dv])
        og = jax.nn.sigmoid(o_ref[:, h * dv:(h + 1) * dv])
        out_ref[:, h * dv:(h + 1) * dv] = (og * hn).astype(BF16)


def _mlstm(a_qk, v, a_o, b_misc, gates_t, conv_w, bias_c, bias_r, g_mnorm, *, heads, dk, dv, gate_blk):
    lp = v.shape[0]
    lc = MLSTM_CHUNK
    hk = heads * dk
    mw = heads * dv
    assert 2 * heads <= 8
    return pl.pallas_call(
        functools.partial(_mlstm_body, heads=heads, dk=dk, dv=dv, lc=lc),
        grid=(lp // lc,),
        in_specs=[
            pl.BlockSpec((lc, hk), lambda c: (c, 0)),
            pl.BlockSpec((lc, hk), lambda c: (c, 1)),
            pl.BlockSpec((lc, mw), lambda c: (c, 0)),
            pl.BlockSpec((lc, mw), lambda c: (c, 0)),
            pl.BlockSpec((lc, GATE_W), lambda c: (c, gate_blk)),
            pl.BlockSpec((2 * heads, lc), lambda c: (0, c)),
            pl.BlockSpec((CONV_W, 2 * hk), lambda c: (0, 0)),
            pl.BlockSpec((1, GATE_W), lambda c: (0, 0)),
            pl.BlockSpec((2 * heads, 1), lambda c: (0, 0)),
            pl.BlockSpec((1, mw), lambda c: (0, 0)),
        ],
        out_specs=pl.BlockSpec((lc, mw), lambda c: (c, 0)),
        out_shape=jax.ShapeDtypeStruct((lp, mw), BF16),
        scratch_shapes=[
            pltpu.VMEM((lc + HALO, 2 * hk), F32),
            pltpu.VMEM((heads, dk, dv), F32),
            pltpu.VMEM((8, dk), F32),
            pltpu.VMEM((8, 128), F32),
        ],
        compiler_params=_params(("arbitrary",)),
        name="mlstm",
    )(a_qk, a_qk, v, a_o, b_misc, gates_t, conv_w, bias_c, bias_r, g_mnorm)


def _rope128(t, tab):
    pr = t * tab
    rr = pr + pltpu.roll(pr, ROPE, axis=1)
    lane = lax.broadcasted_iota(jnp.int32, pr.shape, 1)
    return jnp.where(lane < ROPE, rr, 0.0)


def _qproj_body(c_ref, g_ref, w_ref, tab_ref, o_ref, cn, *, hpb, scale, bm, sub):
    @pl.when(pl.program_id(1) == 0)
    def _():
        def norm(rows):
            cn[rows, :] = _rms(c_ref[rows, :], g_ref[...]).astype(BF16)
        _row_loop(bm, sub, norm)

    def project(rows):
        acc = jnp.dot(cn[rows, :], w_ref[...], preferred_element_type=F32)
        tab = tab_ref[rows, :]
        for hh in range(hpb):
            base = hh * QK_W
            o_ref[rows, base:base + NOPE] = (acc[:, base:base + NOPE] * scale).astype(BF16)
            rr = _rope128(acc[:, base + NOPE:base + QK_W], tab)
            o_ref[rows, base + NOPE:base + QK_W] = (rr * scale).astype(BF16)
    _row_loop(bm, sub, project)


def _kvproj_body(c_ref, g_ref, w_ref, kr_ref, tab_ref, k_ref, vt_ref, *, hpb, av):
    cn = _rms(c_ref[...], g_ref[...]).astype(BF16)
    acc = jnp.dot(cn, w_ref[...], preferred_element_type=F32)
    kr = _rope128(kr_ref[...], tab_ref[...]).astype(BF16)
    for hh in range(hpb):
        src = hh * (NOPE + av)
        k_ref[:, hh * QK_W:hh * QK_W + NOPE] = acc[:, src:src + NOPE].astype(BF16)
        k_ref[:, hh * QK_W + NOPE:(hh + 1) * QK_W] = kr
        vt_ref[hh, 0, 0:av, :] = acc[:, src + NOPE:src + NOPE + av].T.astype(BF16)
        vt_ref[hh, 0, av:av + ONES_ROWS, :] = jnp.ones((ONES_ROWS, vt_ref.shape[3]), BF16)


def _qproj(b_misc, g_cq, w_uq_r, tab, *, bm, hpb, scale):
    lp = b_misc.shape[0]
    cw = w_uq_r.shape[0]
    n = w_uq_r.shape[1]
    bn = hpb * QK_W
    return pl.pallas_call(
        functools.partial(_qproj_body, hpb=hpb, scale=scale, bm=bm, sub=_div_tile(bm, MM_SUB_ROWS, 16)),
        grid=(lp // bm, n // bn),
        in_specs=[
            pl.BlockSpec((bm, cw), lambda i, j: (i, 0)),
            pl.BlockSpec((1, cw), lambda i, j: (0, 0)),
            pl.BlockSpec((cw, bn), lambda i, j: (0, j)),
            pl.BlockSpec((bm, ROPE_W), lambda i, j: (i, 0)),
        ],
        out_specs=pl.BlockSpec((bm, bn), lambda i, j: (i, j)),
        out_shape=jax.ShapeDtypeStruct((lp, n), BF16),
        scratch_shapes=[pltpu.VMEM((bm, cw), BF16)],
        compiler_params=_params(("parallel", "arbitrary")),
        name="mla_qproj",
    )(b_misc, g_cq, w_uq_r, tab)


def _kvproj(b_misc, g_ckv, w_ukv, tab, *, hpb, av, ckv_blk, kr_blk):
    lp = b_misc.shape[0]
    cw = w_ukv.shape[0]
    n = w_ukv.shape[1]
    heads = n // (NOPE + av)
    bn = hpb * (NOPE + av)
    bm = ATT_CHUNK
    return pl.pallas_call(
        functools.partial(_kvproj_body, hpb=hpb, av=av),
        grid=(lp // bm, n // bn),
        in_specs=[
            pl.BlockSpec((bm, cw), lambda i, j: (i, ckv_blk)),
            pl.BlockSpec((1, cw), lambda i, j: (0, 0)),
            pl.BlockSpec((cw, bn), lambda i, j: (0, j)),
            pl.BlockSpec((bm, ROPE_W), lambda i, j: (i, kr_blk)),
            pl.BlockSpec((bm, ROPE_W), lambda i, j: (i, 0)),
        ],
        out_specs=[
            pl.BlockSpec((bm, hpb * QK_W), lambda i, j: (i, j)),
            pl.BlockSpec((hpb, 1, av + ONES_ROWS, bm), lambda i, j: (j, i, 0, 0)),
        ],
        out_shape=[jax.ShapeDtypeStruct((lp, heads * QK_W), BF16),
                   jax.ShapeDtypeStruct((heads, lp // bm, av + ONES_ROWS, bm), BF16)],
        compiler_params=_params(("parallel", "parallel")),
        name="mla_kvproj",
    )(b_misc, g_ckv, w_ukv, b_misc, tab)


def _attn_body(q_ref, k_ref, vt_ref, o_ref, m_sc, acc_sc, s_even, s_odd, c_even, c_odd):
    i = pl.program_id(1)
    n_slab = ATT_TILE // ATT_SLAB
    meta0 = T0 - N_META
    all_slabs = tuple(range(n_slab))
    late_slabs = tuple(s for s in all_slabs if (s + 1) * ATT_SLAB > ATT_CHUNK)

    av = o_ref.shape[1]
    m_sc[...] = jnp.full_like(m_sc, NEG_SCORE)
    acc_sc[...] = jnp.zeros_like(acc_sc)

    def scores(j, buf, masked):
        s_buf, c_buf = buf
        start = j * ATT_CHUNK
        if not isinstance(j, int):
            start = pl.multiple_of(start, ATT_CHUNK)
        k = k_ref[pl.ds(start, ATT_CHUNK), :]
        if masked:
            kpos = j * ATT_CHUNK + lax.broadcasted_iota(jnp.int32, (ATT_CHUNK, ATT_SLAB), 0)
            qlane = lax.broadcasted_iota(jnp.int32, (ATT_CHUNK, ATT_SLAB), 1)
        for s in all_slabs:
            q = q_ref[s * ATT_SLAB:(s + 1) * ATT_SLAB, :]
            st = lax.dot_general(k, q, (((1,), (1,)), ((), ())), preferred_element_type=F32)
            if masked:
                qpos = i * ATT_TILE + s * ATT_SLAB + qlane
                keep = jnp.logical_and(kpos <= qpos, kpos >= meta0)
                st = jnp.where(keep, st, NEG_SCORE)
            s_buf[s] = st
            c_buf[s] = jnp.max(st, axis=0, keepdims=True)

    def absorb(j, buf, slabs):
        s_buf, c_buf = buf
        vt = vt_ref[0, j]
        for s in slabs:
            m_prev = m_sc[s]
            m_new = jnp.maximum(m_prev, c_buf[s])
            alpha = jnp.exp2(m_prev - m_new)
            p = jnp.exp2(s_buf[s] - m_new)
            acc_sc[s] = alpha * acc_sc[s] + jnp.dot(vt, p.astype(BF16), preferred_element_type=F32)
            m_sc[s] = m_new

    even = (s_even, c_even)
    odd = (s_odd, c_odd)

    def pair(p, mask_odd, slabs_odd, mask_next):
        scores(2 * p + 1, odd, mask_odd)
        absorb(2 * p, even, all_slabs)
        if mask_next is not None:
            scores(2 * p + 2, even, mask_next)
        absorb(2 * p + 1, odd, slabs_odd)

    scores(0, even, True)

    @pl.when(i == 1)
    def _():
        pair(0, False, all_slabs, True)

    @pl.when(i > 1)
    def _():
        pair(0, False, all_slabs, False)

        def mid(p, carry):
            pair(p, False, all_slabs, False)
            return carry
        lax.fori_loop(1, i - 1, mid, 0)
        pair(i - 1, False, all_slabs, True)

    pair(i, True, late_slabs, None)

    for s in all_slabs:
        out = (acc_sc[s, 0:av, :] / acc_sc[s, av:av + 1, :]).T
        o_ref[s * ATT_SLAB:(s + 1) * ATT_SLAB, :] = out.astype(o_ref.dtype)


def _attention(q, k, vt, *, heads, av):
    lp = q.shape[0]
    n_slab = ATT_TILE // ATT_SLAB
    assert lp % ATT_TILE == 0 and ATT_TILE % ATT_SLAB == 0 and ATT_TILE == 2 * ATT_CHUNK
    return pl.pallas_call(
        _attn_body,
        grid=(heads, lp // ATT_TILE),
        in_specs=[
            pl.BlockSpec((ATT_TILE, QK_W), lambda h, i: (i, h)),
            pl.BlockSpec((lp, QK_W), lambda h, i: (0, h)),
            pl.BlockSpec((1, lp // ATT_CHUNK, av + ONES_ROWS, ATT_CHUNK), lambda h, i: (h, 0, 0, 0)),
        ],
        out_specs=pl.BlockSpec((ATT_TILE, av), lambda h, i: (i, h)),
        out_shape=jax.ShapeDtypeStruct((lp, heads * av), BF16),
        scratch_shapes=[
            pltpu.VMEM((n_slab, 1, ATT_SLAB), F32),
            pltpu.VMEM((n_slab, av + ONES_ROWS, ATT_SLAB), F32),
            pltpu.VMEM((n_slab, ATT_CHUNK, ATT_SLAB), F32),
            pltpu.VMEM((n_slab, ATT_CHUNK, ATT_SLAB), F32),
            pltpu.VMEM((n_slab, 1, ATT_SLAB), F32),
            pltpu.VMEM((n_slab, 1, ATT_SLAB), F32),
        ],
        compiler_params=_params(("parallel", "parallel")),
        name="mla_attention",
    )(q, k, vt)


def _cast_rows_body(x_ref, o_ref, *, n_src):
    @pl.when(pl.program_id(0) < n_src)
    def _():
        o_ref[...] = x_ref[...].astype(BF16)

    @pl.when(pl.program_id(0) >= n_src)
    def _():
        o_ref[...] = jnp.zeros_like(o_ref)


def _cast_rows(w, layer, rows_out=None):
    _, rows, cols = w.shape
    rows_out = rows if rows_out is None else rows_out
    rb = _div_tile(rows, CAST_ROWS, 8)
    assert rows_out % rb == 0
    n_src = rows // rb
    return pl.pallas_call(
        functools.partial(_cast_rows_body, n_src=n_src),
        grid=(rows_out // rb,),
        in_specs=[pl.BlockSpec((None, rb, cols), lambda i: (layer, jnp.minimum(i, n_src - 1), 0))],
        out_specs=pl.BlockSpec((rb, cols), lambda i: (i, 0)),
        out_shape=jax.ShapeDtypeStruct((rows_out, cols), BF16),
        compiler_params=_params(("parallel",)),
        name="cast_bf16",
    )(w)


def _cast_transposed_body(x_ref, o_ref):
    o_ref[...] = x_ref[...].T.astype(BF16)


def _cast_transposed(w_t, layer):
    _, cols, rows = w_t.shape
    cb = 2 * CAST_ROWS
    return pl.pallas_call(
        _cast_transposed_body,
        grid=(pl.cdiv(cols, cb),),
        in_specs=[pl.BlockSpec((None, cb, rows), lambda i: (layer, i, 0))],
        out_specs=pl.BlockSpec((rows, cb), lambda i: (0, i)),
        out_shape=jax.ShapeDtypeStruct((rows, cols), BF16),
        compiler_params=_params(("parallel",)),
        name="cast_transposed",
    )(w_t)


def _cast_gate_up_body(x_ref, g_ref, u_ref, *, ff):
    pad = g_ref.shape[1] - ff
    g_ref[:, :ff] = x_ref[:, :ff].astype(BF16)
    u_ref[:, :ff] = x_ref[:, ff:].astype(BF16)
    if pad:
        g_ref[:, ff:] = jnp.zeros((g_ref.shape[0], pad), BF16)
        u_ref[:, ff:] = jnp.zeros((u_ref.shape[0], pad), BF16)


def _cast_gate_up(w_gu, layer, ffp):
    _, d, ff2 = w_gu.shape
    ff = ff2 // 2
    rb = _div_tile(d, CAST_ROWS // 2, 8)
    out = jax.ShapeDtypeStruct((d, ffp), BF16)
    return pl.pallas_call(
        functools.partial(_cast_gate_up_body, ff=ff),
        grid=(d // rb,),
        in_specs=[pl.BlockSpec((None, rb, ff2), lambda i: (layer, i, 0))],
        out_specs=[pl.BlockSpec((rb, ffp), lambda i: (i, 0))] * 2,
        out_shape=[out, out],
        compiler_params=_params(("parallel",)),
        name="cast_gate_up",
    )(w_gu)


def _swap_half(w):
    half = w.shape[-1] // 2
    return jnp.concatenate([-w[..., half:], w[..., :half]], axis=-1)


def _prep_layer_weights(layer, w_in_all, w_uq_all, w_ukv_all, w_out_all, w_gu_all, w_down_all, *, mw, hk, ffp):
    w_uq = _cast_rows(w_uq_all, layer)
    d = w_in_all.shape[1]
    o_gate = 2 * hk + 2 * mw
    o_cq = o_gate + 2 * M_HEADS
    o_ckv = o_cq + Q_LORA
    o_kr = o_ckv + KV_LORA
    w_main = _cast_transposed(jnp.swapaxes(w_in_all, 1, 2), layer)
    w_tail = w_main[:, o_gate:]
    wkr = w_tail[:, o_kr - o_gate:o_kr - o_gate + ROPE]
    gate_pad = jnp.zeros((d, GATE_W - 2 * M_HEADS), w_tail.dtype)
    w_b = jnp.concatenate([w_tail[:, o_cq - o_gate:o_kr - o_gate], w_tail[:, :o_cq - o_gate], gate_pad, wkr,
                           _swap_half(wkr)], axis=1)

    uq = w_uq.reshape(Q_LORA, A_HEADS, NOPE + ROPE)
    uq_r = jnp.concatenate([uq[..., :NOPE], uq[..., NOPE:], _swap_half(uq[..., NOPE:])], axis=-1)
    uq_r = uq_r.reshape(Q_LORA, A_HEADS * QK_W)

    w_gate, w_up = _cast_gate_up(w_gu_all, layer, ffp)
    return dict(w_in=w_main, w_b=w_b, uq=uq_r, ukv=_cast_rows(w_ukv_all, layer),
                w_out=_cast_rows(w_out_all, layer), w_gate=w_gate, w_up=w_up,
                w_down=_cast_rows(w_down_all, layer, ffp))


def _rope_table(lp):
    meta0 = T0 - N_META
    pos = jnp.maximum(jnp.arange(lp, dtype=jnp.int32) - meta0, 0).astype(F32)
    inv_freq = ROPE_THETA ** (-jnp.arange(ROPE // 2, dtype=F32) / (ROPE // 2))
    ang = pos[:, None] * inv_freq[None, :]
    cos, sin = jnp.cos(ang), jnp.sin(ang)
    return jnp.concatenate([cos, cos, sin, sin], axis=-1)


def kernel(x, meta, g_mix_pre, w_in, conv_w, b_gates, g_mnorm, g_cq, w_uq, g_ckv, w_ukv, w_out,
           g_mix_post, g_ffn_pre, w_gu, w_down, g_ffn_post):
    batch, seq, d = x.shape
    assert batch == 1 and seq % T0 == 0
    depth = w_in.shape[0]
    lp = T0 + seq
    mw = d // 2
    dv = mw // M_HEADS
    dk = dv // 2
    hk = M_HEADS * dk
    aw = d - mw
    av = aw // A_HEADS
    ff = w_down.shape[1]
    assert Q_LORA % KV_LORA == 0 and KV_LORA % ROPE_W == 0 and ROPE_W == GATE_W and 2 * hk == mw

    bf = 512
    ffp = -(-ff // 1024) * 1024
    bm = _div_tile(lp, 1664, 128)
    bn_in = _div_tile(mw, 512, 128)
    scale = (NOPE + ROPE) ** -0.5 * LOG2_E

    ckv_blk = Q_LORA // KV_LORA
    gate_blk = (Q_LORA + KV_LORA) // GATE_W
    kr_blk = (Q_LORA + KV_LORA + GATE_W) // ROPE_W
    nb = Q_LORA + KV_LORA + GATE_W + ROPE_W

    tab = _rope_table(lp)
    h, u = _prep(x[0], meta, g_mix_pre[0][None])

    out = None
    for l in range(depth):
        w = _prep_layer_weights(l, w_in, w_uq, w_ukv, w_out, w_gu, w_down, mw=mw, hk=hk, ffp=ffp)
        a_qk = _matmul([u], [(w["w_in"], 0, 0)], mw, F32, bm=bm, bn=bn_in, name="inproj_qk")
        v_m = _matmul([u], [(w["w_in"], 0, mw // bn_in)], mw, BF16, bm=bm, bn=bn_in, name="inproj_v")
        a_o = _matmul([u], [(w["w_in"], 0, 2 * mw // bn_in)], mw, F32, bm=bm, bn=bn_in, name="inproj_o")
        b_misc = _matmul([u], [(w["w_b"], 0, 0)], nb, F32, bm=bm, bn=_div_tile(nb, 768, 128),
                         name="inproj_misc")

        gates_t = b_misc[:, Q_LORA + KV_LORA:Q_LORA + KV_LORA + 2 * M_HEADS].T
        bias_c = jnp.pad(b_gates[l], (0, GATE_W - 2 * M_HEADS))[None, :]
        bias_r = b_gates[l][:, None]
        h_m = _mlstm(a_qk, v_m, a_o, b_misc, gates_t, conv_w[l], bias_c, bias_r, g_mnorm[l][None],
                     heads=M_HEADS, dk=dk, dv=dv, gate_blk=gate_blk)

        q_a = _qproj(b_misc, g_cq[l][None], w["uq"], tab, bm=_div_tile(lp, 832, 64), hpb=4, scale=scale)
        k_a, vt_a = _kvproj(b_misc, g_ckv[l][None], w["ukv"], tab, hpb=4, av=av,
                            ckv_blk=ckv_blk, kr_blk=kr_blk)
        h_a = _attention(q_a, k_a, vt_a, heads=A_HEADS, av=av)

        mix = _matmul([h_m, h_a], [(w["w_out"], 0, 0), (w["w_out"], 1, 0)], d, F32, bm=bm,
                      bn=_div_tile(d, 512, 128), name="outproj")
        h, u = _resnorm(h, mix, g_mix_post[l][None], g_ffn_pre[l][None])

        act = _matmul([u], [(w["w_gate"], 0, 0), (w["w_up"], 0, 0)], ffp, BF16,
                      bm=bm, bn=bf, swiglu=True, name="ffn_up")
        y = _matmul([act], [(w["w_down"], 0, 0)], d, F32, bm=bm, bn=_div_tile(d, 1024, 128),
                    bk=_div_tile(ffp, 2816, 128), name="ffn_down")
        if l + 1 < depth:
            h, u = _resnorm(h, y, g_ffn_post[l][None], g_mix_pre[l + 1][None])
        else:
            out = _resnorm_final(h, y, g_ffn_post[l][None])
    return out[None]
```

```python
import functools

import jax
import jax.numpy as jnp
from jax import lax
from jax.experimental import pallas as pl
from jax.experimental.pallas import tpu as pltpu

N_META = 16
M_HEADS = 4
CONV_W = 4
A_HEADS = 16
NOPE = 128
ROPE = 64
Q_LORA = 1536
KV_LORA = 512
ROPE_THETA = 10000.0
NORM_EPS = 1e-6
NEG_SCORE = -1e30
LOG2_E = 1.4426950408889634

T0 = 512
MLSTM_CHUNK = 256
ROW_BLOCK = 256
HALO = 8
QK_W = NOPE + 2 * ROPE
GATE_W = 128
ROPE_W = 2 * ROPE

VMEM_LIMIT_BYTES = 56 * 1024 * 1024
MM_SUB_ROWS = 512
CAST_ROWS = 128
ONES_ROWS = 16
ATT_SLAB = 512
ATT_CHUNK = 768
ATT_TILE = 2 * ATT_CHUNK

F32 = jnp.float32
BF16 = jnp.bfloat16


def _div_tile(n, target, mult):
    best = None
    t = mult
    while t <= min(n, target):
        if n % t == 0:
            best = t
        t += mult
    if best is None:
        raise ValueError(f"no tile for {n} (target {target}, multiple of {mult})")
    return best


def _params(sem):
    return pltpu.CompilerParams(dimension_semantics=sem, vmem_limit_bytes=VMEM_LIMIT_BYTES)


def _rms(t, g):
    return t * lax.rsqrt(jnp.mean(t * t, axis=-1, keepdims=True) + NORM_EPS) * g


def _row_loop(nrows, sub, fn):
    def step(r, carry):
        fn(pl.ds(pl.multiple_of(r * sub, sub), sub))
        return carry
    lax.fori_loop(0, nrows // sub, step, 0, unroll=True)


def _prep_body(x_ref, meta_ref, g_ref, h_ref, u_ref):
    i = pl.program_id(0)

    @pl.when(i == 0)
    def _():
        h_ref[...] = jnp.zeros_like(h_ref)
        h_ref[T0 - N_META:T0, :] = meta_ref[...]

    @pl.when(i > 0)
    def _():
        h_ref[...] = x_ref[...]

    u_ref[...] = _rms(h_ref[...], g_ref[...]).astype(BF16)


def _prep(x2d, meta, g):
    seq, d = x2d.shape
    lp = T0 + seq
    return pl.pallas_call(
        _prep_body,
        grid=(lp // T0,),
        in_specs=[
            pl.BlockSpec((T0, d), lambda i: (jnp.maximum(i - 1, 0), 0)),
            pl.BlockSpec((N_META, d), lambda i: (0, 0)),
            pl.BlockSpec((1, d), lambda i: (0, 0)),
        ],
        out_specs=[
            pl.BlockSpec((T0, d), lambda i: (i, 0)),
            pl.BlockSpec((T0, d), lambda i: (i, 0)),
        ],
        out_shape=[jax.ShapeDtypeStruct((lp, d), F32), jax.ShapeDtypeStruct((lp, d), BF16)],
        compiler_params=_params(("parallel",)),
        name="prep_norm",
    )(x2d, meta, g)


def _mm_body(*refs, n_lhs, nk, swiglu, bm, sub):
    lhs = refs[0:n_lhs]
    rhs = refs[n_lhs:-1]
    o_ref = refs[-1]

    def product(rows, b):
        acc = None
        for a, w in zip(lhs, b):
            d = jnp.dot(a[rows, :], w[...], preferred_element_type=F32)
            acc = d if acc is None else acc + d
        return acc

    def finish(rows):
        if swiglu:
            gate = product(rows, rhs[0:1])
            acc = gate * jax.nn.sigmoid(gate) * product(rows, rhs[1:2])
        else:
            acc = product(rows, rhs)
        o_ref[rows, :] = acc.astype(o_ref.dtype)

    if nk == 1:
        _row_loop(bm, sub, finish)
        return

    k = pl.program_id(2)

    def first(rows):
        o_ref[rows, :] = product(rows, rhs)

    def accumulate(rows):
        o_ref[rows, :] += product(rows, rhs)

    @pl.when(k == 0)
    def _():
        _row_loop(bm, sub, first)

    @pl.when(k > 0)
    def _():
        _row_loop(bm, sub, accumulate)


def _matmul(lhs, rhs, n, out_dtype, *, bm, bn, bk=None, swiglu=False, name):
    m = lhs[0].shape[0]
    if bk is None:
        nk = 1
    else:
        assert len(lhs) == 1 and not swiglu and out_dtype == F32
        nk = lhs[0].shape[1] // bk
    in_specs = []
    for a in lhs:
        kk = a.shape[1] if bk is None else bk
        in_specs.append(pl.BlockSpec((bm, kk), lambda i, j, k: (i, k)))
    for p, (w, rb, cb) in enumerate(rhs):
        a = lhs[0] if swiglu else lhs[p]
        kk = a.shape[1] if bk is None else bk
        in_specs.append(pl.BlockSpec((kk, bn), lambda i, j, k, rb=rb, cb=cb: (k + rb, j + cb)))
    return pl.pallas_call(
        functools.partial(_mm_body, n_lhs=len(lhs), nk=nk, swiglu=swiglu, bm=bm,
                          sub=_div_tile(bm, MM_SUB_ROWS, 16)),
        grid=(m // bm, n // bn, nk),
        in_specs=in_specs,
        out_specs=pl.BlockSpec((bm, bn), lambda i, j, k: (i, j)),
        out_shape=jax.ShapeDtypeStruct((m, n), out_dtype),
        compiler_params=_params(("parallel", "parallel", "arbitrary")),
        name=name,
    )(*lhs, *[w for w, _, _ in rhs])


def _resnorm_body(x_ref, y_ref, gp_ref, gn_ref, xo_ref, u_ref, *, rows):
    i = pl.program_id(0)
    row = i * rows + lax.broadcasted_iota(jnp.int32, (rows, 1), 0)
    xn = x_ref[...] + _rms(y_ref[...], gp_ref[...])
    xn = jnp.where(row >= T0 - N_META, xn, 0.0)
    xo_ref[...] = xn
    u_ref[...] = _rms(xn, gn_ref[...]).astype(BF16)


def _resnorm(x, y, g_post, g_next):
    lp, d = x.shape
    rows = ROW_BLOCK
    blk = pl.BlockSpec((rows, d), lambda i: (i, 0))
    vec = pl.BlockSpec((1, d), lambda i: (0, 0))
    return pl.pallas_call(
        functools.partial(_resnorm_body, rows=rows),
        grid=(lp // rows,),
        in_specs=[blk, blk, vec, vec],
        out_specs=[blk, blk],
        out_shape=[jax.ShapeDtypeStruct((lp, d), F32), jax.ShapeDtypeStruct((lp, d), BF16)],
        compiler_params=_params(("parallel",)),
        name="resnorm",
    )(x, y, g_post, g_next)


def _resnorm_final_body(x_ref, y_ref, gp_ref, o_ref):
    o_ref[...] = x_ref[...] + _rms(y_ref[...], gp_ref[...])


def _resnorm_final(x, y, g_post):
    lp, d = x.shape
    rows = ROW_BLOCK
    skip = T0 // rows
    blk_in = pl.BlockSpec((rows, d), lambda i: (i + skip, 0))
    return pl.pallas_call(
        _resnorm_final_body,
        grid=((lp - T0) // rows,),
        in_specs=[blk_in, blk_in, pl.BlockSpec((1, d), lambda i: (0, 0))],
        out_specs=pl.BlockSpec((rows, d), lambda i: (i, 0)),
        out_shape=jax.ShapeDtypeStruct((lp - T0, d), F32),
        compiler_params=_params(("parallel",)),
        name="resnorm_final",
    )(x, y, g_post)


def _log_sigmoid(x):
    return jnp.minimum(x, 0.0) - jnp.log1p(jnp.exp(-jnp.abs(x)))


def _mlstm_body(q_ref, k_ref, v_ref, o_ref, gc_ref, gr_ref, cw_ref, bc_ref, br_ref, gn_ref,
                out_ref, xext, c_sc, n_sc, m_sc, *, heads, dk, dv, lc):
    c = pl.program_id(0)
    hk = heads * dk
    meta0 = T0 - N_META

    @pl.when(c == 0)
    def _():
        xext[0:HALO, :] = jnp.zeros((HALO, 2 * hk), F32)
        c_sc[...] = jnp.zeros_like(c_sc)
        n_sc[...] = jnp.zeros_like(n_sc)
        m_sc[...] = jnp.zeros_like(m_sc)

    xext[HALO:HALO + lc, 0:hk] = q_ref[...]
    xext[HALO:HALO + lc, hk:2 * hk] = k_ref[...]
    conv = None
    for j in range(CONV_W):
        term = cw_ref[j:j + 1, :] * xext[pl.ds(HALO - (CONV_W - 1) + j, lc), :]
        conv = term if conv is None else conv + term
    xext[0:HALO, :] = xext[lc:lc + HALO, :]
    qk = conv * jax.nn.sigmoid(conv)

    row = c * lc + lax.broadcasted_iota(jnp.int32, (lc, 1), 0)
    col = c * lc + lax.broadcasted_iota(jnp.int32, (1, lc), 1)
    valid_c = row >= meta0
    valid_r = col >= meta0
    gc = gc_ref[...] + bc_ref[...]
    gr = gr_ref[...] + br_ref[...]
    tt = lax.broadcasted_iota(jnp.int32, (lc, lc), 0)
    ss = lax.broadcasted_iota(jnp.int32, (lc, lc), 1)
    tril = ss <= tt
    triu = tt <= ss
    neg_inf = -jnp.inf

    for h in range(heads):
        li_c = jnp.where(valid_c, gc[:, h:h + 1], neg_inf)
        lf_c = jnp.where(valid_c, _log_sigmoid(gc[:, heads + h:heads + h + 1]), 0.0)
        li_r = jnp.where(valid_r, gr[h:h + 1, :], neg_inf)
        lf_r = jnp.where(valid_r, _log_sigmoid(gr[heads + h:heads + h + 1, :]), 0.0)
        b_c = jnp.sum(jnp.where(tril, lf_r, 0.0), axis=1, keepdims=True)
        b_r = jnp.sum(jnp.where(triu, lf_c, 0.0), axis=0, keepdims=True)
        g = jnp.sum(lf_r, axis=1, keepdims=True)
        m = m_sc[h:h + 1, 0:1]

        d = jnp.where(tril, b_c - b_r + li_r, neg_inf)
        inter = b_c + m
        m_t = jnp.maximum(inter, jnp.max(d, axis=1, keepdims=True))
        w_inter = jnp.exp(inter - m_t)
        p = jnp.exp(d - m_t)

        qh = qk[:, h * dk:(h + 1) * dk] * (dk ** -0.5)
        kh = qk[:, hk + h * dk:hk + (h + 1) * dk]
        qb = qh.astype(BF16)
        s = lax.dot_general(qb, kh.astype(BF16), (((1,), (1,)), ((), ())),
                            preferred_element_type=F32) * p
        vh = v_ref[:, h * dv:(h + 1) * dv]
        ch = c_sc[h]
        nh = n_sc[h:h + 1, :]
        num = (w_inter * jnp.dot(qb, ch.astype(BF16), preferred_element_type=F32)
               + jnp.dot(s.astype(BF16), vh, preferred_element_type=F32))
        den = (w_inter * jnp.sum(qh * nh, axis=1, keepdims=True)
               + jnp.sum(s, axis=1, keepdims=True))
        hh = num / jnp.maximum(jnp.abs(den), jnp.exp(-m_t))

        a_c = g - b_c + li_c
        m_new = jnp.maximum(g + m, jnp.max(a_c, axis=0, keepdims=True))
        decay = jnp.exp(g + m - m_new)
        wk = kh * jnp.exp(a_c - m_new)
        c_sc[h] = decay * ch + lax.dot_general(wk.astype(BF16), vh, (((0,), (0,)), ((), ())),
                                               preferred_element_type=F32)
        n_sc[h:h + 1, :] = decay * nh + jnp.sum(wk, axis=0, keepdims=True)
        m_sc[h:h + 1, :] = jnp.broadcast_to(m_new, (1, m_sc.shape[1]))

        hn = _rms(hh, gn_ref[:, h * dv:(h + 1) * dv])
        og = jax.nn.sigmoid(o_ref[:, h * dv:(h + 1) * dv])
        out_ref[:, h * dv:(h + 1) * dv] = (og * hn).astype(BF16)


def _mlstm(a_qk, v, a_o, b_misc, gates_t, conv_w, bias_c, bias_r, g_mnorm, *, heads, dk, dv, gate_blk):
    lp = v.shape[0]
    lc = MLSTM_CHUNK
    hk = heads * dk
    mw = heads * dv
    assert 2 * heads <= 8
    return pl.pallas_call(
        functools.partial(_mlstm_body, heads=heads, dk=dk, dv=dv, lc=lc),
        grid=(lp // lc,),
        in_specs=[
            pl.BlockSpec((lc, hk), lambda c: (c, 0)),
            pl.BlockSpec((lc, hk), lambda c: (c, 1)),
            pl.BlockSpec((lc, mw), lambda c: (c, 0)),
            pl.BlockSpec((lc, mw), lambda c: (c, 0)),
            pl.BlockSpec((lc, GATE_W), lambda c: (c, gate_blk)),
            pl.BlockSpec((2 * heads, lc), lambda c: (0, c)),
            pl.BlockSpec((CONV_W, 2 * hk), lambda c: (0, 0)),
            pl.BlockSpec((1, GATE_W), lambda c: (0, 0)),
            pl.BlockSpec((2 * heads, 1), lambda c: (0, 0)),
            pl.BlockSpec((1, mw), lambda c: (0, 0)),
        ],
        out_specs=pl.BlockSpec((lc, mw), lambda c: (c, 0)),
        out_shape=jax.ShapeDtypeStruct((lp, mw), BF16),
        scratch_shapes=[
            pltpu.VMEM((lc + HALO, 2 * hk), F32),
            pltpu.VMEM((heads, dk, dv), F32),
            pltpu.VMEM((8, dk), F32),
            pltpu.VMEM((8, 128), F32),
        ],
        compiler_params=_params(("arbitrary",)),
        name="mlstm",
    )(a_qk, a_qk, v, a_o, b_misc, gates_t, conv_w, bias_c, bias_r, g_mnorm)


def _rope128(t, tab):
    pr = t * tab
    rr = pr + pltpu.roll(pr, ROPE, axis=1)
    lane = lax.broadcasted_iota(jnp.int32, pr.shape, 1)
    return jnp.where(lane < ROPE, rr, 0.0)


def _qproj_body(c_ref, g_ref, w_ref, tab_ref, o_ref, cn, *, hpb, scale, bm, sub):
    @pl.when(pl.program_id(1) == 0)
    def _():
        def norm(rows):
            cn[rows, :] = _rms(c_ref[rows, :], g_ref[...]).astype(BF16)
        _row_loop(bm, sub, norm)

    def project(rows):
        acc = jnp.dot(cn[rows, :], w_ref[...], preferred_element_type=F32)
        tab = tab_ref[rows, :]
        for hh in range(hpb):
            base = hh * QK_W
            o_ref[rows, base:base + NOPE] = (acc[:, base:base + NOPE] * scale).astype(BF16)
            rr = _rope128(acc[:, base + NOPE:base + QK_W], tab)
            o_ref[rows, base + NOPE:base + QK_W] = (rr * scale).astype(BF16)
    _row_loop(bm, sub, project)


def _kvproj_body(c_ref, g_ref, w_ref, kr_ref, tab_ref, k_ref, vt_ref, cn, *, hpb, av):
    @pl.when(pl.program_id(1) == 0)
    def _():
        cn[...] = _rms(c_ref[...], g_ref[...]).astype(BF16)

    acc = jnp.dot(cn[...], w_ref[...], preferred_element_type=F32)
    kr = _rope128(kr_ref[...], tab_ref[...]).astype(BF16)
    for hh in range(hpb):
        src = hh * (NOPE + av)
        k_ref[:, hh * QK_W:hh * QK_W + NOPE] = acc[:, src:src + NOPE].astype(BF16)
        k_ref[:, hh * QK_W + NOPE:(hh + 1) * QK_W] = kr
        vt_ref[hh, 0, 0:av, :] = acc[:, src + NOPE:src + NOPE + av].T.astype(BF16)
        vt_ref[hh, 0, av:av + ONES_ROWS, :] = jnp.ones((ONES_ROWS, vt_ref.shape[3]), BF16)


def _qproj(b_misc, g_cq, w_uq_r, tab, *, bm, hpb, scale):
    lp = b_misc.shape[0]
    cw = w_uq_r.shape[0]
    n = w_uq_r.shape[1]
    bn = hpb * QK_W
    return pl.pallas_call(
        functools.partial(_qproj_body, hpb=hpb, scale=scale, bm=bm, sub=_div_tile(bm, MM_SUB_ROWS, 16)),
        grid=(lp // bm, n // bn),
        in_specs=[
            pl.BlockSpec((bm, cw), lambda i, j: (i, 0)),
            pl.BlockSpec((1, cw), lambda i, j: (0, 0)),
            pl.BlockSpec((cw, bn), lambda i, j: (0, j)),
            pl.BlockSpec((bm, ROPE_W), lambda i, j: (i, 0)),
        ],
        out_specs=pl.BlockSpec((bm, bn), lambda i, j: (i, j)),
        out_shape=jax.ShapeDtypeStruct((lp, n), BF16),
        scratch_shapes=[pltpu.VMEM((bm, cw), BF16)],
        compiler_params=_params(("parallel", "arbitrary")),
        name="mla_qproj",
    )(b_misc, g_cq, w_uq_r, tab)


def _kvproj(b_misc, g_ckv, w_ukv, tab, *, hpb, av, ckv_blk, kr_blk):
    lp = b_misc.shape[0]
    cw = w_ukv.shape[0]
    n = w_ukv.shape[1]
    heads = n // (NOPE + av)
    bn = hpb * (NOPE + av)
    bm = ATT_CHUNK
    return pl.pallas_call(
        functools.partial(_kvproj_body, hpb=hpb, av=av),
        grid=(lp // bm, n // bn),
        in_specs=[
            pl.BlockSpec((bm, cw), lambda i, j: (i, ckv_blk)),
            pl.BlockSpec((1, cw), lambda i, j: (0, 0)),
            pl.BlockSpec((cw, bn), lambda i, j: (0, j)),
            pl.BlockSpec((bm, ROPE_W), lambda i, j: (i, kr_blk)),
            pl.BlockSpec((bm, ROPE_W), lambda i, j: (i, 0)),
        ],
        out_specs=[
            pl.BlockSpec((bm, hpb * QK_W), lambda i, j: (i, j)),
            pl.BlockSpec((hpb, 1, av + ONES_ROWS, bm), lambda i, j: (j, i, 0, 0)),
        ],
        out_shape=[jax.ShapeDtypeStruct((lp, heads * QK_W), BF16),
                   jax.ShapeDtypeStruct((heads, lp // bm, av + ONES_ROWS, bm), BF16)],
        scratch_shapes=[pltpu.VMEM((bm, cw), BF16)],
        compiler_params=_params(("parallel", "arbitrary")),
        name="mla_kvproj",
    )(b_misc, g_ckv, w_ukv, b_misc, tab)


def _attn_body(q_ref, k_ref, vt_ref, o_ref, m_sc, acc_sc, s_even, s_odd, c_even, c_odd):
    i = pl.program_id(1)
    n_slab = ATT_TILE // ATT_SLAB
    meta0 = T0 - N_META
    all_slabs = tuple(range(n_slab))
    late_slabs = tuple(s for s in all_slabs if (s + 1) * ATT_SLAB > ATT_CHUNK)

    av = o_ref.shape[1]
    m_sc[...] = jnp.full_like(m_sc, NEG_SCORE)
    acc_sc[...] = jnp.zeros_like(acc_sc)

    def scores(j, buf, masked, slabs):
        s_buf, c_buf = buf
        start = j * ATT_CHUNK
        if not isinstance(j, int):
            start = pl.multiple_of(start, ATT_CHUNK)
        k = k_ref[pl.ds(start, ATT_CHUNK), :]
        if masked:
            kpos = j * ATT_CHUNK + lax.broadcasted_iota(jnp.int32, (ATT_CHUNK, ATT_SLAB), 0)
            qlane = lax.broadcasted_iota(jnp.int32, (ATT_CHUNK, ATT_SLAB), 1)
        for s in slabs:
            q = q_ref[s * ATT_SLAB:(s + 1) * ATT_SLAB, :]
            st = lax.dot_general(k, q, (((1,), (1,)), ((), ())), preferred_element_type=F32)
            if masked:
                qpos = i * ATT_TILE + s * ATT_SLAB + qlane
                keep = jnp.logical_and(kpos <= qpos, kpos >= meta0)
                st = jnp.where(keep, st, NEG_SCORE)
            s_buf[s] = st
            c_buf[s] = jnp.max(st, axis=0, keepdims=True)

    def absorb(j, buf, slabs):
        s_buf, c_buf = buf
        vt = vt_ref[0, j]
        for s in slabs:
            m_prev = m_sc[s]
            m_new = jnp.maximum(m_prev, c_buf[s])
            alpha = jnp.exp2(m_prev - m_new)
            p = jnp.exp2(s_buf[s] - m_new)
            acc_sc[s] = alpha * acc_sc[s] + jnp.dot(vt, p.astype(BF16), preferred_element_type=F32)
            m_sc[s] = m_new

    even = (s_even, c_even)
    odd = (s_odd, c_odd)

    def pair(p, mask_odd, slabs_odd, mask_next):
        scores(2 * p + 1, odd, mask_odd, slabs_odd)
        absorb(2 * p, even, all_slabs)
        if mask_next is not None:
            scores(2 * p + 2, even, mask_next, all_slabs)
        absorb(2 * p + 1, odd, slabs_odd)

    scores(0, even, True, all_slabs)

    @pl.when(i == 1)
    def _():
        pair(0, False, all_slabs, True)

    @pl.when(i > 1)
    def _():
        pair(0, False, all_slabs, False)

        def mid(p, carry):
            pair(p, False, all_slabs, False)
            return carry
        lax.fori_loop(1, i - 1, mid, 0)
        pair(i - 1, False, all_slabs, True)

    pair(i, True, late_slabs, None)

    for s in all_slabs:
        out = (acc_sc[s, 0:av, :] / acc_sc[s, av:av + 1, :]).T
        o_ref[s * ATT_SLAB:(s + 1) * ATT_SLAB, :] = out.astype(o_ref.dtype)


def _attention(q, k, vt, *, heads, av):
    lp = q.shape[0]
    n_slab = ATT_TILE // ATT_SLAB
    assert lp % ATT_TILE == 0 and ATT_TILE % ATT_SLAB == 0 and ATT_TILE == 2 * ATT_CHUNK
    return pl.pallas_call(
        _attn_body,
        grid=(heads, lp // ATT_TILE),
        in_specs=[
            pl.BlockSpec((ATT_TILE, QK_W), lambda h, i: (i, h)),
            pl.BlockSpec((lp, QK_W), lambda h, i: (0, h)),
            pl.BlockSpec((1, lp // ATT_CHUNK, av + ONES_ROWS, ATT_CHUNK), lambda h, i: (h, 0, 0, 0)),
        ],
        out_specs=pl.BlockSpec((ATT_TILE, av), lambda h, i: (i, h)),
        out_shape=jax.ShapeDtypeStruct((lp, heads * av), BF16),
        scratch_shapes=[
            pltpu.VMEM((n_slab, 1, ATT_SLAB), F32),
            pltpu.VMEM((n_slab, av + ONES_ROWS, ATT_SLAB), F32),
            pltpu.VMEM((n_slab, ATT_CHUNK, ATT_SLAB), F32),
            pltpu.VMEM((n_slab, ATT_CHUNK, ATT_SLAB), F32),
            pltpu.VMEM((n_slab, 1, ATT_SLAB), F32),
            pltpu.VMEM((n_slab, 1, ATT_SLAB), F32),
        ],
        compiler_params=_params(("parallel", "parallel")),
        name="mla_attention",
    )(q, k, vt)


def _cast_rows_body(x_ref, o_ref, *, n_src):
    @pl.when(pl.program_id(0) < n_src)
    def _():
        o_ref[...] = x_ref[...].astype(BF16)

    @pl.when(pl.program_id(0) >= n_src)
    def _():
        o_ref[...] = jnp.zeros_like(o_ref)


def _cast_rows(w, layer, rows_out=None):
    _, rows, cols = w.shape
    rows_out = rows if rows_out is None else rows_out
    rb = _div_tile(rows, CAST_ROWS, 8)
    assert rows_out % rb == 0
    n_src = rows // rb
    return pl.pallas_call(
        functools.partial(_cast_rows_body, n_src=n_src),
        grid=(rows_out // rb,),
        in_specs=[pl.BlockSpec((None, rb, cols), lambda i: (layer, jnp.minimum(i, n_src - 1), 0))],
        out_specs=pl.BlockSpec((rb, cols), lambda i: (i, 0)),
        out_shape=jax.ShapeDtypeStruct((rows_out, cols), BF16),
        compiler_params=_params(("parallel",)),
        name="cast_bf16",
    )(w)


def _cast_transposed_body(x_ref, o_ref):
    o_ref[...] = x_ref[...].T.astype(BF16)


def _cast_transposed(w_t, layer):
    _, cols, rows = w_t.shape
    cb = 2 * CAST_ROWS
    return pl.pallas_call(
        _cast_transposed_body,
        grid=(pl.cdiv(cols, cb),),
        in_specs=[pl.BlockSpec((None, cb, rows), lambda i: (layer, i, 0))],
        out_specs=pl.BlockSpec((rows, cb), lambda i: (0, i)),
        out_shape=jax.ShapeDtypeStruct((rows, cols), BF16),
        compiler_params=_params(("parallel",)),
        name="cast_transposed",
    )(w_t)


def _cast_gate_up_body(x_ref, g_ref, u_ref, *, ff):
    pad = g_ref.shape[1] - ff
    g_ref[:, :ff] = x_ref[:, :ff].astype(BF16)
    u_ref[:, :ff] = x_ref[:, ff:].astype(BF16)
    if pad:
        g_ref[:, ff:] = jnp.zeros((g_ref.shape[0], pad), BF16)
        u_ref[:, ff:] = jnp.zeros((u_ref.shape[0], pad), BF16)


def _cast_gate_up(w_gu, layer, ffp):
    _, d, ff2 = w_gu.shape
    ff = ff2 // 2
    rb = _div_tile(d, CAST_ROWS // 2, 8)
    out = jax.ShapeDtypeStruct((d, ffp), BF16)
    return pl.pallas_call(
        functools.partial(_cast_gate_up_body, ff=ff),
        grid=(d // rb,),
        in_specs=[pl.BlockSpec((None, rb, ff2), lambda i: (layer, i, 0))],
        out_specs=[pl.BlockSpec((rb, ffp), lambda i: (i, 0))] * 2,
        out_shape=[out, out],
        compiler_params=_params(("parallel",)),
        name="cast_gate_up",
    )(w_gu)


def _swap_half(w):
    half = w.shape[-1] // 2
    return jnp.concatenate([-w[..., half:], w[..., :half]], axis=-1)


def _prep_layer_weights(layer, w_in_all, w_uq_all, w_ukv_all, w_out_all, w_gu_all, w_down_all, *, mw, hk, ffp):
    w_uq = _cast_rows(w_uq_all, layer)
    d = w_in_all.shape[1]
    o_gate = 2 * hk + 2 * mw
    o_cq = o_gate + 2 * M_HEADS
    o_ckv = o_cq + Q_LORA
    o_kr = o_ckv + KV_LORA
    w_main = _cast_transposed(jnp.swapaxes(w_in_all, 1, 2), layer)
    w_tail = w_main[:, o_gate:]
    wkr = w_tail[:, o_kr - o_gate:o_kr - o_gate + ROPE]
    gate_pad = jnp.zeros((d, GATE_W - 2 * M_HEADS), w_tail.dtype)
    w_b = jnp.concatenate([w_tail[:, o_cq - o_gate:o_kr - o_gate], w_tail[:, :o_cq - o_gate], gate_pad, wkr,
                           _swap_half(wkr)], axis=1)

    uq = w_uq.reshape(Q_LORA, A_HEADS, NOPE + ROPE)
    uq_r = jnp.concatenate([uq[..., :NOPE], uq[..., NOPE:], _swap_half(uq[..., NOPE:])], axis=-1)
    uq_r = uq_r.reshape(Q_LORA, A_HEADS * QK_W)

    w_gate, w_up = _cast_gate_up(w_gu_all, layer, ffp)
    return dict(w_in=w_main, w_b=w_b, uq=uq_r, ukv=_cast_rows(w_ukv_all, layer),
                w_out=_cast_rows(w_out_all, layer), w_gate=w_gate, w_up=w_up,
                w_down=_cast_rows(w_down_all, layer, ffp))


def _rope_table(lp):
    meta0 = T0 - N_META
    pos = jnp.maximum(jnp.arange(lp, dtype=jnp.int32) - meta0, 0).astype(F32)
    inv_freq = ROPE_THETA ** (-jnp.arange(ROPE // 2, dtype=F32) / (ROPE // 2))
    ang = pos[:, None] * inv_freq[None, :]
    cos, sin = jnp.cos(ang), jnp.sin(ang)
    return jnp.concatenate([cos, cos, sin, sin], axis=-1)


def kernel(x, meta, g_mix_pre, w_in, conv_w, b_gates, g_mnorm, g_cq, w_uq, g_ckv, w_ukv, w_out,
           g_mix_post, g_ffn_pre, w_gu, w_down, g_ffn_post):
    batch, seq, d = x.shape
    assert batch == 1 and seq % T0 == 0
    depth = w_in.shape[0]
    lp = T0 + seq
    mw = d // 2
    dv = mw // M_HEADS
    dk = dv // 2
    hk = M_HEADS * dk
    aw = d - mw
    av = aw // A_HEADS
    ff = w_down.shape[1]
    assert Q_LORA % KV_LORA == 0 and KV_LORA % ROPE_W == 0 and ROPE_W == GATE_W and 2 * hk == mw

    bf = 512
    ffp = -(-ff // 1024) * 1024
    bm = _div_tile(lp, 1664, 128)
    bn_in = _div_tile(mw, 512, 128)
    scale = (NOPE + ROPE) ** -0.5 * LOG2_E

    ckv_blk = Q_LORA // KV_LORA
    gate_blk = (Q_LORA + KV_LORA) // GATE_W
    kr_blk = (Q_LORA + KV_LORA + GATE_W) // ROPE_W
    nb = Q_LORA + KV_LORA + GATE_W + ROPE_W

    tab = _rope_table(lp)
    h, u = _prep(x[0], meta, g_mix_pre[0][None])

    out = None
    for l in range(depth):
        w = _prep_layer_weights(l, w_in, w_uq, w_ukv, w_out, w_gu, w_down, mw=mw, hk=hk, ffp=ffp)
        a_qk = _matmul([u], [(w["w_in"], 0, 0)], mw, F32, bm=bm, bn=bn_in, name="inproj_qk")
        v_m = _matmul([u], [(w["w_in"], 0, mw // bn_in)], mw, BF16, bm=bm, bn=bn_in, name="inproj_v")
        a_o = _matmul([u], [(w["w_in"], 0, 2 * mw // bn_in)], mw, F32, bm=bm, bn=bn_in, name="inproj_o")
        b_misc = _matmul([u], [(w["w_b"], 0, 0)], nb, F32, bm=bm, bn=_div_tile(nb, 768, 128),
                         name="inproj_misc")

        gates_t = b_misc[:, Q_LORA + KV_LORA:Q_LORA + KV_LORA + 2 * M_HEADS].T
        bias_c = jnp.pad(b_gates[l], (0, GATE_W - 2 * M_HEADS))[None, :]
        bias_r = b_gates[l][:, None]
        h_m = _mlstm(a_qk, v_m, a_o, b_misc, gates_t, conv_w[l], bias_c, bias_r, g_mnorm[l][None],
                     heads=M_HEADS, dk=dk, dv=dv, gate_blk=gate_blk)

        q_a = _qproj(b_misc, g_cq[l][None], w["uq"], tab, bm=_div_tile(lp, 832, 64), hpb=4, scale=scale)
        k_a, vt_a = _kvproj(b_misc, g_ckv[l][None], w["ukv"], tab, hpb=4, av=av,
                            ckv_blk=ckv_blk, kr_blk=kr_blk)
        h_a = _attention(q_a, k_a, vt_a, heads=A_HEADS, av=av)

        mix = _matmul([h_m, h_a], [(w["w_out"], 0, 0), (w["w_out"], 1, 0)], d, F32, bm=bm,
                      bn=_div_tile(d, 512, 128), name="outproj")
        h, u = _resnorm(h, mix, g_mix_post[l][None], g_ffn_pre[l][None])

        act = _matmul([u], [(w["w_gate"], 0, 0), (w["w_up"], 0, 0)], ffp, BF16,
                      bm=bm, bn=bf, swiglu=True, name="ffn_up")
        y = _matmul([act], [(w["w_down"], 0, 0)], d, F32, bm=bm, bn=_div_tile(d, 1024, 128),
                    bk=_div_tile(ffp, 2816, 128), name="ffn_down")
        if l + 1 < depth:
            h, u = _resnorm(h, y, g_ffn_post[l][None], g_mix_pre[l + 1][None])
        else:
            out = _resnorm_final(h, y, g_ffn_post[l][None])
    return out[None]
```

```python
import functools

import jax
import jax.numpy as jnp
from jax import lax
from jax.experimental import pallas as pl
from jax.experimental.pallas import tpu as pltpu

N_META = 16
M_HEADS = 4
CONV_W = 4
A_HEADS = 16
NOPE = 128
ROPE = 64
Q_LORA = 1536
KV_LORA = 512
ROPE_THETA = 10000.0
NORM_EPS = 1e-6
NEG_SCORE = -1e30
LOG2_E = 1.4426950408889634

T0 = 512
MLSTM_CHUNK = 256
ROW_BLOCK = 256
HALO = 8
QK_W = NOPE + 2 * ROPE
GATE_W = 128
ROPE_W = 2 * ROPE

VMEM_LIMIT_BYTES = 56 * 1024 * 1024
MM_SUB_ROWS = 512
CAST_ROWS = 128
ONES_ROWS = 16
ATT_SLAB = 512
ATT_CHUNK = 768
ATT_TILE = 2 * ATT_CHUNK

F32 = jnp.float32
BF16 = jnp.bfloat16


def _div_tile(n, target, mult):
    best = None
    t = mult
    while t <= min(n, target):
        if n % t == 0:
            best = t
        t += mult
    if best is None:
        raise ValueError(f"no tile for {n} (target {target}, multiple of {mult})")
    return best


def _params(sem):
    return pltpu.CompilerParams(dimension_semantics=sem, vmem_limit_bytes=VMEM_LIMIT_BYTES)


def _rms(t, g):
    return t * lax.rsqrt(jnp.mean(t * t, axis=-1, keepdims=True) + NORM_EPS) * g


def _row_loop(nrows, sub, fn):
    def step(r, carry):
        fn(pl.ds(pl.multiple_of(r * sub, sub), sub))
        return carry
    lax.fori_loop(0, nrows // sub, step, 0, unroll=True)


def _prep_body(x_ref, meta_ref, g_ref, h_ref, u_ref):
    i = pl.program_id(0)

    @pl.when(i == 0)
    def _():
        h_ref[...] = jnp.zeros_like(h_ref)
        h_ref[T0 - N_META:T0, :] = meta_ref[...]

    @pl.when(i > 0)
    def _():
        h_ref[...] = x_ref[...]

    u_ref[...] = _rms(h_ref[...], g_ref[...]).astype(BF16)


def _prep(x2d, meta, g):
    seq, d = x2d.shape
    lp = T0 + seq
    return pl.pallas_call(
        _prep_body,
        grid=(lp // T0,),
        in_specs=[
            pl.BlockSpec((T0, d), lambda i: (jnp.maximum(i - 1, 0), 0)),
            pl.BlockSpec((N_META, d), lambda i: (0, 0)),
            pl.BlockSpec((1, d), lambda i: (0, 0)),
        ],
        out_specs=[
            pl.BlockSpec((T0, d), lambda i: (i, 0)),
            pl.BlockSpec((T0, d), lambda i: (i, 0)),
        ],
        out_shape=[jax.ShapeDtypeStruct((lp, d), F32), jax.ShapeDtypeStruct((lp, d), BF16)],
        compiler_params=_params(("parallel",)),
        name="prep_norm",
    )(x2d, meta, g)


def _mm_body(*refs, n_lhs, nk, swiglu, bm, sub):
    lhs = refs[0:n_lhs]
    rhs = refs[n_lhs:-1]
    o_ref = refs[-1]

    def product(rows, b):
        acc = None
        for a, w in zip(lhs, b):
            d = jnp.dot(a[rows, :], w[...], preferred_element_type=F32)
            acc = d if acc is None else acc + d
        return acc

    def finish(rows):
        if swiglu:
            gate = product(rows, rhs[0:1])
            acc = gate * jax.nn.sigmoid(gate) * product(rows, rhs[1:2])
        else:
            acc = product(rows, rhs)
        o_ref[rows, :] = acc.astype(o_ref.dtype)

    if nk == 1:
        _row_loop(bm, sub, finish)
        return

    k = pl.program_id(2)

    def first(rows):
        o_ref[rows, :] = product(rows, rhs)

    def accumulate(rows):
        o_ref[rows, :] += product(rows, rhs)

    @pl.when(k == 0)
    def _():
        _row_loop(bm, sub, first)

    @pl.when(k > 0)
    def _():
        _row_loop(bm, sub, accumulate)


def _matmul(lhs, rhs, n, out_dtype, *, bm, bn, bk=None, swiglu=False, name):
    m = lhs[0].shape[0]
    if bk is None:
        nk = 1
    else:
        assert len(lhs) == 1 and not swiglu and out_dtype == F32
        nk = lhs[0].shape[1] // bk
    in_specs = []
    for a in lhs:
        kk = a.shape[1] if bk is None else bk
        in_specs.append(pl.BlockSpec((bm, kk), lambda i, j, k: (i, k)))
    for p, (w, rb, cb) in enumerate(rhs):
        a = lhs[0] if swiglu else lhs[p]
        kk = a.shape[1] if bk is None else bk
        in_specs.append(pl.BlockSpec((kk, bn), lambda i, j, k, rb=rb, cb=cb: (k + rb, j + cb)))
    return pl.pallas_call(
        functools.partial(_mm_body, n_lhs=len(lhs), nk=nk, swiglu=swiglu, bm=bm,
                          sub=_div_tile(bm, MM_SUB_ROWS, 16)),
        grid=(m // bm, n // bn, nk),
        in_specs=in_specs,
        out_specs=pl.BlockSpec((bm, bn), lambda i, j, k: (i, j)),
        out_shape=jax.ShapeDtypeStruct((m, n), out_dtype),
        compiler_params=_params(("parallel", "parallel", "arbitrary")),
        name=name,
    )(*lhs, *[w for w, _, _ in rhs])


def _resnorm_body(x_ref, y_ref, gp_ref, gn_ref, xo_ref, u_ref, *, rows):
    i = pl.program_id(0)
    row = i * rows + lax.broadcasted_iota(jnp.int32, (rows, 1), 0)
    xn = x_ref[...] + _rms(y_ref[...], gp_ref[...])
    xn = jnp.where(row >= T0 - N_META, xn, 0.0)
    xo_ref[...] = xn
    u_ref[...] = _rms(xn, gn_ref[...]).astype(BF16)


def _resnorm(x, y, g_post, g_next):
    lp, d = x.shape
    rows = ROW_BLOCK
    blk = pl.BlockSpec((rows, d), lambda i: (i, 0))
    vec = pl.BlockSpec((1, d), lambda i: (0, 0))
    return pl.pallas_call(
        functools.partial(_resnorm_body, rows=rows),
        grid=(lp // rows,),
        in_specs=[blk, blk, vec, vec],
        out_specs=[blk, blk],
        out_shape=[jax.ShapeDtypeStruct((lp, d), F32), jax.ShapeDtypeStruct((lp, d), BF16)],
        compiler_params=_params(("parallel",)),
        name="resnorm",
    )(x, y, g_post, g_next)


def _resnorm_final_body(x_ref, y_ref, gp_ref, o_ref):
    o_ref[...] = x_ref[...] + _rms(y_ref[...], gp_ref[...])


def _resnorm_final(x, y, g_post):
    lp, d = x.shape
    rows = ROW_BLOCK
    skip = T0 // rows
    blk_in = pl.BlockSpec((rows, d), lambda i: (i + skip, 0))
    return pl.pallas_call(
        _resnorm_final_body,
        grid=((lp - T0) // rows,),
        in_specs=[blk_in, blk_in, pl.BlockSpec((1, d), lambda i: (0, 0))],
        out_specs=pl.BlockSpec((rows, d), lambda i: (i, 0)),
        out_shape=jax.ShapeDtypeStruct((lp - T0, d), F32),
        compiler_params=_params(("parallel",)),
        name="resnorm_final",
    )(x, y, g_post)


def _log_sigmoid(x):
    return jnp.minimum(x, 0.0) - jnp.log1p(jnp.exp(-jnp.abs(x)))


def _mlstm_body(q_ref, k_ref, v_ref, o_ref, gc_ref, gr_ref, cw_ref, bc_ref, br_ref, gn_ref,
                out_ref, xext, c_sc, n_sc, m_sc, *, heads, dk, dv, lc):
    c = pl.program_id(0)
    hk = heads * dk
    meta0 = T0 - N_META

    @pl.when(c == 0)
    def _():
        xext[0:HALO, :] = jnp.zeros((HALO, 2 * hk), F32)
        c_sc[...] = jnp.zeros_like(c_sc)
        n_sc[...] = jnp.zeros_like(n_sc)
        m_sc[...] = jnp.zeros_like(m_sc)

    xext[HALO:HALO + lc, 0:hk] = q_ref[...]
    xext[HALO:HALO + lc, hk:2 * hk] = k_ref[...]
    conv = None
    for j in range(CONV_W):
        term = cw_ref[j:j + 1, :] * xext[pl.ds(HALO - (CONV_W - 1) + j, lc), :]
        conv = term if conv is None else conv + term
    xext[0:HALO, :] = xext[lc:lc + HALO, :]
    qk = conv * jax.nn.sigmoid(conv)

    row = c * lc + lax.broadcasted_iota(jnp.int32, (lc, 1), 0)
    col = c * lc + lax.broadcasted_iota(jnp.int32, (1, lc), 1)
    valid_c = row >= meta0
    valid_r = col >= meta0
    gc = gc_ref[...] + bc_ref[...]
    gr = gr_ref[...] + br_ref[...]
    tt = lax.broadcasted_iota(jnp.int32, (lc, lc), 0)
    ss = lax.broadcasted_iota(jnp.int32, (lc, lc), 1)
    tril = ss <= tt
    triu = tt <= ss
    neg_inf = -jnp.inf

    for h in range(heads):
        li_c = jnp.where(valid_c, gc[:, h:h + 1], neg_inf)
        lf_c = jnp.where(valid_c, _log_sigmoid(gc[:, heads + h:heads + h + 1]), 0.0)
        li_r = jnp.where(valid_r, gr[h:h + 1, :], neg_inf)
        lf_r = jnp.where(valid_r, _log_sigmoid(gr[heads + h:heads + h + 1, :]), 0.0)
        b_c = jnp.sum(jnp.where(tril, lf_r, 0.0), axis=1, keepdims=True)
        b_r = jnp.sum(jnp.where(triu, lf_c, 0.0), axis=0, keepdims=True)
        g = jnp.sum(lf_r, axis=1, keepdims=True)
        m = m_sc[h:h + 1, 0:1]

        d = jnp.where(tril, b_c - b_r + li_r, neg_inf)
        inter = b_c + m
        m_t = jnp.maximum(inter, jnp.max(d, axis=1, keepdims=True))
        w_inter = jnp.exp(inter - m_t)
        p = jnp.exp(d - m_t)

        qh = qk[:, h * dk:(h + 1) * dk] * (dk ** -0.5)
        kh = qk[:, hk + h * dk:hk + (h + 1) * dk]
        qb = qh.astype(BF16)
        s = lax.dot_general(qb, kh.astype(BF16), (((1,), (1,)), ((), ())),
                            preferred_element_type=F32) * p
        vh = v_ref[:, h * dv:(h + 1) * dv]
        ch = c_sc[h]
        nh = n_sc[h:h + 1, :]
        num = (w_inter * jnp.dot(qb, ch.astype(BF16), preferred_element_type=F32)
               + jnp.dot(s.astype(BF16), vh, preferred_element_type=F32))
        den = (w_inter * jnp.sum(qh * nh, axis=1, keepdims=True)
               + jnp.sum(s, axis=1, keepdims=True))
        hh = num / jnp.maximum(jnp.abs(den), jnp.exp(-m_t))

        a_c = g - b_c + li_c
        m_new = jnp.maximum(g + m, jnp.max(a_c, axis=0, keepdims=True))
        decay = jnp.exp(g + m - m_new)
        wk = kh * jnp.exp(a_c - m_new)
        c_sc[h] = decay * ch + lax.dot_general(wk.astype(BF16), vh, (((0,), (0,)), ((), ())),
                                               preferred_element_type=F32)
        n_sc[h:h + 1, :] = decay * nh + jnp.sum(wk, axis=0, keepdims=True)
        m_sc[h:h + 1, :] = jnp.broadcast_to(m_new, (1, m_sc.shape[1]))

        hn = _rms(hh, gn_ref[:, h * dv:(h + 1) * dv])
        og = jax.nn.sigmoid(o_ref[:, h * dv:(h + 1) * dv])
        out_ref[:, h * dv:(h + 1) * dv] = (og * hn).astype(BF16)


def _mlstm(a_qk, v, a_o, b_misc, gates_t, conv_w, bias_c, bias_r, g_mnorm, *, heads, dk, dv, gate_blk):
    lp = v.shape[0]
    lc = MLSTM_CHUNK
    hk = heads * dk
    mw = heads * dv
    assert 2 * heads <= 8
    return pl.pallas_call(
        functools.partial(_mlstm_body, heads=heads, dk=dk, dv=dv, lc=lc),
        grid=(lp // lc,),
        in_specs=[
            pl.BlockSpec((lc, hk), lambda c: (c, 0)),
            pl.BlockSpec((lc, hk), lambda c: (c, 1)),
            pl.BlockSpec((lc, mw), lambda c: (c, 0)),
            pl.BlockSpec((lc, mw), lambda c: (c, 0)),
            pl.BlockSpec((lc, GATE_W), lambda c: (c, gate_blk)),
            pl.BlockSpec((2 * heads, lc), lambda c: (0, c)),
            pl.BlockSpec((CONV_W, 2 * hk), lambda c: (0, 0)),
            pl.BlockSpec((1, GATE_W), lambda c: (0, 0)),
            pl.BlockSpec((2 * heads, 1), lambda c: (0, 0)),
            pl.BlockSpec((1, mw), lambda c: (0, 0)),
        ],
        out_specs=pl.BlockSpec((lc, mw), lambda c: (c, 0)),
        out_shape=jax.ShapeDtypeStruct((lp, mw), BF16),
        scratch_shapes=[
            pltpu.VMEM((lc + HALO, 2 * hk), F32),
            pltpu.VMEM((heads, dk, dv), F32),
            pltpu.VMEM((8, dk), F32),
            pltpu.VMEM((8, 128), F32),
        ],
        compiler_params=_params(("arbitrary",)),
        name="mlstm",
    )(a_qk, a_qk, v, a_o, b_misc, gates_t, conv_w, bias_c, bias_r, g_mnorm)


def _rope128(t, tab):
    pr = t * tab
    rr = pr + pltpu.roll(pr, ROPE, axis=1)
    lane = lax.broadcasted_iota(jnp.int32, pr.shape, 1)
    return jnp.where(lane < ROPE, rr, 0.0)


def _qproj_body(c_ref, g_ref, w_ref, tab_ref, o_ref, cn, *, hpb, scale, bm, sub):
    @pl.when(pl.program_id(1) == 0)
    def _():
        def norm(rows):
            cn[rows, :] = _rms(c_ref[rows, :], g_ref[...]).astype(BF16)
        _row_loop(bm, sub, norm)

    def project(rows):
        acc = jnp.dot(cn[rows, :], w_ref[...], preferred_element_type=F32)
        tab = tab_ref[rows, :]
        for hh in range(hpb):
            base = hh * QK_W
            o_ref[rows, base:base + NOPE] = (acc[:, base:base + NOPE] * scale).astype(BF16)
            rr = _rope128(acc[:, base + NOPE:base + QK_W], tab)
            o_ref[rows, base + NOPE:base + QK_W] = (rr * scale).astype(BF16)
    _row_loop(bm, sub, project)


def _kvproj_body(c_ref, g_ref, w_ref, kr_ref, tab_ref, k_ref, vt_ref, cn, *, hpb, av):
    @pl.when(pl.program_id(1) == 0)
    def _():
        cn[...] = _rms(c_ref[...], g_ref[...]).astype(BF16)

    acc = jnp.dot(cn[...], w_ref[...], preferred_element_type=F32)
    kr = _rope128(kr_ref[...], tab_ref[...]).astype(BF16)
    for hh in range(hpb):
        src = hh * (NOPE + av)
        k_ref[:, hh * QK_W:hh * QK_W + NOPE] = acc[:, src:src + NOPE].astype(BF16)
        k_ref[:, hh * QK_W + NOPE:(hh + 1) * QK_W] = kr
        vt_ref[hh, 0, 0:av, :] = acc[:, src + NOPE:src + NOPE + av].T.astype(BF16)
        vt_ref[hh, 0, av:av + ONES_ROWS, :] = jnp.ones((ONES_ROWS, vt_ref.shape[3]), BF16)


def _qproj(b_misc, g_cq, w_uq_r, tab, *, bm, hpb, scale):
    lp = b_misc.shape[0]
    cw = w_uq_r.shape[0]
    n = w_uq_r.shape[1]
    bn = hpb * QK_W
    return pl.pallas_call(
        functools.partial(_qproj_body, hpb=hpb, scale=scale, bm=bm, sub=_div_tile(bm, MM_SUB_ROWS, 16)),
        grid=(lp // bm, n // bn),
        in_specs=[
            pl.BlockSpec((bm, cw), lambda i, j: (i, 0)),
            pl.BlockSpec((1, cw), lambda i, j: (0, 0)),
            pl.BlockSpec((cw, bn), lambda i, j: (0, j)),
            pl.BlockSpec((bm, ROPE_W), lambda i, j: (i, 0)),
        ],
        out_specs=pl.BlockSpec((bm, bn), lambda i, j: (i, j)),
        out_shape=jax.ShapeDtypeStruct((lp, n), BF16),
        scratch_shapes=[pltpu.VMEM((bm, cw), BF16)],
        compiler_params=_params(("parallel", "arbitrary")),
        name="mla_qproj",
    )(b_misc, g_cq, w_uq_r, tab)


def _kvproj(b_misc, g_ckv, w_ukv, tab, *, hpb, av, ckv_blk, kr_blk):
    lp = b_misc.shape[0]
    cw = w_ukv.shape[0]
    n = w_ukv.shape[1]
    heads = n // (NOPE + av)
    bn = hpb * (NOPE + av)
    bm = ATT_CHUNK
    return pl.pallas_call(
        functools.partial(_kvproj_body, hpb=hpb, av=av),
        grid=(lp // bm, n // bn),
        in_specs=[
            pl.BlockSpec((bm, cw), lambda i, j: (i, ckv_blk)),
            pl.BlockSpec((1, cw), lambda i, j: (0, 0)),
            pl.BlockSpec((cw, bn), lambda i, j: (0, j)),
            pl.BlockSpec((bm, ROPE_W), lambda i, j: (i, kr_blk)),
            pl.BlockSpec((bm, ROPE_W), lambda i, j: (i, 0)),
        ],
        out_specs=[
            pl.BlockSpec((bm, hpb * QK_W), lambda i, j: (i, j)),
            pl.BlockSpec((hpb, 1, av + ONES_ROWS, bm), lambda i, j: (j, i, 0, 0)),
        ],
        out_shape=[jax.ShapeDtypeStruct((lp, heads * QK_W), BF16),
                   jax.ShapeDtypeStruct((heads, lp // bm, av + ONES_ROWS, bm), BF16)],
        scratch_shapes=[pltpu.VMEM((bm, cw), BF16)],
        compiler_params=_params(("parallel", "arbitrary")),
        name="mla_kvproj",
    )(b_misc, g_ckv, w_ukv, b_misc, tab)


def _attn_body(q_ref, k_ref, vt_ref, o_ref, m_sc, acc_sc, s_even, s_odd, c_even, c_odd):
    i = pl.program_id(1)
    n_slab = ATT_TILE // ATT_SLAB
    meta0 = T0 - N_META
    all_slabs = tuple((s, ATT_CHUNK) for s in range(n_slab))
    late_slabs = tuple((s, min(ATT_CHUNK, (s + 1) * ATT_SLAB - ATT_CHUNK)) for s in range(n_slab)
                       if (s + 1) * ATT_SLAB > ATT_CHUNK)

    av = o_ref.shape[1]
    m_sc[...] = jnp.full_like(m_sc, NEG_SCORE)
    acc_sc[...] = jnp.zeros_like(acc_sc)

    def scores(j, buf, masked, slabs):
        s_buf, c_buf = buf
        start = j * ATT_CHUNK
        if not isinstance(j, int):
            start = pl.multiple_of(start, ATT_CHUNK)
        for s, nkeys in slabs:
            k = k_ref[pl.ds(start, nkeys), :]
            q = q_ref[s * ATT_SLAB:(s + 1) * ATT_SLAB, :]
            st = lax.dot_general(k, q, (((1,), (1,)), ((), ())), preferred_element_type=F32)
            if masked:
                kpos = j * ATT_CHUNK + lax.broadcasted_iota(jnp.int32, (nkeys, ATT_SLAB), 0)
                qpos = (i * ATT_TILE + s * ATT_SLAB
                        + lax.broadcasted_iota(jnp.int32, (nkeys, ATT_SLAB), 1))
                keep = jnp.logical_and(kpos <= qpos, kpos >= meta0)
                st = jnp.where(keep, st, NEG_SCORE)
            s_buf[s, 0:nkeys, :] = st
            c_buf[s] = jnp.max(st, axis=0, keepdims=True)

    def absorb(j, buf, slabs):
        s_buf, c_buf = buf
        for s, nkeys in slabs:
            m_prev = m_sc[s]
            m_new = jnp.maximum(m_prev, c_buf[s])
            alpha = jnp.exp2(m_prev - m_new)
            p = jnp.exp2(s_buf[s, 0:nkeys, :] - m_new)
            acc_sc[s] = alpha * acc_sc[s] + jnp.dot(vt_ref[0, j, :, 0:nkeys], p.astype(BF16),
                                                    preferred_element_type=F32)
            m_sc[s] = m_new

    even = (s_even, c_even)
    odd = (s_odd, c_odd)

    def pair(p, mask_odd, slabs_odd, mask_next):
        scores(2 * p + 1, odd, mask_odd, slabs_odd)
        absorb(2 * p, even, all_slabs)
        if mask_next is not None:
            scores(2 * p + 2, even, mask_next, all_slabs)
        absorb(2 * p + 1, odd, slabs_odd)

    scores(0, even, True, all_slabs)

    @pl.when(i == 1)
    def _():
        pair(0, False, all_slabs, True)

    @pl.when(i > 1)
    def _():
        pair(0, False, all_slabs, False)

        def mid(p, carry):
            pair(p, False, all_slabs, False)
            return carry
        lax.fori_loop(1, i - 1, mid, 0)
        pair(i - 1, False, all_slabs, True)

    pair(i, True, late_slabs, None)

    for s in range(n_slab):
        out = (acc_sc[s, 0:av, :] / acc_sc[s, av:av + 1, :]).T
        o_ref[s * ATT_SLAB:(s + 1) * ATT_SLAB, :] = out.astype(o_ref.dtype)


def _attention(q, k, vt, *, heads, av):
    lp = q.shape[0]
    n_slab = ATT_TILE // ATT_SLAB
    assert lp % ATT_TILE == 0 and ATT_TILE % ATT_SLAB == 0 and ATT_TILE == 2 * ATT_CHUNK
    return pl.pallas_call(
        _attn_body,
        grid=(heads, lp // ATT_TILE),
        in_specs=[
            pl.BlockSpec((ATT_TILE, QK_W), lambda h, i: (i, h)),
            pl.BlockSpec((lp, QK_W), lambda h, i: (0, h)),
            pl.BlockSpec((1, lp // ATT_CHUNK, av + ONES_ROWS, ATT_CHUNK), lambda h, i: (h, 0, 0, 0)),
        ],
        out_specs=pl.BlockSpec((ATT_TILE, av), lambda h, i: (i, h)),
        out_shape=jax.ShapeDtypeStruct((lp, heads * av), BF16),
        scratch_shapes=[
            pltpu.VMEM((n_slab, 1, ATT_SLAB), F32),
            pltpu.VMEM((n_slab, av + ONES_ROWS, ATT_SLAB), F32),
            pltpu.VMEM((n_slab, ATT_CHUNK, ATT_SLAB), F32),
            pltpu.VMEM((n_slab, ATT_CHUNK, ATT_SLAB), F32),
            pltpu.VMEM((n_slab, 1, ATT_SLAB), F32),
            pltpu.VMEM((n_slab, 1, ATT_SLAB), F32),
        ],
        compiler_params=_params(("parallel", "parallel")),
        name="mla_attention",
    )(q, k, vt)


def _cast_rows_body(x_ref, o_ref, *, n_src):
    @pl.when(pl.program_id(0) < n_src)
    def _():
        o_ref[...] = x_ref[...].astype(BF16)

    @pl.when(pl.program_id(0) >= n_src)
    def _():
        o_ref[...] = jnp.zeros_like(o_ref)


def _cast_rows(w, layer, rows_out=None):
    _, rows, cols = w.shape
    rows_out = rows if rows_out is None else rows_out
    rb = _div_tile(rows, CAST_ROWS, 8)
    assert rows_out % rb == 0
    n_src = rows // rb
    return pl.pallas_call(
        functools.partial(_cast_rows_body, n_src=n_src),
        grid=(rows_out // rb,),
        in_specs=[pl.BlockSpec((None, rb, cols), lambda i: (layer, jnp.minimum(i, n_src - 1), 0))],
        out_specs=pl.BlockSpec((rb, cols), lambda i: (i, 0)),
        out_shape=jax.ShapeDtypeStruct((rows_out, cols), BF16),
        compiler_params=_params(("parallel",)),
        name="cast_bf16",
    )(w)


def _cast_transposed_body(x_ref, o_ref):
    o_ref[...] = x_ref[...].T.astype(BF16)


def _cast_transposed(w_t, layer):
    _, cols, rows = w_t.shape
    cb = 2 * CAST_ROWS
    return pl.pallas_call(
        _cast_transposed_body,
        grid=(pl.cdiv(cols, cb),),
        in_specs=[pl.BlockSpec((None, cb, rows), lambda i: (layer, i, 0))],
        out_specs=pl.BlockSpec((rows, cb), lambda i: (0, i)),
        out_shape=jax.ShapeDtypeStruct((rows, cols), BF16),
        compiler_params=_params(("parallel",)),
        name="cast_transposed",
    )(w_t)


def _cast_gate_up_body(x_ref, g_ref, u_ref, *, ff):
    pad = g_ref.shape[1] - ff
    g_ref[:, :ff] = x_ref[:, :ff].astype(BF16)
    u_ref[:, :ff] = x_ref[:, ff:].astype(BF16)
    if pad:
        g_ref[:, ff:] = jnp.zeros((g_ref.shape[0], pad), BF16)
        u_ref[:, ff:] = jnp.zeros((u_ref.shape[0], pad), BF16)


def _cast_gate_up(w_gu, layer, ffp):
    _, d, ff2 = w_gu.shape
    ff = ff2 // 2
    rb = _div_tile(d, CAST_ROWS // 2, 8)
    out = jax.ShapeDtypeStruct((d, ffp), BF16)
    return pl.pallas_call(
        functools.partial(_cast_gate_up_body, ff=ff),
        grid=(d // rb,),
        in_specs=[pl.BlockSpec((None, rb, ff2), lambda i: (layer, i, 0))],
        out_specs=[pl.BlockSpec((rb, ffp), lambda i: (i, 0))] * 2,
        out_shape=[out, out],
        compiler_params=_params(("parallel",)),
        name="cast_gate_up",
    )(w_gu)


def _swap_half(w):
    half = w.shape[-1] // 2
    return jnp.concatenate([-w[..., half:], w[..., :half]], axis=-1)


def _prep_layer_weights(layer, w_in_all, w_uq_all, w_ukv_all, w_out_all, w_gu_all, w_down_all, *, mw, hk, ffp):
    w_uq = _cast_rows(w_uq_all, layer)
    d = w_in_all.shape[1]
    o_gate = 2 * hk + 2 * mw
    o_cq = o_gate + 2 * M_HEADS
    o_ckv = o_cq + Q_LORA
    o_kr = o_ckv + KV_LORA
    w_main = _cast_transposed(jnp.swapaxes(w_in_all, 1, 2), layer)
    w_tail = w_main[:, o_gate:]
    wkr = w_tail[:, o_kr - o_gate:o_kr - o_gate + ROPE]
    gate_pad = jnp.zeros((d, GATE_W - 2 * M_HEADS), w_tail.dtype)
    w_b = jnp.concatenate([w_tail[:, o_cq - o_gate:o_kr - o_gate], w_tail[:, :o_cq - o_gate], gate_pad, wkr,
                           _swap_half(wkr)], axis=1)

    uq = w_uq.reshape(Q_LORA, A_HEADS, NOPE + ROPE)
    uq_r = jnp.concatenate([uq[..., :NOPE], uq[..., NOPE:], _swap_half(uq[..., NOPE:])], axis=-1)
    uq_r = uq_r.reshape(Q_LORA, A_HEADS * QK_W)

    w_gate, w_up = _cast_gate_up(w_gu_all, layer, ffp)
    return dict(w_in=w_main, w_b=w_b, uq=uq_r, ukv=_cast_rows(w_ukv_all, layer),
                w_out=_cast_rows(w_out_all, layer), w_gate=w_gate, w_up=w_up,
                w_down=_cast_rows(w_down_all, layer, ffp))


def _rope_table(lp):
    meta0 = T0 - N_META
    pos = jnp.maximum(jnp.arange(lp, dtype=jnp.int32) - meta0, 0).astype(F32)
    inv_freq = ROPE_THETA ** (-jnp.arange(ROPE // 2, dtype=F32) / (ROPE // 2))
    ang = pos[:, None] * inv_freq[None, :]
    cos, sin = jnp.cos(ang), jnp.sin(ang)
    return jnp.concatenate([cos, cos, sin, sin], axis=-1)


def kernel(x, meta, g_mix_pre, w_in, conv_w, b_gates, g_mnorm, g_cq, w_uq, g_ckv, w_ukv, w_out,
           g_mix_post, g_ffn_pre, w_gu, w_down, g_ffn_post):
    batch, seq, d = x.shape
    assert batch == 1 and seq % T0 == 0
    depth = w_in.shape[0]
    lp = T0 + seq
    mw = d // 2
    dv = mw // M_HEADS
    dk = dv // 2
    hk = M_HEADS * dk
    aw = d - mw
    av = aw // A_HEADS
    ff = w_down.shape[1]
    assert Q_LORA % KV_LORA == 0 and KV_LORA % ROPE_W == 0 and ROPE_W == GATE_W and 2 * hk == mw

    bf = 512
    ffp = -(-ff // 1024) * 1024
    bm = _div_tile(lp, 1664, 128)
    bn_in = _div_tile(mw, 512, 128)
    scale = (NOPE + ROPE) ** -0.5 * LOG2_E

    ckv_blk = Q_LORA // KV_LORA
    gate_blk = (Q_LORA + KV_LORA) // GATE_W
    kr_blk = (Q_LORA + KV_LORA + GATE_W) // ROPE_W
    nb = Q_LORA + KV_LORA + GATE_W + ROPE_W

    tab = _rope_table(lp)
    h, u = _prep(x[0], meta, g_mix_pre[0][None])

    out = None
    for l in range(depth):
        w = _prep_layer_weights(l, w_in, w_uq, w_ukv, w_out, w_gu, w_down, mw=mw, hk=hk, ffp=ffp)
        a_qk = _matmul([u], [(w["w_in"], 0, 0)], mw, F32, bm=bm, bn=bn_in, name="inproj_qk")
        v_m = _matmul([u], [(w["w_in"], 0, mw // bn_in)], mw, BF16, bm=bm, bn=bn_in, name="inproj_v")
        a_o = _matmul([u], [(w["w_in"], 0, 2 * mw // bn_in)], mw, F32, bm=bm, bn=bn_in, name="inproj_o")
        b_misc = _matmul([u], [(w["w_b"], 0, 0)], nb, F32, bm=bm, bn=_div_tile(nb, 768, 128),
                         name="inproj_misc")

        gates_t = b_misc[:, Q_LORA + KV_LORA:Q_LORA + KV_LORA + 2 * M_HEADS].T
        bias_c = jnp.pad(b_gates[l], (0, GATE_W - 2 * M_HEADS))[None, :]
        bias_r = b_gates[l][:, None]
        h_m = _mlstm(a_qk, v_m, a_o, b_misc, gates_t, conv_w[l], bias_c, bias_r, g_mnorm[l][None],
                     heads=M_HEADS, dk=dk, dv=dv, gate_blk=gate_blk)

        q_a = _qproj(b_misc, g_cq[l][None], w["uq"], tab, bm=_div_tile(lp, 832, 64), hpb=4, scale=scale)
        k_a, vt_a = _kvproj(b_misc, g_ckv[l][None], w["ukv"], tab, hpb=4, av=av,
                            ckv_blk=ckv_blk, kr_blk=kr_blk)
        h_a = _attention(q_a, k_a, vt_a, heads=A_HEADS, av=av)

        mix = _matmul([h_m, h_a], [(w["w_out"], 0, 0), (w["w_out"], 1, 0)], d, F32, bm=bm,
                      bn=_div_tile(d, 512, 128), name="outproj")
        h, u = _resnorm(h, mix, g_mix_post[l][None], g_ffn_pre[l][None])

        act = _matmul([u], [(w["w_gate"], 0, 0), (w["w_up"], 0, 0)], ffp, BF16,
                      bm=bm, bn=bf, swiglu=True, name="ffn_up")
        y = _matmul([act], [(w["w_down"], 0, 0)], d, F32, bm=bm, bn=_div_tile(d, 1024, 128),
                    bk=_div_tile(ffp, 2816, 128), name="ffn_down")
        if l + 1 < depth:
            h, u = _resnorm(h, y, g_ffn_post[l][None], g_mix_pre[l + 1][None])
        else:
            out = _resnorm_final(h, y, g_ffn_post[l][None])
    return out[None]
```

```python
import functools

import jax
import jax.numpy as jnp
from jax import lax
from jax.experimental import pallas as pl
from jax.experimental.pallas import tpu as pltpu

N_META = 16
M_HEADS = 4
CONV_W = 4
A_HEADS = 16
NOPE = 128
ROPE = 64
Q_LORA = 1536
KV_LORA = 512
ROPE_THETA = 10000.0
NORM_EPS = 1e-6
NEG_SCORE = -1e30
LOG2_E = 1.4426950408889634

T0 = 512
MLSTM_CHUNK = 256
ROW_BLOCK = 256
HALO = 8
QK_W = NOPE + 2 * ROPE
GATE_W = 128
ROPE_W = 2 * ROPE

VMEM_LIMIT_BYTES = 56 * 1024 * 1024
MM_SUB_ROWS = 512
CAST_ROWS = 128
ONES_ROWS = 16
ATT_SLAB = 512
ATT_CHUNK = 768
ATT_TILE = 2 * ATT_CHUNK

F32 = jnp.float32
BF16 = jnp.bfloat16


def _div_tile(n, target, mult):
    best = None
    t = mult
    while t <= min(n, target):
        if n % t == 0:
            best = t
        t += mult
    if best is None:
        raise ValueError(f"no tile for {n} (target {target}, multiple of {mult})")
    return best


def _params(sem):
    return pltpu.CompilerParams(dimension_semantics=sem, vmem_limit_bytes=VMEM_LIMIT_BYTES)


def _rms(t, g):
    return t * lax.rsqrt(jnp.mean(t * t, axis=-1, keepdims=True) + NORM_EPS) * g


def _row_loop(nrows, sub, fn):
    def step(r, carry):
        fn(pl.ds(pl.multiple_of(r * sub, sub), sub))
        return carry
    lax.fori_loop(0, nrows // sub, step, 0, unroll=True)


def _prep_body(x_ref, meta_ref, g_ref, h_ref, u_ref):
    i = pl.program_id(0)

    @pl.when(i == 0)
    def _():
        h_ref[...] = jnp.zeros_like(h_ref)
        h_ref[T0 - N_META:T0, :] = meta_ref[...]

    @pl.when(i > 0)
    def _():
        h_ref[...] = x_ref[...]

    u_ref[...] = _rms(h_ref[...], g_ref[...]).astype(BF16)


def _prep(x2d, meta, g):
    seq, d = x2d.shape
    lp = T0 + seq
    return pl.pallas_call(
        _prep_body,
        grid=(lp // T0,),
        in_specs=[
            pl.BlockSpec((T0, d), lambda i: (jnp.maximum(i - 1, 0), 0)),
            pl.BlockSpec((N_META, d), lambda i: (0, 0)),
            pl.BlockSpec((1, d), lambda i: (0, 0)),
        ],
        out_specs=[
            pl.BlockSpec((T0, d), lambda i: (i, 0)),
            pl.BlockSpec((T0, d), lambda i: (i, 0)),
        ],
        out_shape=[jax.ShapeDtypeStruct((lp, d), F32), jax.ShapeDtypeStruct((lp, d), BF16)],
        compiler_params=_params(("parallel",)),
        name="prep_norm",
    )(x2d, meta, g)


def _mm_body(*refs, n_lhs, nk, swiglu, bm, sub):
    lhs = refs[0:n_lhs]
    rhs = refs[n_lhs:-1]
    o_ref = refs[-1]

    def product(rows, b):
        acc = None
        for a, w in zip(lhs, b):
            d = jnp.dot(a[rows, :], w[...], preferred_element_type=F32)
            acc = d if acc is None else acc + d
        return acc

    def finish(rows):
        if swiglu:
            gate = product(rows, rhs[0:1])
            acc = gate * jax.nn.sigmoid(gate) * product(rows, rhs[1:2])
        else:
            acc = product(rows, rhs)
        o_ref[rows, :] = acc.astype(o_ref.dtype)

    if nk == 1:
        _row_loop(bm, sub, finish)
        return

    k = pl.program_id(2)

    def first(rows):
        o_ref[rows, :] = product(rows, rhs)

    def accumulate(rows):
        o_ref[rows, :] += product(rows, rhs)

    @pl.when(k == 0)
    def _():
        _row_loop(bm, sub, first)

    @pl.when(k > 0)
    def _():
        _row_loop(bm, sub, accumulate)


def _matmul(lhs, rhs, n, out_dtype, *, bm, bn, bk=None, swiglu=False, name):
    m = lhs[0].shape[0]
    if bk is None:
        nk = 1
    else:
        assert len(lhs) == 1 and not swiglu and out_dtype == F32
        nk = lhs[0].shape[1] // bk
    in_specs = []
    for a in lhs:
        kk = a.shape[1] if bk is None else bk
        in_specs.append(pl.BlockSpec((bm, kk), lambda i, j, k: (i, k)))
    for p, (w, rb, cb) in enumerate(rhs):
        a = lhs[0] if swiglu else lhs[p]
        kk = a.shape[1] if bk is None else bk
        in_specs.append(pl.BlockSpec((kk, bn), lambda i, j, k, rb=rb, cb=cb: (k + rb, j + cb)))
    return pl.pallas_call(
        functools.partial(_mm_body, n_lhs=len(lhs), nk=nk, swiglu=swiglu, bm=bm,
                          sub=_div_tile(bm, MM_SUB_ROWS, 16)),
        grid=(m // bm, n // bn, nk),
        in_specs=in_specs,
        out_specs=pl.BlockSpec((bm, bn), lambda i, j, k: (i, j)),
        out_shape=jax.ShapeDtypeStruct((m, n), out_dtype),
        compiler_params=_params(("parallel", "parallel", "arbitrary")),
        name=name,
    )(*lhs, *[w for w, _, _ in rhs])


def _resnorm_body(x_ref, y_ref, gp_ref, gn_ref, xo_ref, u_ref, *, rows):
    i = pl.program_id(0)
    row = i * rows + lax.broadcasted_iota(jnp.int32, (rows, 1), 0)
    xn = x_ref[...] + _rms(y_ref[...], gp_ref[...])
    xn = jnp.where(row >= T0 - N_META, xn, 0.0)
    xo_ref[...] = xn
    u_ref[...] = _rms(xn, gn_ref[...]).astype(BF16)


def _resnorm(x, y, g_post, g_next):
    lp, d = x.shape
    rows = ROW_BLOCK
    blk = pl.BlockSpec((rows, d), lambda i: (i, 0))
    vec = pl.BlockSpec((1, d), lambda i: (0, 0))
    return pl.pallas_call(
        functools.partial(_resnorm_body, rows=rows),
        grid=(lp // rows,),
        in_specs=[blk, blk, vec, vec],
        out_specs=[blk, blk],
        out_shape=[jax.ShapeDtypeStruct((lp, d), F32), jax.ShapeDtypeStruct((lp, d), BF16)],
        compiler_params=_params(("parallel",)),
        name="resnorm",
    )(x, y, g_post, g_next)


def _resnorm_final_body(x_ref, y_ref, gp_ref, o_ref):
    o_ref[...] = x_ref[...] + _rms(y_ref[...], gp_ref[...])


def _resnorm_final(x, y, g_post):
    lp, d = x.shape
    rows = ROW_BLOCK
    skip = T0 // rows
    blk_in = pl.BlockSpec((rows, d), lambda i: (i + skip, 0))
    return pl.pallas_call(
        _resnorm_final_body,
        grid=((lp - T0) // rows,),
        in_specs=[blk_in, blk_in, pl.BlockSpec((1, d), lambda i: (0, 0))],
        out_specs=pl.BlockSpec((rows, d), lambda i: (i, 0)),
        out_shape=jax.ShapeDtypeStruct((lp - T0, d), F32),
        compiler_params=_params(("parallel",)),
        name="resnorm_final",
    )(x, y, g_post)


def _log_sigmoid(x):
    return jnp.minimum(x, 0.0) - jnp.log1p(jnp.exp(-jnp.abs(x)))


def _mlstm_body(q_ref, k_ref, v_ref, o_ref, gc_ref, gr_ref, cw_ref, bc_ref, br_ref, gn_ref,
                out_ref, xext, c_sc, n_sc, m_sc, *, heads, dk, dv, lc):
    c = pl.program_id(0)
    hk = heads * dk
    meta0 = T0 - N_META

    @pl.when(c == 0)
    def _():
        xext[0:HALO, :] = jnp.zeros((HALO, 2 * hk), F32)
        c_sc[...] = jnp.zeros_like(c_sc)
        n_sc[...] = jnp.zeros_like(n_sc)
        m_sc[...] = jnp.zeros_like(m_sc)

    xext[HALO:HALO + lc, 0:hk] = q_ref[...]
    xext[HALO:HALO + lc, hk:2 * hk] = k_ref[...]
    conv = None
    for j in range(CONV_W):
        term = cw_ref[j:j + 1, :] * xext[pl.ds(HALO - (CONV_W - 1) + j, lc), :]
        conv = term if conv is None else conv + term
    xext[0:HALO, :] = xext[lc:lc + HALO, :]
    qk = conv * jax.nn.sigmoid(conv)

    row = c * lc + lax.broadcasted_iota(jnp.int32, (lc, 1), 0)
    col = c * lc + lax.broadcasted_iota(jnp.int32, (1, lc), 1)
    valid_c = row >= meta0
    valid_r = col >= meta0
    gc = gc_ref[...] + bc_ref[...]
    gr = gr_ref[...] + br_ref[...]
    tt = lax.broadcasted_iota(jnp.int32, (lc, lc), 0)
    ss = lax.broadcasted_iota(jnp.int32, (lc, lc), 1)
    tril = ss <= tt
    triu = tt <= ss
    neg_inf = -jnp.inf

    for h in range(heads):
        li_c = jnp.where(valid_c, gc[:, h:h + 1], neg_inf)
        lf_c = jnp.where(valid_c, _log_sigmoid(gc[:, heads + h:heads + h + 1]), 0.0)
        li_r = jnp.where(valid_r, gr[h:h + 1, :], neg_inf)
        lf_r = jnp.where(valid_r, _log_sigmoid(gr[heads + h:heads + h + 1, :]), 0.0)
        b_c = jnp.sum(jnp.where(tril, lf_r, 0.0), axis=1, keepdims=True)
        b_r = jnp.sum(jnp.where(triu, lf_c, 0.0), axis=0, keepdims=True)
        g = jnp.sum(lf_r, axis=1, keepdims=True)
        m = m_sc[h:h + 1, 0:1]

        d = jnp.where(tril, b_c - b_r + li_r, neg_inf)
        inter = b_c + m
        m_t = jnp.maximum(inter, jnp.max(d, axis=1, keepdims=True))
        w_inter = jnp.exp(inter - m_t)
        p = jnp.exp(d - m_t)

        qh = qk[:, h * dk:(h + 1) * dk] * (dk ** -0.5)
        kh = qk[:, hk + h * dk:hk + (h + 1) * dk]
        qb = qh.astype(BF16)
        s = lax.dot_general(qb, kh.astype(BF16), (((1,), (1,)), ((), ())),
                            preferred_element_type=F32) * p
        vh = v_ref[:, h * dv:(h + 1) * dv]
        ch = c_sc[h]
        nh = n_sc[h:h + 1, :]
        num = (w_inter * jnp.dot(qb, ch.astype(BF16), preferred_element_type=F32)
               + jnp.dot(s.astype(BF16), vh, preferred_element_type=F32))
        den = (w_inter * jnp.sum(qh * nh, axis=1, keepdims=True)
               + jnp.sum(s, axis=1, keepdims=True))
        hh = num / jnp.maximum(jnp.abs(den), jnp.exp(-m_t))

        a_c = g - b_c + li_c
        m_new = jnp.maximum(g + m, jnp.max(a_c, axis=0, keepdims=True))
        decay = jnp.exp(g + m - m_new)
        wk = kh * jnp.exp(a_c - m_new)
        c_sc[h] = decay * ch + lax.dot_general(wk.astype(BF16), vh, (((0,), (0,)), ((), ())),
                                               preferred_element_type=F32)
        n_sc[h:h + 1, :] = decay * nh + jnp.sum(wk, axis=0, keepdims=True)
        m_sc[h:h + 1, :] = jnp.broadcast_to(m_new, (1, m_sc.shape[1]))

        hn = _rms(hh, gn_ref[:, h * dv:(h + 1) * dv])
        og = jax.nn.sigmoid(o_ref[:, h * dv:(h + 1) * dv])
        out_ref[:, h * dv:(h + 1) * dv] = (og * hn).astype(BF16)


def _mlstm(a_qk, v, a_o, b_misc, gates_t, conv_w, bias_c, bias_r, g_mnorm, *, heads, dk, dv, gate_blk):
    lp = v.shape[0]
    lc = MLSTM_CHUNK
    hk = heads * dk
    mw = heads * dv
    assert 2 * heads <= 8
    return pl.pallas_call(
        functools.partial(_mlstm_body, heads=heads, dk=dk, dv=dv, lc=lc),
        grid=(lp // lc,),
        in_specs=[
            pl.BlockSpec((lc, hk), lambda c: (c, 0)),
            pl.BlockSpec((lc, hk), lambda c: (c, 1)),
            pl.BlockSpec((lc, mw), lambda c: (c, 0)),
            pl.BlockSpec((lc, mw), lambda c: (c, 0)),
            pl.BlockSpec((lc, GATE_W), lambda c: (c, gate_blk)),
            pl.BlockSpec((2 * heads, lc), lambda c: (0, c)),
            pl.BlockSpec((CONV_W, 2 * hk), lambda c: (0, 0)),
            pl.BlockSpec((1, GATE_W), lambda c: (0, 0)),
            pl.BlockSpec((2 * heads, 1), lambda c: (0, 0)),
            pl.BlockSpec((1, mw), lambda c: (0, 0)),
        ],
        out_specs=pl.BlockSpec((lc, mw), lambda c: (c, 0)),
        out_shape=jax.ShapeDtypeStruct((lp, mw), BF16),
        scratch_shapes=[
            pltpu.VMEM((lc + HALO, 2 * hk), F32),
            pltpu.VMEM((heads, dk, dv), F32),
            pltpu.VMEM((8, dk), F32),
            pltpu.VMEM((8, 128), F32),
        ],
        compiler_params=_params(("arbitrary",)),
        name="mlstm",
    )(a_qk, a_qk, v, a_o, b_misc, gates_t, conv_w, bias_c, bias_r, g_mnorm)


def _rope128(t, tab):
    pr = t * tab
    rr = pr + pltpu.roll(pr, ROPE, axis=1)
    lane = lax.broadcasted_iota(jnp.int32, pr.shape, 1)
    return jnp.where(lane < ROPE, rr, 0.0)


def _qproj_body(c_ref, g_ref, w_ref, tab_ref, o_ref, cn, *, hpb, scale, bm, sub):
    @pl.when(pl.program_id(1) == 0)
    def _():
        def norm(rows):
            cn[rows, :] = _rms(c_ref[rows, :], g_ref[...]).astype(BF16)
        _row_loop(bm, sub, norm)

    for r in range(bm // sub):
        lo, hi = r * sub, (r + 1) * sub
        acc = jnp.dot(cn[lo:hi, :], w_ref[...], preferred_element_type=F32)
        tab = tab_ref[lo:hi, :]
        for hh in range(hpb):
            base = hh * QK_W
            o_ref[base:base + NOPE, lo:hi] = (acc[:, base:base + NOPE] * scale).T.astype(BF16)
            rr = _rope128(acc[:, base + NOPE:base + QK_W], tab)
            o_ref[base + NOPE:base + QK_W, lo:hi] = (rr * scale).T.astype(BF16)


def _kvproj_body(c_ref, g_ref, w_ref, kr_ref, tab_ref, k_ref, vt_ref, cn, *, hpb, av):
    @pl.when(pl.program_id(1) == 0)
    def _():
        cn[...] = _rms(c_ref[...], g_ref[...]).astype(BF16)

    acc = jnp.dot(cn[...], w_ref[...], preferred_element_type=F32)
    kr = _rope128(kr_ref[...], tab_ref[...]).astype(BF16)
    for hh in range(hpb):
        src = hh * (NOPE + av)
        k_ref[:, hh * QK_W:hh * QK_W + NOPE] = acc[:, src:src + NOPE].astype(BF16)
        k_ref[:, hh * QK_W + NOPE:(hh + 1) * QK_W] = kr
        vt_ref[hh, 0, 0:av, :] = acc[:, src + NOPE:src + NOPE + av].T.astype(BF16)
        vt_ref[hh, 0, av:av + ONES_ROWS, :] = jnp.ones((ONES_ROWS, vt_ref.shape[3]), BF16)


def _qproj(b_misc, g_cq, w_uq_r, tab, *, bm, hpb, scale):
    lp = b_misc.shape[0]
    cw = w_uq_r.shape[0]
    n = w_uq_r.shape[1]
    bn = hpb * QK_W
    return pl.pallas_call(
        functools.partial(_qproj_body, hpb=hpb, scale=scale, bm=bm, sub=_div_tile(bm, MM_SUB_ROWS, 16)),
        grid=(lp // bm, n // bn),
        in_specs=[
            pl.BlockSpec((bm, cw), lambda i, j: (i, 0)),
            pl.BlockSpec((1, cw), lambda i, j: (0, 0)),
            pl.BlockSpec((cw, bn), lambda i, j: (0, j)),
            pl.BlockSpec((bm, ROPE_W), lambda i, j: (i, 0)),
        ],
        out_specs=pl.BlockSpec((bn, bm), lambda i, j: (j, i)),
        out_shape=jax.ShapeDtypeStruct((n, lp), BF16),
        scratch_shapes=[pltpu.VMEM((bm, cw), BF16)],
        compiler_params=_params(("parallel", "arbitrary")),
        name="mla_qproj",
    )(b_misc, g_cq, w_uq_r, tab)


def _kvproj(b_misc, g_ckv, w_ukv, tab, *, hpb, av, ckv_blk, kr_blk):
    lp = b_misc.shape[0]
    cw = w_ukv.shape[0]
    n = w_ukv.shape[1]
    heads = n // (NOPE + av)
    bn = hpb * (NOPE + av)
    bm = ATT_CHUNK
    return pl.pallas_call(
        functools.partial(_kvproj_body, hpb=hpb, av=av),
        grid=(lp // bm, n // bn),
        in_specs=[
            pl.BlockSpec((bm, cw), lambda i, j: (i, ckv_blk)),
            pl.BlockSpec((1, cw), lambda i, j: (0, 0)),
            pl.BlockSpec((cw, bn), lambda i, j: (0, j)),
            pl.BlockSpec((bm, ROPE_W), lambda i, j: (i, kr_blk)),
            pl.BlockSpec((bm, ROPE_W), lambda i, j: (i, 0)),
        ],
        out_specs=[
            pl.BlockSpec((bm, hpb * QK_W), lambda i, j: (i, j)),
            pl.BlockSpec((hpb, 1, av + ONES_ROWS, bm), lambda i, j: (j, i, 0, 0)),
        ],
        out_shape=[jax.ShapeDtypeStruct((lp, heads * QK_W), BF16),
                   jax.ShapeDtypeStruct((heads, lp // bm, av + ONES_ROWS, bm), BF16)],
        scratch_shapes=[pltpu.VMEM((bm, cw), BF16)],
        compiler_params=_params(("parallel", "arbitrary")),
        name="mla_kvproj",
    )(b_misc, g_ckv, w_ukv, b_misc, tab)


def _attn_body(qt_ref, k_ref, vt_ref, o_ref, m_sc, acc_sc, s_even, s_odd, c_even, c_odd):
    i = pl.program_id(1)
    n_slab = ATT_TILE // ATT_SLAB
    meta0 = T0 - N_META
    all_slabs = tuple((s, ATT_CHUNK) for s in range(n_slab))
    late_slabs = tuple((s, min(ATT_CHUNK, (s + 1) * ATT_SLAB - ATT_CHUNK)) for s in range(n_slab)
                       if (s + 1) * ATT_SLAB > ATT_CHUNK)

    av = o_ref.shape[1]
    m_sc[...] = jnp.full_like(m_sc, NEG_SCORE)
    acc_sc[...] = jnp.zeros_like(acc_sc)

    def scores(j, buf, masked, slabs):
        s_buf, c_buf = buf
        start = j * ATT_CHUNK
        if not isinstance(j, int):
            start = pl.multiple_of(start, ATT_CHUNK)
        for s, nkeys in slabs:
            k = k_ref[pl.ds(start, nkeys), :]
            qt = qt_ref[:, s * ATT_SLAB:(s + 1) * ATT_SLAB]
            st = jnp.dot(k, qt, preferred_element_type=F32)
            if masked:
                kpos = j * ATT_CHUNK + lax.broadcasted_iota(jnp.int32, (nkeys, ATT_SLAB), 0)
                qpos = (i * ATT_TILE + s * ATT_SLAB
                        + lax.broadcasted_iota(jnp.int32, (nkeys, ATT_SLAB), 1))
                keep = jnp.logical_and(kpos <= qpos, kpos >= meta0)
                st = jnp.where(keep, st, NEG_SCORE)
            s_buf[s, 0:nkeys, :] = st
            c_buf[s] = jnp.max(st, axis=0, keepdims=True)

    def absorb(j, buf, slabs):
        s_buf, c_buf = buf
        for s, nkeys in slabs:
            m_prev = m_sc[s]
            m_new = jnp.maximum(m_prev, c_buf[s])
            alpha = jnp.exp2(m_prev - m_new)
            p = jnp.exp2(s_buf[s, 0:nkeys, :] - m_new)
            acc_sc[s] = alpha * acc_sc[s] + jnp.dot(vt_ref[0, j, :, 0:nkeys], p.astype(BF16),
                                                    preferred_element_type=F32)
            m_sc[s] = m_new

    even = (s_even, c_even)
    odd = (s_odd, c_odd)

    def pair(p, mask_odd, slabs_odd, mask_next):
        scores(2 * p + 1, odd, mask_odd, slabs_odd)
        absorb(2 * p, even, all_slabs)
        if mask_next is not None:
            scores(2 * p + 2, even, mask_next, all_slabs)
        absorb(2 * p + 1, odd, slabs_odd)

    scores(0, even, True, all_slabs)

    @pl.when(i == 1)
    def _():
        pair(0, False, all_slabs, True)

    @pl.when(i > 1)
    def _():
        pair(0, False, all_slabs, False)

        def mid(p, carry):
            pair(p, False, all_slabs, False)
            return carry
        lax.fori_loop(1, i - 1, mid, 0)
        pair(i - 1, False, all_slabs, True)

    pair(i, True, late_slabs, None)

    for s in range(n_slab):
        out = (acc_sc[s, 0:av, :] / acc_sc[s, av:av + 1, :]).T
        o_ref[s * ATT_SLAB:(s + 1) * ATT_SLAB, :] = out.astype(o_ref.dtype)


def _attention(qt, k, vt, *, heads, av):
    lp = qt.shape[1]
    n_slab = ATT_TILE // ATT_SLAB
    assert lp % ATT_TILE == 0 and ATT_TILE % ATT_SLAB == 0 and ATT_TILE == 2 * ATT_CHUNK
    return pl.pallas_call(
        _attn_body,
        grid=(heads, lp // ATT_TILE),
        in_specs=[
            pl.BlockSpec((QK_W, ATT_TILE), lambda h, i: (h, i)),
            pl.BlockSpec((lp, QK_W), lambda h, i: (0, h)),
            pl.BlockSpec((1, lp // ATT_CHUNK, av + ONES_ROWS, ATT_CHUNK), lambda h, i: (h, 0, 0, 0)),
        ],
        out_specs=pl.BlockSpec((ATT_TILE, av), lambda h, i: (i, h)),
        out_shape=jax.ShapeDtypeStruct((lp, heads * av), BF16),
        scratch_shapes=[
            pltpu.VMEM((n_slab, 1, ATT_SLAB), F32),
            pltpu.VMEM((n_slab, av + ONES_ROWS, ATT_SLAB), F32),
            pltpu.VMEM((n_slab, ATT_CHUNK, ATT_SLAB), F32),
            pltpu.VMEM((n_slab, ATT_CHUNK, ATT_SLAB), F32),
            pltpu.VMEM((n_slab, 1, ATT_SLAB), F32),
            pltpu.VMEM((n_slab, 1, ATT_SLAB), F32),
        ],
        compiler_params=_params(("parallel", "parallel")),
        name="mla_attention",
    )(qt, k, vt)


def _cast_rows_body(x_ref, o_ref, *, n_src):
    @pl.when(pl.program_id(0) < n_src)
    def _():
        o_ref[...] = x_ref[...].astype(BF16)

    @pl.when(pl.program_id(0) >= n_src)
    def _():
        o_ref[...] = jnp.zeros_like(o_ref)


def _cast_rows(w, layer, rows_out=None):
    _, rows, cols = w.shape
    rows_out = rows if rows_out is None else rows_out
    rb = _div_tile(rows, CAST_ROWS, 8)
    assert rows_out % rb == 0
    n_src = rows // rb
    return pl.pallas_call(
        functools.partial(_cast_rows_body, n_src=n_src),
        grid=(rows_out // rb,),
        in_specs=[pl.BlockSpec((None, rb, cols), lambda i: (layer, jnp.minimum(i, n_src - 1), 0))],
        out_specs=pl.BlockSpec((rb, cols), lambda i: (i, 0)),
        out_shape=jax.ShapeDtypeStruct((rows_out, cols), BF16),
        compiler_params=_params(("parallel",)),
        name="cast_bf16",
    )(w)


def _cast_transposed_body(x_ref, o_ref):
    o_ref[...] = x_ref[...].T.astype(BF16)


def _cast_transposed(w_t, layer):
    _, cols, rows = w_t.shape
    cb = 2 * CAST_ROWS
    return pl.pallas_call(
        _cast_transposed_body,
        grid=(pl.cdiv(cols, cb),),
        in_specs=[pl.BlockSpec((None, cb, rows), lambda i: (layer, i, 0))],
        out_specs=pl.BlockSpec((rows, cb), lambda i: (0, i)),
        out_shape=jax.ShapeDtypeStruct((rows, cols), BF16),
        compiler_params=_params(("parallel",)),
        name="cast_transposed",
    )(w_t)


def _cast_gate_up_body(x_ref, g_ref, u_ref, *, ff):
    pad = g_ref.shape[1] - ff
    g_ref[:, :ff] = x_ref[:, :ff].astype(BF16)
    u_ref[:, :ff] = x_ref[:, ff:].astype(BF16)
    if pad:
        g_ref[:, ff:] = jnp.zeros((g_ref.shape[0], pad), BF16)
        u_ref[:, ff:] = jnp.zeros((u_ref.shape[0], pad), BF16)


def _cast_gate_up(w_gu, layer, ffp):
    _, d, ff2 = w_gu.shape
    ff = ff2 // 2
    rb = _div_tile(d, CAST_ROWS // 2, 8)
    out = jax.ShapeDtypeStruct((d, ffp), BF16)
    return pl.pallas_call(
        functools.partial(_cast_gate_up_body, ff=ff),
        grid=(d // rb,),
        in_specs=[pl.BlockSpec((None, rb, ff2), lambda i: (layer, i, 0))],
        out_specs=[pl.BlockSpec((rb, ffp), lambda i: (i, 0))] * 2,
        out_shape=[out, out],
        compiler_params=_params(("parallel",)),
        name="cast_gate_up",
    )(w_gu)


def _swap_half(w):
    half = w.shape[-1] // 2
    return jnp.concatenate([-w[..., half:], w[..., :half]], axis=-1)


def _prep_layer_weights(layer, w_in_all, w_uq_all, w_ukv_all, w_out_all, w_gu_all, w_down_all, *, mw, hk, ffp):
    w_uq = _cast_rows(w_uq_all, layer)
    d = w_in_all.shape[1]
    o_gate = 2 * hk + 2 * mw
    o_cq = o_gate + 2 * M_HEADS
    o_ckv = o_cq + Q_LORA
    o_kr = o_ckv + KV_LORA
    w_main = _cast_transposed(jnp.swapaxes(w_in_all, 1, 2), layer)
    w_tail = w_main[:, o_gate:]
    wkr = w_tail[:, o_kr - o_gate:o_kr - o_gate + ROPE]
    gate_pad = jnp.zeros((d, GATE_W - 2 * M_HEADS), w_tail.dtype)
    w_b = jnp.concatenate([w_tail[:, o_cq - o_gate:o_kr - o_gate], w_tail[:, :o_cq - o_gate], gate_pad, wkr,
                           _swap_half(wkr)], axis=1)

    uq = w_uq.reshape(Q_LORA, A_HEADS, NOPE + ROPE)
    uq_r = jnp.concatenate([uq[..., :NOPE], uq[..., NOPE:], _swap_half(uq[..., NOPE:])], axis=-1)
    uq_r = uq_r.reshape(Q_LORA, A_HEADS * QK_W)

    w_gate, w_up = _cast_gate_up(w_gu_all, layer, ffp)
    return dict(w_in=w_main, w_b=w_b, uq=uq_r, ukv=_cast_rows(w_ukv_all, layer),
                w_out=_cast_rows(w_out_all, layer), w_gate=w_gate, w_up=w_up,
                w_down=_cast_rows(w_down_all, layer, ffp))


def _rope_table(lp):
    meta0 = T0 - N_META
    pos = jnp.maximum(jnp.arange(lp, dtype=jnp.int32) - meta0, 0).astype(F32)
    inv_freq = ROPE_THETA ** (-jnp.arange(ROPE // 2, dtype=F32) / (ROPE // 2))
    ang = pos[:, None] * inv_freq[None, :]
    cos, sin = jnp.cos(ang), jnp.sin(ang)
    return jnp.concatenate([cos, cos, sin, sin], axis=-1)


def kernel(x, meta, g_mix_pre, w_in, conv_w, b_gates, g_mnorm, g_cq, w_uq, g_ckv, w_ukv, w_out,
           g_mix_post, g_ffn_pre, w_gu, w_down, g_ffn_post):
    batch, seq, d = x.shape
    assert batch == 1 and seq % T0 == 0
    depth = w_in.shape[0]
    lp = T0 + seq
    mw = d // 2
    dv = mw // M_HEADS
    dk = dv // 2
    hk = M_HEADS * dk
    aw = d - mw
    av = aw // A_HEADS
    ff = w_down.shape[1]
    assert Q_LORA % KV_LORA == 0 and KV_LORA % ROPE_W == 0 and ROPE_W == GATE_W and 2 * hk == mw

    bf = 512
    ffp = -(-ff // 1024) * 1024
    bm = _div_tile(lp, 1664, 128)
    bn_in = _div_tile(mw, 512, 128)
    scale = (NOPE + ROPE) ** -0.5 * LOG2_E

    ckv_blk = Q_LORA // KV_LORA
    gate_blk = (Q_LORA + KV_LORA) // GATE_W
    kr_blk = (Q_LORA + KV_LORA + GATE_W) // ROPE_W
    nb = Q_LORA + KV_LORA + GATE_W + ROPE_W

    tab = _rope_table(lp)
    h, u = _prep(x[0], meta, g_mix_pre[0][None])

    out = None
    for l in range(depth):
        w = _prep_layer_weights(l, w_in, w_uq, w_ukv, w_out, w_gu, w_down, mw=mw, hk=hk, ffp=ffp)
        a_qk = _matmul([u], [(w["w_in"], 0, 0)], mw, F32, bm=bm, bn=bn_in, name="inproj_qk")
        v_m = _matmul([u], [(w["w_in"], 0, mw // bn_in)], mw, BF16, bm=bm, bn=bn_in, name="inproj_v")
        a_o = _matmul([u], [(w["w_in"], 0, 2 * mw // bn_in)], mw, F32, bm=bm, bn=bn_in, name="inproj_o")
        b_misc = _matmul([u], [(w["w_b"], 0, 0)], nb, F32, bm=bm, bn=_div_tile(nb, 768, 128),
                         name="inproj_misc")

        gates_t = b_misc[:, Q_LORA + KV_LORA:Q_LORA + KV_LORA + 2 * M_HEADS].T
        bias_c = jnp.pad(b_gates[l], (0, GATE_W - 2 * M_HEADS))[None, :]
        bias_r = b_gates[l][:, None]
        h_m = _mlstm(a_qk, v_m, a_o, b_misc, gates_t, conv_w[l], bias_c, bias_r, g_mnorm[l][None],
                     heads=M_HEADS, dk=dk, dv=dv, gate_blk=gate_blk)

        q_a = _qproj(b_misc, g_cq[l][None], w["uq"], tab, bm=_div_tile(lp, 832, 64), hpb=4, scale=scale)
        k_a, vt_a = _kvproj(b_misc, g_ckv[l][None], w["ukv"], tab, hpb=4, av=av,
                            ckv_blk=ckv_blk, kr_blk=kr_blk)
        h_a = _attention(q_a, k_a, vt_a, heads=A_HEADS, av=av)

        mix = _matmul([h_m, h_a], [(w["w_out"], 0, 0), (w["w_out"], 1, 0)], d, F32, bm=bm,
                      bn=_div_tile(d, 512, 128), name="outproj")
        h, u = _resnorm(h, mix, g_mix_post[l][None], g_ffn_pre[l][None])

        act = _matmul([u], [(w["w_gate"], 0, 0), (w["w_up"], 0, 0)], ffp, BF16,
                      bm=bm, bn=bf, swiglu=True, name="ffn_up")
        y = _matmul([act], [(w["w_down"], 0, 0)], d, F32, bm=bm, bn=_div_tile(d, 1024, 128),
                    bk=_div_tile(ffp, 2816, 128), name="ffn_down")
        if l + 1 < depth:
            h, u = _resnorm(h, y, g_ffn_post[l][None], g_mix_pre[l + 1][None])
        else:
            out = _resnorm_final(h, y, g_ffn_post[l][None])
    return out[None]
```

```python
import functools

import jax
import jax.numpy as jnp
from jax import lax
from jax.experimental import pallas as pl
from jax.experimental.pallas import tpu as pltpu

N_META = 16
M_HEADS = 4
CONV_W = 4
A_HEADS = 16
NOPE = 128
ROPE = 64
Q_LORA = 1536
KV_LORA = 512
ROPE_THETA = 10000.0
NORM_EPS = 1e-6
NEG_SCORE = -1e30
LOG2_E = 1.4426950408889634

T0 = 512
MLSTM_CHUNK = 256
ROW_BLOCK = 256
HALO = 8
QK_W = NOPE + 2 * ROPE
GATE_W = 128
ROPE_W = 2 * ROPE

VMEM_LIMIT_BYTES = 56 * 1024 * 1024
MM_SUB_ROWS = 512
CAST_ROWS = 128
ONES_ROWS = 16
ATT_SLAB = 512
ATT_CHUNK = 768
ATT_TILE = 2 * ATT_CHUNK

F32 = jnp.float32
BF16 = jnp.bfloat16


def _div_tile(n, target, mult):
    best = None
    t = mult
    while t <= min(n, target):
        if n % t == 0:
            best = t
        t += mult
    if best is None:
        raise ValueError(f"no tile for {n} (target {target}, multiple of {mult})")
    return best


def _params(sem):
    return pltpu.CompilerParams(dimension_semantics=sem, vmem_limit_bytes=VMEM_LIMIT_BYTES)


def _rms(t, g):
    return t * lax.rsqrt(jnp.mean(t * t, axis=-1, keepdims=True) + NORM_EPS) * g


def _row_loop(nrows, sub, fn):
    def step(r, carry):
        fn(pl.ds(pl.multiple_of(r * sub, sub), sub))
        return carry
    lax.fori_loop(0, nrows // sub, step, 0, unroll=True)


def _prep_body(x_ref, meta_ref, g_ref, h_ref, u_ref):
    i = pl.program_id(0)

    @pl.when(i == 0)
    def _():
        h_ref[...] = jnp.zeros_like(h_ref)
        h_ref[T0 - N_META:T0, :] = meta_ref[...]

    @pl.when(i > 0)
    def _():
        h_ref[...] = x_ref[...]

    u_ref[...] = _rms(h_ref[...], g_ref[...]).astype(BF16)


def _prep(x2d, meta, g):
    seq, d = x2d.shape
    lp = T0 + seq
    return pl.pallas_call(
        _prep_body,
        grid=(lp // T0,),
        in_specs=[
            pl.BlockSpec((T0, d), lambda i: (jnp.maximum(i - 1, 0), 0)),
            pl.BlockSpec((N_META, d), lambda i: (0, 0)),
            pl.BlockSpec((1, d), lambda i: (0, 0)),
        ],
        out_specs=[
            pl.BlockSpec((T0, d), lambda i: (i, 0)),
            pl.BlockSpec((T0, d), lambda i: (i, 0)),
        ],
        out_shape=[jax.ShapeDtypeStruct((lp, d), F32), jax.ShapeDtypeStruct((lp, d), BF16)],
        compiler_params=_params(("parallel",)),
        name="prep_norm",
    )(x2d, meta, g)


def _mm_body(*refs, n_lhs, nk, swiglu, bm, sub):
    lhs = refs[0:n_lhs]
    rhs = refs[n_lhs:-1]
    o_ref = refs[-1]

    def product(rows, b):
        acc = None
        for a, w in zip(lhs, b):
            d = jnp.dot(a[rows, :], w[...], preferred_element_type=F32)
            acc = d if acc is None else acc + d
        return acc

    def finish(rows):
        if swiglu:
            gate = product(rows, rhs[0:1])
            acc = gate * jax.nn.sigmoid(gate) * product(rows, rhs[1:2])
        else:
            acc = product(rows, rhs)
        o_ref[rows, :] = acc.astype(o_ref.dtype)

    if nk == 1:
        _row_loop(bm, sub, finish)
        return

    k = pl.program_id(2)

    def first(rows):
        o_ref[rows, :] = product(rows, rhs)

    def accumulate(rows):
        o_ref[rows, :] += product(rows, rhs)

    @pl.when(k == 0)
    def _():
        _row_loop(bm, sub, first)

    @pl.when(k > 0)
    def _():
        _row_loop(bm, sub, accumulate)


def _matmul(lhs, rhs, n, out_dtype, *, bm, bn, bk=None, swiglu=False, name):
    m = lhs[0].shape[0]
    if bk is None:
        nk = 1
    else:
        assert len(lhs) == 1 and not swiglu and out_dtype == F32
        nk = lhs[0].shape[1] // bk
    in_specs = []
    for a in lhs:
        kk = a.shape[1] if bk is None else bk
        in_specs.append(pl.BlockSpec((bm, kk), lambda i, j, k: (i, k)))
    for p, (w, rb, cb) in enumerate(rhs):
        a = lhs[0] if swiglu else lhs[p]
        kk = a.shape[1] if bk is None else bk
        in_specs.append(pl.BlockSpec((kk, bn), lambda i, j, k, rb=rb, cb=cb: (k + rb, j + cb)))
    return pl.pallas_call(
        functools.partial(_mm_body, n_lhs=len(lhs), nk=nk, swiglu=swiglu, bm=bm,
                          sub=_div_tile(bm, MM_SUB_ROWS, 16)),
        grid=(m // bm, n // bn, nk),
        in_specs=in_specs,
        out_specs=pl.BlockSpec((bm, bn), lambda i, j, k: (i, j)),
        out_shape=jax.ShapeDtypeStruct((m, n), out_dtype),
        compiler_params=_params(("parallel", "parallel", "arbitrary")),
        name=name,
    )(*lhs, *[w for w, _, _ in rhs])


def _resnorm_body(x_ref, y_ref, gp_ref, gn_ref, xo_ref, u_ref, *, rows):
    i = pl.program_id(0)
    row = i * rows + lax.broadcasted_iota(jnp.int32, (rows, 1), 0)
    xn = x_ref[...] + _rms(y_ref[...], gp_ref[...])
    xn = jnp.where(row >= T0 - N_META, xn, 0.0)
    xo_ref[...] = xn
    u_ref[...] = _rms(xn, gn_ref[...]).astype(BF16)


def _resnorm(x, y, g_post, g_next):
    lp, d = x.shape
    rows = ROW_BLOCK
    blk = pl.BlockSpec((rows, d), lambda i: (i, 0))
    vec = pl.BlockSpec((1, d), lambda i: (0, 0))
    return pl.pallas_call(
        functools.partial(_resnorm_body, rows=rows),
        grid=(lp // rows,),
        in_specs=[blk, blk, vec, vec],
        out_specs=[blk, blk],
        out_shape=[jax.ShapeDtypeStruct((lp, d), F32), jax.ShapeDtypeStruct((lp, d), BF16)],
        compiler_params=_params(("parallel",)),
        name="resnorm",
    )(x, y, g_post, g_next)


def _resnorm_final_body(x_ref, y_ref, gp_ref, o_ref):
    o_ref[...] = x_ref[...] + _rms(y_ref[...], gp_ref[...])


def _resnorm_final(x, y, g_post):
    lp, d = x.shape
    rows = ROW_BLOCK
    skip = T0 // rows
    blk_in = pl.BlockSpec((rows, d), lambda i: (i + skip, 0))
    return pl.pallas_call(
        _resnorm_final_body,
        grid=((lp - T0) // rows,),
        in_specs=[blk_in, blk_in, pl.BlockSpec((1, d), lambda i: (0, 0))],
        out_specs=pl.BlockSpec((rows, d), lambda i: (i, 0)),
        out_shape=jax.ShapeDtypeStruct((lp - T0, d), F32),
        compiler_params=_params(("parallel",)),
        name="resnorm_final",
    )(x, y, g_post)


def _log_sigmoid(x):
    return jnp.minimum(x, 0.0) - jnp.log1p(jnp.exp(-jnp.abs(x)))


def _mlstm_body(q_ref, k_ref, v_ref, o_ref, gc_ref, gr_ref, cw_ref, bc_ref, br_ref, gn_ref,
                out_ref, xext, c_sc, n_sc, m_sc, *, heads, dk, dv, lc):
    c = pl.program_id(0)
    hk = heads * dk
    meta0 = T0 - N_META

    @pl.when(c == 0)
    def _():
        xext[0:HALO, :] = jnp.zeros((HALO, 2 * hk), F32)
        c_sc[...] = jnp.zeros_like(c_sc)
        n_sc[...] = jnp.zeros_like(n_sc)
        m_sc[...] = jnp.zeros_like(m_sc)

    xext[HALO:HALO + lc, 0:hk] = q_ref[...]
    xext[HALO:HALO + lc, hk:2 * hk] = k_ref[...]

    def conv_silu(lo, hi):
        conv = None
        for j in range(CONV_W):
            term = cw_ref[j:j + 1, lo:hi] * xext[pl.ds(HALO - (CONV_W - 1) + j, lc), lo:hi]
            conv = term if conv is None else conv + term
        return conv * jax.nn.sigmoid(conv)

    row = c * lc + lax.broadcasted_iota(jnp.int32, (lc, 1), 0)
    col = c * lc + lax.broadcasted_iota(jnp.int32, (1, lc), 1)
    valid_c = row >= meta0
    valid_r = col >= meta0
    gc = gc_ref[...] + bc_ref[...]
    gr = gr_ref[...] + br_ref[...]
    tt = lax.broadcasted_iota(jnp.int32, (lc, lc), 0)
    ss = lax.broadcasted_iota(jnp.int32, (lc, lc), 1)
    tril = ss <= tt
    triu = tt <= ss
    neg_inf = -jnp.inf

    for h in range(heads):
        li_c = jnp.where(valid_c, gc[:, h:h + 1], neg_inf)
        lf_c = jnp.where(valid_c, _log_sigmoid(gc[:, heads + h:heads + h + 1]), 0.0)
        li_r = jnp.where(valid_r, gr[h:h + 1, :], neg_inf)
        lf_r = jnp.where(valid_r, _log_sigmoid(gr[heads + h:heads + h + 1, :]), 0.0)
        b_c = jnp.sum(jnp.where(tril, lf_r, 0.0), axis=1, keepdims=True)
        b_r = jnp.sum(jnp.where(triu, lf_c, 0.0), axis=0, keepdims=True)
        g = jnp.sum(lf_r, axis=1, keepdims=True)
        m = m_sc[h:h + 1, 0:1]

        d = jnp.where(tril, b_c - b_r + li_r, neg_inf)
        inter = b_c + m
        m_t = jnp.maximum(inter, jnp.max(d, axis=1, keepdims=True))
        w_inter = jnp.exp(inter - m_t)
        p = jnp.exp(d - m_t)

        qh = conv_silu(h * dk, (h + 1) * dk) * (dk ** -0.5)
        kh = conv_silu(hk + h * dk, hk + (h + 1) * dk)
        qb = qh.astype(BF16)
        s = lax.dot_general(qb, kh.astype(BF16), (((1,), (1,)), ((), ())),
                            preferred_element_type=F32) * p
        vh = v_ref[:, h * dv:(h + 1) * dv]
        ch = c_sc[h]
        nh = n_sc[h:h + 1, :]
        num = (w_inter * jnp.dot(qb, ch.astype(BF16), preferred_element_type=F32)
               + jnp.dot(s.astype(BF16), vh, preferred_element_type=F32))
        den = (w_inter * jnp.sum(qh * nh, axis=1, keepdims=True)
               + jnp.sum(s, axis=1, keepdims=True))
        hh = num / jnp.maximum(jnp.abs(den), jnp.exp(-m_t))

        a_c = g - b_c + li_c
        m_new = jnp.maximum(g + m, jnp.max(a_c, axis=0, keepdims=True))
        decay = jnp.exp(g + m - m_new)
        wk = kh * jnp.exp(a_c - m_new)
        c_sc[h] = decay * ch + lax.dot_general(wk.astype(BF16), vh, (((0,), (0,)), ((), ())),
                                               preferred_element_type=F32)
        n_sc[h:h + 1, :] = decay * nh + jnp.sum(wk, axis=0, keepdims=True)
        m_sc[h:h + 1, :] = jnp.broadcast_to(m_new, (1, m_sc.shape[1]))

        hn = _rms(hh, gn_ref[:, h * dv:(h + 1) * dv])
        og = jax.nn.sigmoid(o_ref[:, h * dv:(h + 1) * dv])
        out_ref[:, h * dv:(h + 1) * dv] = (og * hn).astype(BF16)

    xext[0:HALO, :] = xext[lc:lc + HALO, :]


def _mlstm(a_qk, v, a_o, b_misc, gates_t, conv_w, bias_c, bias_r, g_mnorm, *, heads, dk, dv, gate_blk):
    lp = v.shape[0]
    lc = MLSTM_CHUNK
    hk = heads * dk
    mw = heads * dv
    assert 2 * heads <= 8
    return pl.pallas_call(
        functools.partial(_mlstm_body, heads=heads, dk=dk, dv=dv, lc=lc),
        grid=(lp // lc,),
        in_specs=[
            pl.BlockSpec((lc, hk), lambda c: (c, 0)),
            pl.BlockSpec((lc, hk), lambda c: (c, 1)),
            pl.BlockSpec((lc, mw), lambda c: (c, 0)),
            pl.BlockSpec((lc, mw), lambda c: (c, 0)),
            pl.BlockSpec((lc, GATE_W), lambda c: (c, gate_blk)),
            pl.BlockSpec((2 * heads, lc), lambda c: (0, c)),
            pl.BlockSpec((CONV_W, 2 * hk), lambda c: (0, 0)),
            pl.BlockSpec((1, GATE_W), lambda c: (0, 0)),
            pl.BlockSpec((2 * heads, 1), lambda c: (0, 0)),
            pl.BlockSpec((1, mw), lambda c: (0, 0)),
        ],
        out_specs=pl.BlockSpec((lc, mw), lambda c: (c, 0)),
        out_shape=jax.ShapeDtypeStruct((lp, mw), BF16),
        scratch_shapes=[
            pltpu.VMEM((lc + HALO, 2 * hk), F32),
            pltpu.VMEM((heads, dk, dv), F32),
            pltpu.VMEM((8, dk), F32),
            pltpu.VMEM((8, 128), F32),
        ],
        compiler_params=_params(("arbitrary",)),
        name="mlstm",
    )(a_qk, a_qk, v, a_o, b_misc, gates_t, conv_w, bias_c, bias_r, g_mnorm)


def _rope128(t, tab):
    pr = t * tab
    rr = pr + pltpu.roll(pr, ROPE, axis=1)
    lane = lax.broadcasted_iota(jnp.int32, pr.shape, 1)
    return jnp.where(lane < ROPE, rr, 0.0)


def _qproj_body(c_ref, g_ref, w_ref, tab_ref, o_ref, cn, *, hpb, scale, bm, sub):
    @pl.when(pl.program_id(1) == 0)
    def _():
        def norm(rows):
            cn[rows, :] = _rms(c_ref[rows, :], g_ref[...]).astype(BF16)
        _row_loop(bm, sub, norm)

    for r in range(bm // sub):
        lo, hi = r * sub, (r + 1) * sub
        acc = jnp.dot(cn[lo:hi, :], w_ref[...], preferred_element_type=F32)
        tab = tab_ref[lo:hi, :]
        for hh in range(hpb):
            base = hh * QK_W
            o_ref[base:base + NOPE, lo:hi] = (acc[:, base:base + NOPE] * scale).T.astype(BF16)
            rr = _rope128(acc[:, base + NOPE:base + QK_W], tab)
            o_ref[base + NOPE:base + QK_W, lo:hi] = (rr * scale).T.astype(BF16)


def _kvproj_body(c_ref, g_ref, w_ref, kr_ref, tab_ref, k_ref, vt_ref, cn, *, hpb, av):
    @pl.when(pl.program_id(1) == 0)
    def _():
        cn[...] = _rms(c_ref[...], g_ref[...]).astype(BF16)

    acc = jnp.dot(cn[...], w_ref[...], preferred_element_type=F32)
    kr = _rope128(kr_ref[...], tab_ref[...]).astype(BF16)
    for hh in range(hpb):
        src = hh * (NOPE + av)
        k_ref[:, hh * QK_W:hh * QK_W + NOPE] = acc[:, src:src + NOPE].astype(BF16)
        k_ref[:, hh * QK_W + NOPE:(hh + 1) * QK_W] = kr
        vt_ref[hh, 0, 0:av, :] = acc[:, src + NOPE:src + NOPE + av].T.astype(BF16)
        vt_ref[hh, 0, av:av + ONES_ROWS, :] = jnp.ones((ONES_ROWS, vt_ref.shape[3]), BF16)


def _qproj(b_misc, g_cq, w_uq_r, tab, *, bm, hpb, scale):
    lp = b_misc.shape[0]
    cw = w_uq_r.shape[0]
    n = w_uq_r.shape[1]
    bn = hpb * QK_W
    return pl.pallas_call(
        functools.partial(_qproj_body, hpb=hpb, scale=scale, bm=bm, sub=_div_tile(bm, MM_SUB_ROWS, 16)),
        grid=(lp // bm, n // bn),
        in_specs=[
            pl.BlockSpec((bm, cw), lambda i, j: (i, 0)),
            pl.BlockSpec((1, cw), lambda i, j: (0, 0)),
            pl.BlockSpec((cw, bn), lambda i, j: (0, j)),
            pl.BlockSpec((bm, ROPE_W), lambda i, j: (i, 0)),
        ],
        out_specs=pl.BlockSpec((bn, bm), lambda i, j: (j, i)),
        out_shape=jax.ShapeDtypeStruct((n, lp), BF16),
        scratch_shapes=[pltpu.VMEM((bm, cw), BF16)],
        compiler_params=_params(("parallel", "arbitrary")),
        name="mla_qproj",
    )(b_misc, g_cq, w_uq_r, tab)


def _kvproj(b_misc, g_ckv, w_ukv, tab, *, hpb, av, ckv_blk, kr_blk):
    lp = b_misc.shape[0]
    cw = w_ukv.shape[0]
    n = w_ukv.shape[1]
    heads = n // (NOPE + av)
    bn = hpb * (NOPE + av)
    bm = ATT_CHUNK
    return pl.pallas_call(
        functools.partial(_kvproj_body, hpb=hpb, av=av),
        grid=(lp // bm, n // bn),
        in_specs=[
            pl.BlockSpec((bm, cw), lambda i, j: (i, ckv_blk)),
            pl.BlockSpec((1, cw), lambda i, j: (0, 0)),
            pl.BlockSpec((cw, bn), lambda i, j: (0, j)),
            pl.BlockSpec((bm, ROPE_W), lambda i, j: (i, kr_blk)),
            pl.BlockSpec((bm, ROPE_W), lambda i, j: (i, 0)),
        ],
        out_specs=[
            pl.BlockSpec((bm, hpb * QK_W), lambda i, j: (i, j)),
            pl.BlockSpec((hpb, 1, av + ONES_ROWS, bm), lambda i, j: (j, i, 0, 0)),
        ],
        out_shape=[jax.ShapeDtypeStruct((lp, heads * QK_W), BF16),
                   jax.ShapeDtypeStruct((heads, lp // bm, av + ONES_ROWS, bm), BF16)],
        scratch_shapes=[pltpu.VMEM((bm, cw), BF16)],
        compiler_params=_params(("parallel", "arbitrary")),
        name="mla_kvproj",
    )(b_misc, g_ckv, w_ukv, b_misc, tab)


def _attn_body(qt_ref, k_ref, vt_ref, o_ref, m_sc, acc_sc, s_even, s_odd, c_even, c_odd):
    i = pl.program_id(1)
    n_slab = ATT_TILE // ATT_SLAB
    meta0 = T0 - N_META
    all_slabs = tuple((s, ATT_CHUNK) for s in range(n_slab))
    late_slabs = tuple((s, min(ATT_CHUNK, (s + 1) * ATT_SLAB - ATT_CHUNK)) for s in range(n_slab)
                       if (s + 1) * ATT_SLAB > ATT_CHUNK)

    av = o_ref.shape[1]
    m_sc[...] = jnp.full_like(m_sc, NEG_SCORE)
    acc_sc[...] = jnp.zeros_like(acc_sc)

    def scores(j, buf, masked, slabs):
        s_buf, c_buf = buf
        start = j * ATT_CHUNK
        if not isinstance(j, int):
            start = pl.multiple_of(start, ATT_CHUNK)
        for s, nkeys in slabs:
            k = k_ref[pl.ds(start, nkeys), :]
            qt = qt_ref[:, s * ATT_SLAB:(s + 1) * ATT_SLAB]
            st = jnp.dot(k, qt, preferred_element_type=F32)
            if masked:
                kpos = j * ATT_CHUNK + lax.broadcasted_iota(jnp.int32, (nkeys, ATT_SLAB), 0)
                qpos = (i * ATT_TILE + s * ATT_SLAB
                        + lax.broadcasted_iota(jnp.int32, (nkeys, ATT_SLAB), 1))
                keep = jnp.logical_and(kpos <= qpos, kpos >= meta0)
                st = jnp.where(keep, st, NEG_SCORE)
            s_buf[s, 0:nkeys, :] = st
            c_buf[s] = jnp.max(st, axis=0, keepdims=True)

    def absorb(j, buf, slabs):
        s_buf, c_buf = buf
        for s, nkeys in slabs:
            m_prev = m_sc[s]
            m_new = jnp.maximum(m_prev, c_buf[s])
            alpha = jnp.exp2(m_prev - m_new)
            p = jnp.exp2(s_buf[s, 0:nkeys, :] - m_new)
            acc_sc[s] = alpha * acc_sc[s] + jnp.dot(vt_ref[0, j, :, 0:nkeys], p.astype(BF16),
                                                    preferred_element_type=F32)
            m_sc[s] = m_new

    even = (s_even, c_even)
    odd = (s_odd, c_odd)

    def pair(p, mask_odd, slabs_odd, mask_next):
        scores(2 * p + 1, odd, mask_odd, slabs_odd)
        absorb(2 * p, even, all_slabs)
        if mask_next is not None:
            scores(2 * p + 2, even, mask_next, all_slabs)
        absorb(2 * p + 1, odd, slabs_odd)

    scores(0, even, True, all_slabs)

    @pl.when(i == 1)
    def _():
        pair(0, False, all_slabs, True)

    @pl.when(i > 1)
    def _():
        pair(0, False, all_slabs, False)

        def mid(p, carry):
            pair(p, False, all_slabs, False)
            return carry
        lax.fori_loop(1, i - 1, mid, 0)
        pair(i - 1, False, all_slabs, True)

    pair(i, True, late_slabs, None)

    for s in range(n_slab):
        out = (acc_sc[s, 0:av, :] / acc_sc[s, av:av + 1, :]).T
        o_ref[s * ATT_SLAB:(s + 1) * ATT_SLAB, :] = out.astype(o_ref.dtype)


def _attention(qt, k, vt, *, heads, av):
    lp = qt.shape[1]
    n_slab = ATT_TILE // ATT_SLAB
    assert lp % ATT_TILE == 0 and ATT_TILE % ATT_SLAB == 0 and ATT_TILE == 2 * ATT_CHUNK
    return pl.pallas_call(
        _attn_body,
        grid=(heads, lp // ATT_TILE),
        in_specs=[
            pl.BlockSpec((QK_W, ATT_TILE), lambda h, i: (h, i)),
            pl.BlockSpec((lp, QK_W), lambda h, i: (0, h)),
            pl.BlockSpec((1, lp // ATT_CHUNK, av + ONES_ROWS, ATT_CHUNK), lambda h, i: (h, 0, 0, 0)),
        ],
        out_specs=pl.BlockSpec((ATT_TILE, av), lambda h, i: (i, h)),
        out_shape=jax.ShapeDtypeStruct((lp, heads * av), BF16),
        scratch_shapes=[
            pltpu.VMEM((n_slab, 1, ATT_SLAB), F32),
            pltpu.VMEM((n_slab, av + ONES_ROWS, ATT_SLAB), F32),
            pltpu.VMEM((n_slab, ATT_CHUNK, ATT_SLAB), F32),
            pltpu.VMEM((n_slab, ATT_CHUNK, ATT_SLAB), F32),
            pltpu.VMEM((n_slab, 1, ATT_SLAB), F32),
            pltpu.VMEM((n_slab, 1, ATT_SLAB), F32),
        ],
        compiler_params=_params(("parallel", "parallel")),
        name="mla_attention",
    )(qt, k, vt)


def _cast_rows_body(x_ref, o_ref, *, n_src):
    @pl.when(pl.program_id(0) < n_src)
    def _():
        o_ref[...] = x_ref[...].astype(BF16)

    @pl.when(pl.program_id(0) >= n_src)
    def _():
        o_ref[...] = jnp.zeros_like(o_ref)


def _cast_rows(w, layer, rows_out=None):
    _, rows, cols = w.shape
    rows_out = rows if rows_out is None else rows_out
    rb = _div_tile(rows, CAST_ROWS, 8)
    assert rows_out % rb == 0
    n_src = rows // rb
    return pl.pallas_call(
        functools.partial(_cast_rows_body, n_src=n_src),
        grid=(rows_out // rb,),
        in_specs=[pl.BlockSpec((None, rb, cols), lambda i: (layer, jnp.minimum(i, n_src - 1), 0))],
        out_specs=pl.BlockSpec((rb, cols), lambda i: (i, 0)),
        out_shape=jax.ShapeDtypeStruct((rows_out, cols), BF16),
        compiler_params=_params(("parallel",)),
        name="cast_bf16",
    )(w)


def _cast_transposed_body(x_ref, o_ref):
    o_ref[...] = x_ref[...].T.astype(BF16)


def _cast_transposed(w_t, layer):
    _, cols, rows = w_t.shape
    cb = 2 * CAST_ROWS
    return pl.pallas_call(
        _cast_transposed_body,
        grid=(pl.cdiv(cols, cb),),
        in_specs=[pl.BlockSpec((None, cb, rows), lambda i: (layer, i, 0))],
        out_specs=pl.BlockSpec((rows, cb), lambda i: (0, i)),
        out_shape=jax.ShapeDtypeStruct((rows, cols), BF16),
        compiler_params=_params(("parallel",)),
        name="cast_transposed",
    )(w_t)


def _cast_gate_up_body(x_ref, g_ref, u_ref, *, ff):
    pad = g_ref.shape[1] - ff
    g_ref[:, :ff] = x_ref[:, :ff].astype(BF16)
    u_ref[:, :ff] = x_ref[:, ff:].astype(BF16)
    if pad:
        g_ref[:, ff:] = jnp.zeros((g_ref.shape[0], pad), BF16)
        u_ref[:, ff:] = jnp.zeros((u_ref.shape[0], pad), BF16)


def _cast_gate_up(w_gu, layer, ffp):
    _, d, ff2 = w_gu.shape
    ff = ff2 // 2
    rb = _div_tile(d, CAST_ROWS // 2, 8)
    out = jax.ShapeDtypeStruct((d, ffp), BF16)
    return pl.pallas_call(
        functools.partial(_cast_gate_up_body, ff=ff),
        grid=(d // rb,),
        in_specs=[pl.BlockSpec((None, rb, ff2), lambda i: (layer, i, 0))],
        out_specs=[pl.BlockSpec((rb, ffp), lambda i: (i, 0))] * 2,
        out_shape=[out, out],
        compiler_params=_params(("parallel",)),
        name="cast_gate_up",
    )(w_gu)


def _swap_half(w):
    half = w.shape[-1] // 2
    return jnp.concatenate([-w[..., half:], w[..., :half]], axis=-1)


def _prep_layer_weights(layer, w_in_all, w_uq_all, w_ukv_all, w_out_all, w_gu_all, w_down_all, *, mw, hk, ffp):
    w_uq = _cast_rows(w_uq_all, layer)
    d = w_in_all.shape[1]
    o_gate = 2 * hk + 2 * mw
    o_cq = o_gate + 2 * M_HEADS
    o_ckv = o_cq + Q_LORA
    o_kr = o_ckv + KV_LORA
    w_main = _cast_transposed(jnp.swapaxes(w_in_all, 1, 2), layer)
    w_tail = w_main[:, o_gate:]
    wkr = w_tail[:, o_kr - o_gate:o_kr - o_gate + ROPE]
    gate_pad = jnp.zeros((d, GATE_W - 2 * M_HEADS), w_tail.dtype)
    w_b = jnp.concatenate([w_tail[:, o_cq - o_gate:o_kr - o_gate], w_tail[:, :o_cq - o_gate], gate_pad, wkr,
                           _swap_half(wkr)], axis=1)

    uq = w_uq.reshape(Q_LORA, A_HEADS, NOPE + ROPE)
    uq_r = jnp.concatenate([uq[..., :NOPE], uq[..., NOPE:], _swap_half(uq[..., NOPE:])], axis=-1)
    uq_r = uq_r.reshape(Q_LORA, A_HEADS * QK_W)

    w_gate, w_up = _cast_gate_up(w_gu_all, layer, ffp)
    return dict(w_in=w_main, w_b=w_b, uq=uq_r, ukv=_cast_rows(w_ukv_all, layer),
                w_out=_cast_rows(w_out_all, layer), w_gate=w_gate, w_up=w_up,
                w_down=_cast_rows(w_down_all, layer, ffp))


def _rope_table(lp):
    meta0 = T0 - N_META
    pos = jnp.maximum(jnp.arange(lp, dtype=jnp.int32) - meta0, 0).astype(F32)
    inv_freq = ROPE_THETA ** (-jnp.arange(ROPE // 2, dtype=F32) / (ROPE // 2))
    ang = pos[:, None] * inv_freq[None, :]
    cos, sin = jnp.cos(ang), jnp.sin(ang)
    return jnp.concatenate([cos, cos, sin, sin], axis=-1)


def kernel(x, meta, g_mix_pre, w_in, conv_w, b_gates, g_mnorm, g_cq, w_uq, g_ckv, w_ukv, w_out,
           g_mix_post, g_ffn_pre, w_gu, w_down, g_ffn_post):
    batch, seq, d = x.shape
    assert batch == 1 and seq % T0 == 0
    depth = w_in.shape[0]
    lp = T0 + seq
    mw = d // 2
    dv = mw // M_HEADS
    dk = dv // 2
    hk = M_HEADS * dk
    aw = d - mw
    av = aw // A_HEADS
    ff = w_down.shape[1]
    assert Q_LORA % KV_LORA == 0 and KV_LORA % ROPE_W == 0 and ROPE_W == GATE_W and 2 * hk == mw

    bf = 512
    ffp = -(-ff // 1024) * 1024
    bm = _div_tile(lp, 1664, 128)
    bn_in = _div_tile(mw, 512, 128)
    scale = (NOPE + ROPE) ** -0.5 * LOG2_E

    ckv_blk = Q_LORA // KV_LORA
    gate_blk = (Q_LORA + KV_LORA) // GATE_W
    kr_blk = (Q_LORA + KV_LORA + GATE_W) // ROPE_W
    nb = Q_LORA + KV_LORA + GATE_W + ROPE_W

    tab = _rope_table(lp)
    h, u = _prep(x[0], meta, g_mix_pre[0][None])

    out = None
    for l in range(depth):
        w = _prep_layer_weights(l, w_in, w_uq, w_ukv, w_out, w_gu, w_down, mw=mw, hk=hk, ffp=ffp)
        a_qk = _matmul([u], [(w["w_in"], 0, 0)], mw, F32, bm=bm, bn=bn_in, name="inproj_qk")
        v_m = _matmul([u], [(w["w_in"], 0, mw // bn_in)], mw, BF16, bm=bm, bn=bn_in, name="inproj_v")
        a_o = _matmul([u], [(w["w_in"], 0, 2 * mw // bn_in)], mw, F32, bm=bm, bn=bn_in, name="inproj_o")
        b_misc = _matmul([u], [(w["w_b"], 0, 0)], nb, F32, bm=bm, bn=_div_tile(nb, 768, 128),
                         name="inproj_misc")

        gates_t = b_misc[:, Q_LORA + KV_LORA:Q_LORA + KV_LORA + 2 * M_HEADS].T
        bias_c = jnp.pad(b_gates[l], (0, GATE_W - 2 * M_HEADS))[None, :]
        bias_r = b_gates[l][:, None]
        h_m = _mlstm(a_qk, v_m, a_o, b_misc, gates_t, conv_w[l], bias_c, bias_r, g_mnorm[l][None],
                     heads=M_HEADS, dk=dk, dv=dv, gate_blk=gate_blk)

        q_a = _qproj(b_misc, g_cq[l][None], w["uq"], tab, bm=_div_tile(lp, 832, 64), hpb=4, scale=scale)
        k_a, vt_a = _kvproj(b_misc, g_ckv[l][None], w["ukv"], tab, hpb=4, av=av,
                            ckv_blk=ckv_blk, kr_blk=kr_blk)
        h_a = _attention(q_a, k_a, vt_a, heads=A_HEADS, av=av)

        mix = _matmul([h_m, h_a], [(w["w_out"], 0, 0), (w["w_out"], 1, 0)], d, F32, bm=bm,
                      bn=_div_tile(d, 512, 128), name="outproj")
        h, u = _resnorm(h, mix, g_mix_post[l][None], g_ffn_pre[l][None])

        act = _matmul([u], [(w["w_gate"], 0, 0), (w["w_up"], 0, 0)], ffp, BF16,
                      bm=bm, bn=bf, swiglu=True, name="ffn_up")
        y = _matmul([act], [(w["w_down"], 0, 0)], d, F32, bm=bm, bn=_div_tile(d, 1024, 128),
                    bk=_div_tile(ffp, 2816, 128), name="ffn_down")
        if l + 1 < depth:
            h, u = _resnorm(h, y, g_ffn_post[l][None], g_mix_pre[l + 1][None])
        else:
            out = _resnorm_final(h, y, g_ffn_post[l][None])
    return out[None]
```
